```python
import math
import jax, jax.numpy as jnp
from jax import lax
import numpy as np

D_MODEL = 1024
BATCH = 16
SEQ = 4096
DEPTH = 4

MLA_HEADS = 8
MLA_NOPE = 64
MLA_ROPE = 32
MLA_V = 64
MLA_QK = MLA_NOPE + MLA_ROPE
MLA_Q_RANK = 256
MLA_KV_RANK = 128
ROPE_THETA = 10000.0
Q_BLOCK = 128
MOBA_HEADS = 8
MOBA_HEAD_DIM = 64
MOBA_W = MOBA_HEADS * MOBA_HEAD_DIM
MOBA_BLOCK = 256
MOBA_TOPK = 3
MOBA_Q_CHUNK = 16
S5_GROUP = 16
S5_GROUPS = D_MODEL // S5_GROUP
S5_STATE = 64
S5_CHUNK = 128
DT_MIN = 1e-3
DT_MAX = 1e-1
D_FF = 4 * D_MODEL
EPS = 1e-6

N_EVEN = (DEPTH + 1) // 2
N_ODD = DEPTH // 2
IN_SIZES = [MLA_Q_RANK, MLA_KV_RANK, MLA_ROPE, MOBA_W, MOBA_W, MOBA_W]
IN_COLS = sum(IN_SIZES)
IN_SPLITS = [int(v) for v in np.cumsum(IN_SIZES)[:-1]]
MIX_WIDTH = MLA_HEADS * MLA_V + MOBA_W

kernel_name = "hybrid_mla_moba_s5_block"


def rms_norm(x, g):
    xf = x.astype(jnp.float32)
    y = xf * lax.rsqrt(jnp.mean(xf * xf, axis=-1, keepdims=True) + EPS)
    return (y * g.astype(jnp.float32)).astype(x.dtype)


def apply_rope(x, pos):
    half = x.shape[-1] // 2
    inv = ROPE_THETA ** (-jnp.arange(half, dtype=jnp.float32) / half)
    ang = pos.astype(jnp.float32)[:, None] * inv[None, :]
    cos = jnp.cos(ang).astype(x.dtype)
    sin = jnp.sin(ang).astype(x.dtype)
    x1, x2 = x[..., :half], x[..., half:]
    return jnp.concatenate([x1 * cos - x2 * sin, x1 * sin + x2 * cos], axis=-1)


def causal_attention_blocks(q, k, v):
    S = q.shape[2]
    outs = []
    for i in range(S // Q_BLOCK):
        lo, hi = i * Q_BLOCK, (i + 1) * Q_BLOCK
        s = jnp.einsum('bhqd,bhkd->bhqk', q[:, :, lo:hi], k[:, :, :hi]).astype(jnp.float32)
        mask = jnp.arange(hi)[None, :] <= jnp.arange(lo, hi)[:, None]
        p = jax.nn.softmax(jnp.where(mask, s, -jnp.inf), axis=-1).astype(v.dtype)
        outs.append(jnp.einsum('bhqk,bhkd->bhqd', p, v[:, :, :hi]))
    return jnp.concatenate(outs, axis=2)


def moba_attention(q, k, v):
    B, H, S, d = q.shape
    nb = -(-S // MOBA_BLOCK)
    pad = nb * MOBA_BLOCK - S
    kp = jnp.pad(k, ((0, 0), (0, 0), (0, pad), (0, 0)))
    vp = jnp.pad(v, ((0, 0), (0, 0), (0, pad), (0, 0)))
    k_blocks = kp.reshape(B, H, nb, MOBA_BLOCK, d)
    kv_blocks = jnp.concatenate([k_blocks, vp.reshape(B, H, nb, MOBA_BLOCK, d)], axis=-1)
    k_mean = jnp.mean(k_blocks.astype(jnp.float32), axis=3).astype(k.dtype)
    n_sel = min(MOBA_TOPK, nb - 1)
    b_idx = jnp.arange(B)[:, None, None, None]
    h_idx = jnp.arange(H)[None, :, None, None]
    blk_pos = jnp.arange(MOBA_BLOCK)

    def chunk(c):
        t0 = c * MOBA_Q_CHUNK
        qc = lax.dynamic_slice_in_dim(q, t0, MOBA_Q_CHUNK, axis=2)
        t = t0 + jnp.arange(MOBA_Q_CHUNK)
        cur = t0 // MOBA_BLOCK
        kv_own = lax.dynamic_index_in_dim(kv_blocks, cur, axis=2, keepdims=False)
        s_own = jnp.einsum('bhqd,bhld->bhql', qc, kv_own[..., :d]).astype(jnp.float32)
        s_own = jnp.where(cur * MOBA_BLOCK + blk_pos[None, :] <= t[:, None], s_own, -jnp.inf)
        if n_sel == 0:
            p = jax.nn.softmax(s_own, axis=-1).astype(v.dtype)
            return jnp.einsum('bhql,bhld->bhqd', p, kv_own[..., d:])
        gate = jnp.einsum('bhqd,bhnd->bhqn', qc, k_mean).astype(jnp.float32)
        gate = jnp.where(jnp.arange(nb) < cur, gate, -jnp.inf)
        _, sel = lax.top_k(gate, n_sel)
        sel_ok = jnp.arange(n_sel) < cur
        kv_sel = kv_blocks[b_idx, h_idx, sel]
        s_sel = jnp.einsum('bhqd,bhqnld->bhqnl', qc, kv_sel[..., :d]).astype(jnp.float32)
        s_sel = jnp.where(sel_ok[:, None], s_sel, -jnp.inf)
        s_sel = s_sel.reshape(B, H, MOBA_Q_CHUNK, n_sel * MOBA_BLOCK)
        p = jax.nn.softmax(jnp.concatenate([s_sel, s_own], axis=-1), axis=-1).astype(v.dtype)
        p_sel = p[..., :n_sel * MOBA_BLOCK].reshape(B, H, MOBA_Q_CHUNK, n_sel, MOBA_BLOCK)
        p_own = p[..., n_sel * MOBA_BLOCK:]
        return (jnp.einsum('bhqnl,bhqnld->bhqd', p_sel, kv_sel[..., d:])
                + jnp.einsum('bhql,bhld->bhqd', p_own, kv_own[..., d:]))

    outs = lax.map(chunk, jnp.arange(S // MOBA_Q_CHUNK))
    return outs.transpose(1, 2, 0, 3, 4).reshape(B, H, S, d)


def attn_mixer(h, w_in, g_cq, w_uq, g_ckv, w_ukv, g_qn_mla, g_kn_mla, g_qn_moba, g_kn_moba, w_o):
    B, S, _ = h.shape
    pos = jnp.arange(S)
    c_q, c_kv, k_r, q_b, k_b, v_b = jnp.split(h @ w_in, IN_SPLITS, axis=-1)
    q = (rms_norm(c_q, g_cq) @ w_uq).reshape(B, S, MLA_HEADS, MLA_QK).transpose(0, 2, 1, 3)
    kv = (rms_norm(c_kv, g_ckv) @ w_ukv).reshape(B, S, MLA_HEADS, MLA_NOPE + MLA_V).transpose(0, 2, 1, 3)
    k_nope, v = kv[..., :MLA_NOPE], kv[..., MLA_NOPE:]
    q = jnp.concatenate([q[..., :MLA_NOPE], apply_rope(q[..., MLA_NOPE:], pos)], axis=-1)
    k_rope = jnp.broadcast_to(apply_rope(k_r, pos)[:, None], (B, MLA_HEADS, S, MLA_ROPE))
    k = jnp.concatenate([k_nope, k_rope], axis=-1)
    q = rms_norm(q, g_qn_mla) * (MLA_QK ** -0.5)
    k = rms_norm(k, g_kn_mla)
    o_mla = causal_attention_blocks(q, k, v)
    def heads(t):
        return t.reshape(B, S, MOBA_HEADS, MOBA_HEAD_DIM).transpose(0, 2, 1, 3)
    qm = rms_norm(heads(q_b), g_qn_moba) * (MOBA_HEAD_DIM ** -0.5)
    km = rms_norm(heads(k_b), g_kn_moba)
    o_moba = moba_attention(qm, km, heads(v_b))
    o = jnp.concatenate([o_mla.transpose(0, 2, 1, 3).reshape(B, S, MLA_HEADS * MLA_V),
                         o_moba.transpose(0, 2, 1, 3).reshape(B, S, MOBA_W)], axis=-1)
    return o @ w_o


def s5_mixer(h, lam_re, lam_im, log_dt, b_re, b_im, c_re, c_im, d_skip, w_glu):
    B, S, D = h.shape
    f32 = jnp.float32
    u = h.astype(f32)
    lam = lax.complex(lam_re.astype(f32), lam_im.astype(f32))
    dt = jnp.exp(log_dt.astype(f32))[:, None]
    a_bar = jnp.exp(lam * dt)
    b_bar = ((a_bar - 1.0) / lam)[..., None] * lax.complex(b_re.astype(f32), b_im.astype(f32))
    cmat = lax.complex(c_re.astype(f32), c_im.astype(f32))
    nc = S // S5_CHUNK
    u_chunks = u.reshape(B, nc, S5_CHUNK, S5_GROUPS, S5_GROUP).transpose(1, 2, 0, 3, 4)
    steps = jnp.arange(1, S5_CHUNK + 1, dtype=f32)
    a_pow = jnp.exp(lam[None] * dt[None] * steps[:, None, None])
    a_el = jnp.broadcast_to(a_bar, (S5_CHUNK, 1, S5_GROUPS, S5_STATE))

    def combine(e1, e2):
        a1, b1 = e1
        a2, b2 = e2
        return a1 * a2, a2 * b1 + b2

    def step(state, uc):
        bu = jnp.einsum('gpi,lbgi->lbgp', b_bar, uc.astype(jnp.complex64))
        _, hs = lax.associative_scan(combine, (a_el, bu), axis=0)
        hs = hs + a_pow[:, None] * state[None]
        y = jnp.einsum('gop,lbgp->lbgo', cmat, hs).real
        return hs[-1], y

    state0 = jnp.zeros((B, S5_GROUPS, S5_STATE), jnp.complex64)
    _, ys = lax.scan(step, state0, u_chunks)
    y = ys.transpose(2, 0, 1, 3, 4).reshape(B, S, D) + d_skip.astype(f32) * u
    g = jax.nn.gelu(y).astype(h.dtype)
    val, gate = jnp.split(g @ w_glu, 2, axis=-1)
    return val * jax.nn.sigmoid(gate)


def sq_relu_mlp(h, w1, w2):
    return jnp.square(jax.nn.relu(h @ w1)) @ w2


def setup_inputs(seed: int = 0) -> dict:
    key = jax.random.key(seed)
    ks = iter(jax.random.split(key, 32))
    f32 = jnp.float32

    def nrm(shape, scale):
        return jax.random.normal(next(ks), shape, f32) * scale

    def gain(shape):
        return 1.0 + 0.02 * jax.random.normal(next(ks), shape, f32)

    n_idx = jnp.arange(S5_STATE, dtype=f32)
    return {
        "x": nrm((BATCH, SEQ, D_MODEL), 1.0),
        "mix_norm_g": gain((DEPTH, D_MODEL)),
        "ffn_norm_g": gain((DEPTH, D_MODEL)),
        "w_in": nrm((N_EVEN, D_MODEL, IN_COLS), D_MODEL ** -0.5),
        "g_cq": gain((N_EVEN, MLA_Q_RANK)),
        "w_uq": nrm((N_EVEN, MLA_Q_RANK, MLA_HEADS * MLA_QK), MLA_Q_RANK ** -0.5),
        "g_ckv": gain((N_EVEN, MLA_KV_RANK)),
        "w_ukv": nrm((N_EVEN, MLA_KV_RANK, MLA_HEADS * (MLA_NOPE + MLA_V)), MLA_KV_RANK ** -0.5),
        "g_qn_mla": gain((N_EVEN, MLA_QK)),
        "g_kn_mla": gain((N_EVEN, MLA_QK)),
        "g_qn_moba": gain((N_EVEN, MOBA_HEAD_DIM)),
        "g_kn_moba": gain((N_EVEN, MOBA_HEAD_DIM)),
        "w_o": nrm((N_EVEN, MIX_WIDTH, D_MODEL), MIX_WIDTH ** -0.5),
        "lam_re": -0.5 + nrm((N_ODD, S5_GROUPS, S5_STATE), 0.01),
        "lam_im": math.pi * n_idx + nrm((N_ODD, S5_GROUPS, S5_STATE), 0.01),
        "log_dt": jax.random.uniform(next(ks), (N_ODD, S5_GROUPS), f32, math.log(DT_MIN), math.log(DT_MAX)),
        "b_re": nrm((N_ODD, S5_GROUPS, S5_STATE, S5_GROUP), (2 * S5_GROUP) ** -0.5),
        "b_im": nrm((N_ODD, S5_GROUPS, S5_STATE, S5_GROUP), (2 * S5_GROUP) ** -0.5),
        "c_re": nrm((N_ODD, S5_GROUPS, S5_GROUP, S5_STATE), (2 * S5_STATE) ** -0.5),
        "c_im": nrm((N_ODD, S5_GROUPS, S5_GROUP, S5_STATE), (2 * S5_STATE) ** -0.5),
        "d_skip": nrm((N_ODD, D_MODEL), 1.0),
        "w_glu": nrm((N_ODD, D_MODEL, 2 * D_MODEL), D_MODEL ** -0.5),
        "w_ff1": nrm((DEPTH, D_MODEL, D_FF), D_MODEL ** -0.5),
        "w_ff2": nrm((DEPTH, D_FF, D_MODEL), D_FF ** -0.5),
    }


def reference(x, mix_norm_g, ffn_norm_g, w_in, g_cq, w_uq, g_ckv, w_ukv, g_qn_mla, g_kn_mla,
              g_qn_moba, g_kn_moba, w_o, lam_re, lam_im, log_dt, b_re, b_im, c_re, c_im,
              d_skip, w_glu, w_ff1, w_ff2):
    for layer in range(DEPTH):
        h = rms_norm(x, mix_norm_g[layer])
        i = layer // 2
        if layer % 2 == 0:
            x = x + attn_mixer(h, w_in[i], g_cq[i], w_uq[i], g_ckv[i], w_ukv[i], g_qn_mla[i],
                               g_kn_mla[i], g_qn_moba[i], g_kn_moba[i], w_o[i])
        else:
            x = x + s5_mixer(h, lam_re[i], lam_im[i], log_dt[i], b_re[i], b_im[i], c_re[i],
                             c_im[i], d_skip[i], w_glu[i]).astype(x.dtype)
        x = x + sq_relu_mlp(rms_norm(x, ffn_norm_g[layer]), w_ff1[layer], w_ff2[layer])
    return x
```

```python
import functools
import math

import jax
import jax.numpy as jnp
from jax import lax
from jax.experimental import pallas as pl
from jax.experimental.pallas import tpu as pltpu

F32 = jnp.float32
BF16 = jnp.bfloat16

D_MODEL = 1024
D_FF = 4 * D_MODEL
EPS = 1e-6
MLA_HEADS = 8
MLA_NOPE = 64
MLA_ROPE = 32
MLA_V = 64
MLA_QK = MLA_NOPE + MLA_ROPE
MLA_Q_RANK = 256
MLA_KV_RANK = 128
ROPE_THETA = 10000.0
MOBA_HEADS = 8
MOBA_HEAD_DIM = 64
MOBA_W = MOBA_HEADS * MOBA_HEAD_DIM
MOBA_BLOCK = 256
MOBA_TOPK = 3
S5_GROUP = 16
S5_GROUPS = D_MODEL // S5_GROUP
S5_STATE = 64

LANE = 128
HEAD_SLOT = 128
N_HEADS = MLA_HEADS + MOBA_HEADS
MASK_VALUE = -1e30
ATT_TILE = 512
FFN_TILE = 512
FF_CHUNK = 1024
S5_L = 8
S5_SLAB = LANE // S5_GROUP
N_SLABS = D_MODEL // LANE
S5_ROWS = 512
VMEM_LIMIT = 56 * 1024 * 1024

_NT = (((1,), (1,)), ((), ()))


def _dot(a, b, precision=None):
    return jnp.dot(a, b, preferred_element_type=F32, precision=precision)


def _dot_nt(a, b):
    return lax.dot_general(a, b, _NT, preferred_element_type=F32)


def _rms_rows(x, gain_row):
    ms = jnp.mean(x * x, axis=-1, keepdims=True)
    return x * lax.rsqrt(ms + EPS) * gain_row


def _const_spec(shape):
    nd = len(shape)
    return pl.BlockSpec(shape, lambda *_: (0,) * nd, pipeline_mode=pl.Buffered(1))


def _attn_proj_kernel(x_ref, gmix_ref, wstd_ref, wt_ref, gcq_ref, gckv_ref, wuqt_ref, wukv_ref,
                      wukvvt_ref, cost_ref, sint_ref, cosr_ref, sinr_ref, gqmla_ref, gkmla_ref,
                      gqmoba_ref, gkmoba_ref, qt_ref, k_ref, vt_ref, km_ref, *, tile):
    ti = pl.program_id(1)
    nsub = tile // MOBA_BLOCK
    x = x_ref[0]
    hb = _rms_rows(x, gmix_ref[...]).astype(BF16)
    pstd = _dot(hb, wstd_ref[...])
    pt = _dot_nt(wt_ref[...], hb)

    o = 0
    cq = pstd[:, o:o + MLA_Q_RANK]; o += MLA_Q_RANK
    ckv = pstd[:, o:o + MLA_KV_RANK]; o += MLA_KV_RANK
    kr_a = pstd[:, o:o + HEAD_SLOT]; o += HEAD_SLOT
    kr_b = pstd[:, o:o + HEAD_SLOT]; o += HEAD_SLOT
    kb_off = o

    cqn = _rms_rows(cq, gcq_ref[...]).astype(BF16)
    ckvn = _rms_rows(ckv, gckv_ref[...]).astype(BF16)
    qup_t = _dot_nt(wuqt_ref[...], cqn)
    kv_std = _dot(ckvn, wukv_ref[...])
    vt_ref[0, 0, 0:MLA_HEADS * MLA_V, :] = _dot_nt(wukvvt_ref[...], ckvn).astype(BF16)
    vt_ref[0, 0, MLA_HEADS * MLA_V:, :] = pt[MOBA_W:, :].astype(BF16)
    k_rope = kr_a * cosr_ref[...] + kr_b * sinr_ref[...]

    cos_t = cost_ref[...]
    sin_t = sint_ref[...]
    gq_mla = gqmla_ref[...]
    gk_mla = gkmla_ref[...]
    pad_q = jnp.zeros((HEAD_SLOT - MLA_QK, tile), F32)
    for h in range(MLA_HEADS):
        blk = qup_t[h * HEAD_SLOT:(h + 1) * HEAD_SLOT]
        nope = blk[0:MLA_NOPE]
        rope = (blk[MLA_NOPE:MLA_QK] * cos_t + blk[MLA_QK:MLA_QK + MLA_ROPE] * sin_t)
        ssq = (jnp.sum(nope * nope, axis=0, keepdims=True)
               + jnp.sum(rope * rope, axis=0, keepdims=True))
        r = lax.rsqrt(ssq * (1.0 / MLA_QK) + EPS)
        qn = jnp.concatenate([nope * r * gq_mla[0:MLA_NOPE], rope * r * gq_mla[MLA_NOPE:MLA_QK],
                              pad_q], axis=0)
        qt_ref[0, h] = qn.astype(BF16)
        kh = kv_std[:, h * HEAD_SLOT:(h + 1) * HEAD_SLOT] + k_rope
        ssk = jnp.sum(kh * kh, axis=-1, keepdims=True)
        k_ref[0, h] = (kh * lax.rsqrt(ssk * (1.0 / MLA_QK) + EPS) * gk_mla).astype(BF16)

    @pl.when(ti == 0)
    def _():
        km_ref[...] = jnp.zeros_like(km_ref)

    gq_moba = gqmoba_ref[...]
    gk_moba = gkmoba_ref[...]
    nb = km_ref.shape[1]
    row_blk = ti * nsub + lax.broadcasted_iota(jnp.int32, (tile, HEAD_SLOT), 0) // MOBA_BLOCK
    lane = lax.broadcasted_iota(jnp.int32, (tile, HEAD_SLOT), 1)
    onehot = jnp.where(lane == MOBA_HEAD_DIM + row_blk, 1.0, 0.0)
    cur = (ti * tile + lax.broadcasted_iota(jnp.int32, (1, tile), 1)) // MOBA_BLOCK
    jidx = lax.broadcasted_iota(jnp.int32, (nb, tile), 0)
    pad_m = jnp.zeros((HEAD_SLOT - MOBA_HEAD_DIM - nb, tile), F32)
    lower = [jnp.where(jp < jidx, 1.0, 0.0) for jp in range(nb)]
    for h in range(MOBA_HEADS):
        kh = pstd[:, kb_off + h * HEAD_SLOT:kb_off + (h + 1) * HEAD_SLOT]
        ssk = jnp.sum(kh * kh, axis=-1, keepdims=True)
        kn = kh * lax.rsqrt(ssk * (1.0 / MOBA_HEAD_DIM) + EPS) * gk_moba
        for sb in range(nsub):
            km_ref[h, pl.ds(ti * nsub + sb, 1), :] = jnp.mean(
                kn[sb * MOBA_BLOCK:(sb + 1) * MOBA_BLOCK], axis=0, keepdims=True)
        k_ref[0, MLA_HEADS + h] = (kn + onehot).astype(BF16)

        qh = pt[h * MOBA_HEAD_DIM:(h + 1) * MOBA_HEAD_DIM]
        ssq = jnp.sum(qh * qh, axis=0, keepdims=True)
        qn = qh * lax.rsqrt(ssq * (1.0 / MOBA_HEAD_DIM) + EPS) * gq_moba
        gate = _dot(km_ref[h][:, 0:MOBA_HEAD_DIM], qn, precision=lax.Precision.HIGHEST)
        cnt = jnp.zeros((nb, tile), F32)
        for jp in range(nb):
            gj = gate[jp:jp + 1, :]
            tie = jnp.where(gj == gate, lower[jp], 0.0)
            cnt = cnt + jnp.where(jp < cur, jnp.where(gj > gate, 1.0, tie), 0.0)
        bias = jnp.where(jidx < cur, jnp.where(cnt < float(MOBA_TOPK), 0.0, MASK_VALUE),
                         jnp.where(jidx == cur, 0.0, MASK_VALUE))
        qt_ref[0, MLA_HEADS + h] = jnp.concatenate([qn, bias, pad_m], axis=0).astype(BF16)


def _attn_proj(x, gmix, p, tabs):
    B, S, D = x.shape
    T = ATT_TILE
    nt = S // T
    nb = S // MOBA_BLOCK
    consts = [gmix, p["w_std"], p["w_t"], p["g_cq"], p["g_ckv"], p["w_uq_t"], p["w_ukv_std"],
              p["w_ukv_vt"]]
    tail = [p["gq_mla"], p["gk_mla"], p["gq_moba"], p["gk_moba"]]
    in_specs = ([pl.BlockSpec((1, T, D), lambda b, t: (b, t, 0))]
                + [_const_spec(c.shape) for c in consts]
                + [pl.BlockSpec((MLA_ROPE, T), lambda b, t: (0, t)),
                   pl.BlockSpec((MLA_ROPE, T), lambda b, t: (0, t)),
                   pl.BlockSpec((T, HEAD_SLOT), lambda b, t: (t, 0)),
                   pl.BlockSpec((T, HEAD_SLOT), lambda b, t: (t, 0))]
                + [_const_spec(c.shape) for c in tail])
    out_shape = (jax.ShapeDtypeStruct((B, N_HEADS, HEAD_SLOT, S), BF16),
                 jax.ShapeDtypeStruct((B, N_HEADS, S, HEAD_SLOT), BF16),
                 jax.ShapeDtypeStruct((B, nt, N_HEADS * MLA_V, T), BF16))
    out_specs = (pl.BlockSpec((1, N_HEADS, HEAD_SLOT, T), lambda b, t: (b, 0, 0, t)),
                 pl.BlockSpec((1, N_HEADS, T, HEAD_SLOT), lambda b, t: (b, 0, t, 0)),
                 pl.BlockSpec((1, 1, N_HEADS * MLA_V, T), lambda b, t: (b, t, 0, 0)))
    return pl.pallas_call(
        functools.partial(_attn_proj_kernel, tile=T),
        grid=(B, nt),
        in_specs=in_specs,
        out_specs=out_specs,
        out_shape=out_shape,
        scratch_shapes=[pltpu.VMEM((MOBA_HEADS, nb, HEAD_SLOT), F32)],
        compiler_params=pltpu.CompilerParams(
            dimension_semantics=("arbitrary", "arbitrary"), vmem_limit_bytes=VMEM_LIMIT),
        name="attn_proj",
    )(x, *consts, tabs["cos_t"], tabs["sin_t"], tabs["cos_r"], tabs["sin_r"], *tail)


def _flash_kernel(qt_ref, k_ref, vt_ref, o_ref, *, tile):
    qi = pl.program_id(2)

    def one_head(hh):
        qt = qt_ref[0, hh]
        kd = k_ref[0, hh, pl.ds(pl.multiple_of(qi * tile, tile), tile), :]
        s = _dot(kd, qt)
        key = lax.broadcasted_iota(jnp.int32, (tile, tile), 0)
        qry = lax.broadcasted_iota(jnp.int32, (tile, tile), 1)
        s = jnp.where(key <= qry, s, MASK_VALUE)
        m = jnp.max(s, axis=0, keepdims=True)
        p = jnp.exp(s - m)
        l = jnp.sum(p, axis=0, keepdims=True)
        acc = _dot(vt_ref[0, qi], p.astype(BF16))

        def body(j, carry):
            m, l, acc = carry
            kj = k_ref[0, hh, pl.ds(pl.multiple_of(j * tile, tile), tile), :]
            s = _dot(kj, qt)
            m_new = jnp.maximum(m, jnp.max(s, axis=0, keepdims=True))
            alpha = jnp.exp(m - m_new)
            p = jnp.exp(s - m_new)
            l = alpha * l + jnp.sum(p, axis=0, keepdims=True)
            acc = alpha * acc + _dot(vt_ref[0, j], p.astype(BF16))
            return m_new, l, acc

        m, l, acc = lax.fori_loop(0, qi, body, (m, l, acc))
        return acc * (1.0 / l)

    o_a = one_head(0)
    o_b = one_head(1)
    row = lax.broadcasted_iota(jnp.int32, o_a.shape, 0)
    o_t = jnp.where(row < MLA_V, o_a, o_b)
    o_ref[0] = o_t.T.astype(BF16)


def _flash(qt, k, vt):
    B, H, _, S = qt.shape
    T = ATT_TILE
    nt = S // T
    return pl.pallas_call(
        functools.partial(_flash_kernel, tile=T),
        grid=(B, H // 2, nt),
        in_specs=[pl.BlockSpec((1, 2, HEAD_SLOT, T), lambda b, p, q: (b, p, 0, q)),
                  pl.BlockSpec((1, 2, S, HEAD_SLOT), lambda b, p, q: (b, p, 0, 0)),
                  pl.BlockSpec((1, nt, 2 * MLA_V, T), lambda b, p, q: (b, 0, p, 0))],
        out_specs=pl.BlockSpec((1, T, 2 * MLA_V), lambda b, p, q: (b, q, p)),
        out_shape=jax.ShapeDtypeStruct((B, S, H * MLA_V), BF16),
        compiler_params=pltpu.CompilerParams(
            dimension_semantics=("arbitrary", "arbitrary", "arbitrary"),
            vmem_limit_bytes=VMEM_LIMIT),
        name="flash",
    )(qt, k, vt)


def _ffn(x1, gffn_ref, w1_ref, w2_ref):
    hb = _rms_rows(x1, gffn_ref[...]).astype(BF16)
    acc = x1
    for c in range(D_FF // FF_CHUNK):
        a = _dot(hb, w1_ref[:, c * FF_CHUNK:(c + 1) * FF_CHUNK])
        a = jnp.square(jnp.maximum(a, 0.0)).astype(BF16)
        acc = acc + _dot(a, w2_ref[c * FF_CHUNK:(c + 1) * FF_CHUNK, :])
    return acc


def _emit_chunk_rows(x2, gnext_ref, u_scr, uc_ref, tile):
    u = _rms_rows(x2, gnext_ref[...])
    rows = tile // S5_L
    for j in range(N_SLABS):
        u_scr[j] = u[:, j * LANE:(j + 1) * LANE]
        for l in range(S5_L):
            uc_ref[j, 0, :, l * LANE:(l + 1) * LANE] = (
                u_scr[j, pl.ds(l, rows, stride=S5_L), :].astype(BF16))


def _attn_out_ffn_kernel(x_ref, o_ref, wo_ref, gffn_ref, w1_ref, w2_ref, gnext_ref,
                         x2_ref, uc_ref, u_scr, *, tile):
    x1 = x_ref[0] + _dot(o_ref[0], wo_ref[...])
    x2 = _ffn(x1, gffn_ref, w1_ref, w2_ref)
    x2_ref[0] = x2
    _emit_chunk_rows(x2, gnext_ref, u_scr, uc_ref, tile)


def _bgroup(B):
    return 8 if B % 8 == 0 else B


def _attn_out_ffn(x, o, wo, gffn, w1, w2, gnext):
    B, S, D = x.shape
    T = FFN_TILE
    bg = _bgroup(B)
    row_w = S5_L * LANE
    consts_a = [wo, gffn, w1, w2, gnext]
    return pl.pallas_call(
        functools.partial(_attn_out_ffn_kernel, tile=T),
        grid=(B, S // T),
        in_specs=[pl.BlockSpec((1, T, D), lambda b, t: (b, t, 0)),
                  pl.BlockSpec((1, T, D), lambda b, t: (b, t, 0))]
                 + [_const_spec(c.shape) for c in consts_a],
        out_specs=(pl.BlockSpec((1, T, D), lambda b, t: (b, t, 0)),
                   pl.BlockSpec((N_SLABS, 1, T // S5_L, row_w),
                                lambda b, t: (0, b // bg, t, b % bg))),
        out_shape=(jax.ShapeDtypeStruct((B, S, D), F32),
                   jax.ShapeDtypeStruct((N_SLABS, B // bg, S // S5_L, bg * row_w), BF16)),
        scratch_shapes=[pltpu.VMEM((N_SLABS, T, LANE), F32)],
        compiler_params=pltpu.CompilerParams(
            dimension_semantics=("arbitrary", "arbitrary"), vmem_limit_bytes=VMEM_LIMIT),
        name="attn_out_ffn",
    )(x, o, *consts_a)


def _s5_kernel(u_ref, wt_ref, min_ref, mout_ref, a_ref, y_ref, v_scr, h_scr, st_scr, *, bg):
    @pl.when(pl.program_id(2) == 0)
    def _():
        st_scr[...] = jnp.zeros_like(st_scr)

    u = u_ref[0, 0]
    rows = u.shape[0]
    half = st_scr.shape[1] // 2
    v_scr[...] = _dot(u, min_ref[0])
    a = a_ref[0]
    a_re = jnp.broadcast_to(a[:, :half], (bg, half))
    a_im = jnp.broadcast_to(a[:, half:], (bg, half))

    def step(c, carry):
        h_re, h_im = carry
        r0 = pl.multiple_of(c * bg, bg)
        h_scr[pl.ds(r0, bg), 0:half] = h_re
        h_scr[pl.ds(r0, bg), half:] = h_im
        v = v_scr[pl.ds(r0, bg), :]
        n_re = a_re * h_re - a_im * h_im + v[:, :half]
        n_im = a_re * h_im + a_im * h_re + v[:, half:]
        return n_re, n_im

    h_re, h_im = lax.fori_loop(0, rows // bg, step, (st_scr[:, :half], st_scr[:, half:]))
    st_scr[:, 0:half] = h_re
    st_scr[:, half:] = h_im
    y_ref[0, 0] = _dot(u, wt_ref[0]) + _dot(h_scr[...].astype(BF16), mout_ref[0])


def _s5_core(uc, p, bg):
    n_slab, nbg, n_chunks, w = uc.shape
    row_w = S5_L * LANE
    rows_total = n_chunks * bg
    uc = uc.reshape(n_slab, nbg, rows_total, row_w)
    R = min(S5_ROWS, rows_total)
    n_state = 2 * S5_SLAB * S5_STATE
    wspec = lambda shape: pl.BlockSpec((1,) + shape, lambda j, g, r: (j, 0, 0))
    y = pl.pallas_call(
        functools.partial(_s5_kernel, bg=bg),
        grid=(n_slab, nbg, rows_total // R),
        in_specs=[pl.BlockSpec((1, 1, R, row_w), lambda j, g, r: (j, g, r, 0)),
                  wspec((row_w, row_w)), wspec((row_w, n_state)), wspec((n_state, row_w)),
                  wspec((1, n_state))],
        out_specs=pl.BlockSpec((1, 1, R, row_w), lambda j, g, r: (j, g, r, 0)),
        out_shape=jax.ShapeDtypeStruct((n_slab, nbg, rows_total, row_w), F32),
        scratch_shapes=[pltpu.VMEM((R, n_state), F32), pltpu.VMEM((R, n_state), F32),
                        pltpu.VMEM((bg, n_state), F32)],
        compiler_params=pltpu.CompilerParams(
            dimension_semantics=("arbitrary", "arbitrary", "arbitrary"),
            vmem_limit_bytes=VMEM_LIMIT),
        name="s5_core",
    )(uc, p["w_toep"], p["m_in"], p["m_out"], p["a_chunk"])
    return y.reshape(n_slab, nbg, n_chunks, bg * row_w)


def _gelu_tanh(y):
    c = math.sqrt(2.0 / math.pi)
    return 0.5 * y * (1.0 + jnp.tanh(c * (y + 0.044715 * (y * y * y))))


def _s5_out_ffn_kernel(x_ref, yc_ref, gmix_ref, dskip_ref, wglu_ref, gffn_ref, w1_ref, w2_ref,
                       x2_ref, y_scr, *, tile):
    x = x_ref[0]
    rows = tile // S5_L
    for j in range(N_SLABS):
        for l in range(S5_L):
            y_scr[j, pl.ds(l, rows, stride=S5_L), :] = yc_ref[j, 0, :, l * LANE:(l + 1) * LANE]
    y = jnp.concatenate([y_scr[j] for j in range(N_SLABS)], axis=1)
    u = _rms_rows(x, gmix_ref[...])
    g = _gelu_tanh(y + dskip_ref[...] * u).astype(BF16)
    vg = _dot(g, wglu_ref[...])
    x1 = x + vg[:, :D_MODEL] * jax.nn.sigmoid(vg[:, D_MODEL:])
    x2_ref[0] = _ffn(x1, gffn_ref, w1_ref, w2_ref)


def _s5_out_ffn(x, yc, gmix, dskip, wglu, gffn, w1, w2):
    B, S, D = x.shape
    T = FFN_TILE
    bg = _bgroup(B)
    row_w = S5_L * LANE
    consts = [gmix, dskip, wglu, gffn, w1, w2]
    return pl.pallas_call(
        functools.partial(_s5_out_ffn_kernel, tile=T),
        grid=(B, S // T),
        in_specs=[pl.BlockSpec((1, T, D), lambda b, t: (b, t, 0)),
                  pl.BlockSpec((N_SLABS, 1, T // S5_L, row_w), lambda b, t: (0, b // bg, t, b % bg))]
                 + [_const_spec(c.shape) for c in consts],
        out_specs=pl.BlockSpec((1, T, D), lambda b, t: (b, t, 0)),
        out_shape=jax.ShapeDtypeStruct((B, S, D), F32),
        scratch_shapes=[pltpu.VMEM((N_SLABS, T, LANE), F32)],
        compiler_params=pltpu.CompilerParams(
            dimension_semantics=("arbitrary", "arbitrary"), vmem_limit_bytes=VMEM_LIMIT),
        name="s5_out_ffn",
    )(x, yc, *consts)


def _rot_cols(w):
    half = w.shape[1] // 2
    return jnp.concatenate([-w[:, half:], w[:, :half]], axis=1)


def _slot_cols(w, n_heads, width, offset=0):
    k = w.shape[0]
    w = w.reshape(k, n_heads, width)
    w = jnp.pad(w, ((0, 0), (0, 0), (offset, HEAD_SLOT - width - offset)))
    return w.reshape(k, n_heads * HEAD_SLOT)


def _attn_params(w_in, g_cq, w_uq, g_ckv, w_ukv, g_qn_mla, g_kn_mla, g_qn_moba, g_kn_moba):
    sizes = [MLA_Q_RANK, MLA_KV_RANK, MLA_ROPE, MOBA_W, MOBA_W, MOBA_W]
    offs = [0]
    for s in sizes:
        offs.append(offs[-1] + s)
    w_cq, w_ckv, w_kr, w_qb, w_kb, w_vb = [w_in[:, offs[i]:offs[i + 1]] for i in range(6)]
    w_std = jnp.concatenate([w_cq, w_ckv,
                             _slot_cols(w_kr, 1, MLA_ROPE, MLA_NOPE),
                             _slot_cols(_rot_cols(w_kr), 1, MLA_ROPE, MLA_NOPE),
                             _slot_cols(w_kb, MOBA_HEADS, MOBA_HEAD_DIM)], axis=1)
    w_t = jnp.concatenate([w_qb, w_vb], axis=1).T
    wq = w_uq.reshape(MLA_Q_RANK, MLA_HEADS, MLA_QK)
    wq_nope, wq_rope = wq[:, :, :MLA_NOPE], wq[:, :, MLA_NOPE:]
    wq_rot = jnp.concatenate([-wq_rope[:, :, MLA_ROPE // 2:], wq_rope[:, :, :MLA_ROPE // 2]], axis=2)
    w_uq_t = jnp.concatenate([wq_nope, wq_rope, wq_rot], axis=2).reshape(
        MLA_Q_RANK, MLA_HEADS * HEAD_SLOT).T
    wkv = w_ukv.reshape(MLA_KV_RANK, MLA_HEADS, MLA_NOPE + MLA_V)
    w_ukv_std = _slot_cols(wkv[:, :, :MLA_NOPE].reshape(MLA_KV_RANK, MLA_HEADS * MLA_NOPE),
                           MLA_HEADS, MLA_NOPE)
    w_ukv_vt = wkv[:, :, MLA_NOPE:].reshape(MLA_KV_RANK, MLA_HEADS * MLA_V).T
    return {
        "w_std": w_std.astype(BF16), "w_t": w_t.astype(BF16),
        "g_cq": g_cq.reshape(1, -1), "g_ckv": g_ckv.reshape(1, -1),
        "w_uq_t": w_uq_t.astype(BF16), "w_ukv_std": w_ukv_std.astype(BF16),
        "w_ukv_vt": w_ukv_vt.astype(BF16),
        "gq_mla": (g_qn_mla * (MLA_QK ** -0.5)).reshape(-1, 1),
        "gk_mla": jnp.pad(g_kn_mla, (0, HEAD_SLOT - MLA_QK)).reshape(1, -1),
        "gq_moba": (g_qn_moba * (MOBA_HEAD_DIM ** -0.5)).reshape(-1, 1),
        "gk_moba": jnp.pad(g_kn_moba, (0, HEAD_SLOT - MOBA_HEAD_DIM)).reshape(1, -1),
    }


def _rope_tables(S):
    half = MLA_ROPE // 2
    inv = ROPE_THETA ** (-jnp.arange(half, dtype=F32) / half)
    ang = jnp.arange(S).astype(F32)[:, None] * inv[None, :]
    cos = jnp.tile(jnp.cos(ang), (1, 2))
    sin = jnp.tile(jnp.sin(ang), (1, 2))
    pad = ((0, 0), (MLA_NOPE, HEAD_SLOT - MLA_QK))
    return {"cos_t": cos.T, "sin_t": sin.T, "cos_r": jnp.pad(cos, pad), "sin_r": jnp.pad(sin, pad)}


def _s5_params(lam_re, lam_im, log_dt, b_re, b_im, c_re, c_im):
    hi = lax.Precision.HIGHEST
    G, P, L = S5_GROUPS, S5_STATE, S5_L
    dt = jnp.exp(log_dt)[:, None]

    def a_pow(s):
        mag = jnp.exp(lam_re * dt * s)
        return mag * jnp.cos(lam_im * dt * s), mag * jnp.sin(lam_im * dt * s)

    a_re, a_im = a_pow(1.0)
    den = lam_re * lam_re + lam_im * lam_im
    k_re = ((a_re - 1.0) * lam_re + a_im * lam_im) / den
    k_im = (a_im * lam_re - (a_re - 1.0) * lam_im) / den
    bb_re = k_re[..., None] * b_re - k_im[..., None] * b_im
    bb_im = k_re[..., None] * b_im + k_im[..., None] * b_re
    pows = [a_pow(float(s)) for s in range(L + 1)]

    lag = []
    for s in range(L):
        pr, pi = pows[s]
        e_re = pr[..., None] * bb_re - pi[..., None] * bb_im
        e_im = pr[..., None] * bb_im + pi[..., None] * bb_re
        lag.append(jnp.einsum("gop,gpi->goi", c_re, e_re, precision=hi)
                   - jnp.einsum("gop,gpi->goi", c_im, e_im, precision=hi))
    zero = jnp.zeros_like(lag[0])
    toep = jnp.stack([jnp.stack([lag[lo - li] if lo >= li else zero for lo in range(L)], axis=1)
                      for li in range(L)], axis=1)
    toep = toep.transpose(0, 1, 4, 2, 3)
    eye = jnp.eye(S5_SLAB, dtype=F32)
    I = S5_GROUP
    t5 = toep.reshape(N_SLABS, S5_SLAB, L, I, L, I)
    w_toep = jnp.einsum("jaxiyo,ab->jxaiybo", t5, eye).reshape(N_SLABS, L * LANE, L * LANE)

    f_re, f_im = [], []
    for l in range(L):
        pr, pi = pows[L - 1 - l]
        f_re.append(pr[..., None] * bb_re - pi[..., None] * bb_im)
        f_im.append(pr[..., None] * bb_im + pi[..., None] * bb_re)
    f = jnp.stack([jnp.stack(f_re, axis=1), jnp.stack(f_im, axis=1)], axis=0)
    f = f.reshape(2, N_SLABS, S5_SLAB, L, P, I)
    m_in = jnp.einsum("cjalpi,ab->jlaicbp", f, eye).reshape(N_SLABS, L * LANE, 2 * S5_SLAB * P)

    q_re, q_im = [], []
    for l in range(L):
        pr, pi = pows[l + 1]
        q_re.append(c_re * pr[:, None, :] - c_im * pi[:, None, :])
        q_im.append(-(c_re * pi[:, None, :] + c_im * pr[:, None, :]))
    q = jnp.stack([jnp.stack(q_re, axis=1), jnp.stack(q_im, axis=1)], axis=0)
    q = q.reshape(2, N_SLABS, S5_SLAB, L, I, P)
    m_out = jnp.einsum("cjalop,ab->jcaplbo", q, eye).reshape(N_SLABS, 2 * S5_SLAB * P, L * LANE)

    al_re, al_im = pows[L]
    a_chunk = jnp.concatenate([al_re.reshape(N_SLABS, 1, S5_SLAB * P),
                               al_im.reshape(N_SLABS, 1, S5_SLAB * P)], axis=2)
    return {"w_toep": w_toep.astype(BF16), "m_in": m_in.astype(BF16),
            "m_out": m_out.astype(BF16), "a_chunk": a_chunk}


def kernel(x, mix_norm_g, ffn_norm_g, w_in, g_cq, w_uq, g_ckv, w_ukv, g_qn_mla, g_kn_mla,
           g_qn_moba, g_kn_moba, w_o, lam_re, lam_im, log_dt, b_re, b_im, c_re, c_im,
           d_skip, w_glu, w_ff1, w_ff2):
    B, S, D = x.shape
    depth = mix_norm_g.shape[0]
    assert D == D_MODEL and S % ATT_TILE == 0 and S % FFN_TILE == 0 and depth % 2 == 0
    bg = _bgroup(B)
    tabs = _rope_tables(S)
    row = lambda v: v.reshape(1, -1)
    for layer in range(0, depth, 2):
        i = layer // 2
        ap = _attn_params(w_in[i], g_cq[i], w_uq[i], g_ckv[i], w_ukv[i], g_qn_mla[i], g_kn_mla[i],
                          g_qn_moba[i], g_kn_moba[i])
        qt, k, vt = _attn_proj(x, row(mix_norm_g[layer]), ap, tabs)
        o = _flash(qt, k, vt)
        x, uc = _attn_out_ffn(x, o, w_o[i].astype(BF16), row(ffn_norm_g[layer]),
                              w_ff1[layer].astype(BF16), w_ff2[layer].astype(BF16),
                              row(mix_norm_g[layer + 1]))
        sp = _s5_params(lam_re[i], lam_im[i], log_dt[i], b_re[i], b_im[i], c_re[i], c_im[i])
        yc = _s5_core(uc, sp, bg)
        x = _s5_out_ffn(x, yc, row(mix_norm_g[layer + 1]), row(d_skip[i]), w_glu[i].astype(BF16),
                        row(ffn_norm_g[layer + 1]), w_ff1[layer + 1].astype(BF16),
                        w_ff2[layer + 1].astype(BF16))
    return x
```

```python
import functools
import math

import jax
import jax.numpy as jnp
from jax import lax
from jax.experimental import pallas as pl
from jax.experimental.pallas import tpu as pltpu

F32 = jnp.float32
BF16 = jnp.bfloat16

D_MODEL = 1024
D_FF = 4 * D_MODEL
EPS = 1e-6
MLA_HEADS = 8
MLA_NOPE = 64
MLA_ROPE = 32
MLA_V = 64
MLA_QK = MLA_NOPE + MLA_ROPE
MLA_Q_RANK = 256
MLA_KV_RANK = 128
ROPE_THETA = 10000.0
MOBA_HEADS = 8
MOBA_HEAD_DIM = 64
MOBA_W = MOBA_HEADS * MOBA_HEAD_DIM
MOBA_BLOCK = 256
MOBA_TOPK = 3
S5_GROUP = 16
S5_GROUPS = D_MODEL // S5_GROUP
S5_STATE = 64

LANE = 128
HEAD_SLOT = 128
N_HEADS = MLA_HEADS + MOBA_HEADS
MASK_VALUE = -1e30
SUM_ROWS = 16
LOG2E = math.log2(math.e)
ATT_TILE = 512
SCORE_ROWS = 256
FFN_TILE = 512
FF_CHUNK = 1024
S5_L = 8
S5_SLAB = LANE // S5_GROUP
N_SLABS = D_MODEL // LANE
S5_ROWS = 512
VMEM_LIMIT = 56 * 1024 * 1024

_NT = (((1,), (1,)), ((), ()))


def _dot(a, b, precision=None):
    return jnp.dot(a, b, preferred_element_type=F32, precision=precision)


def _dot_nt(a, b):
    return lax.dot_general(a, b, _NT, preferred_element_type=F32)


def _rms_rows(x, gain_row):
    ms = jnp.mean(x * x, axis=-1, keepdims=True)
    return x * lax.rsqrt(ms + EPS) * gain_row


def _const_spec(shape):
    nd = len(shape)
    return pl.BlockSpec(shape, lambda *_: (0,) * nd, pipeline_mode=pl.Buffered(1))


def _attn_proj_kernel(x_ref, gmix_ref, wstd_ref, wt_ref, gcq_ref, gckv_ref, wuqt_ref, wukv_ref,
                      wukvvt_ref, cost_ref, sint_ref, cosr_ref, sinr_ref, gqmla_ref, gkmla_ref,
                      gqmoba_ref, gkmoba_ref, qt_ref, k_ref, vt_ref, km_ref, *, tile):
    ti = pl.program_id(1)
    nsub = tile // MOBA_BLOCK
    x = x_ref[0]
    hb = _rms_rows(x, gmix_ref[...]).astype(BF16)
    pstd = _dot(hb, wstd_ref[...])
    pt = _dot_nt(wt_ref[...], hb)

    o = 0
    cq = pstd[:, o:o + MLA_Q_RANK]; o += MLA_Q_RANK
    ckv = pstd[:, o:o + MLA_KV_RANK]; o += MLA_KV_RANK
    kr_a = pstd[:, o:o + HEAD_SLOT]; o += HEAD_SLOT
    kr_b = pstd[:, o:o + HEAD_SLOT]; o += HEAD_SLOT
    kb_off = o

    cqn = _rms_rows(cq, gcq_ref[...]).astype(BF16)
    ckvn = _rms_rows(ckv, gckv_ref[...]).astype(BF16)
    qup_t = _dot_nt(wuqt_ref[...], cqn)
    kv_std = _dot(ckvn, wukv_ref[...])
    vt_ref[0, 0, 0:MLA_HEADS * MLA_V, :] = _dot_nt(wukvvt_ref[...], ckvn).astype(BF16)
    vt_ref[0, 0, MLA_HEADS * MLA_V:, :] = pt[MOBA_W:, :].astype(BF16)
    k_rope = kr_a * cosr_ref[...] + kr_b * sinr_ref[...]

    cos_t = cost_ref[...]
    sin_t = sint_ref[...]
    gq_mla = gqmla_ref[...]
    gk_mla = gkmla_ref[...]
    pad_q = jnp.zeros((HEAD_SLOT - MLA_QK, tile), F32)
    for h in range(MLA_HEADS):
        blk = qup_t[h * HEAD_SLOT:(h + 1) * HEAD_SLOT]
        nope = blk[0:MLA_NOPE]
        rope = (blk[MLA_NOPE:MLA_QK] * cos_t + blk[MLA_QK:MLA_QK + MLA_ROPE] * sin_t)
        ssq = (jnp.sum(nope * nope, axis=0, keepdims=True)
               + jnp.sum(rope * rope, axis=0, keepdims=True))
        r = lax.rsqrt(ssq * (1.0 / MLA_QK) + EPS)
        qn = jnp.concatenate([nope * r * gq_mla[0:MLA_NOPE], rope * r * gq_mla[MLA_NOPE:MLA_QK],
                              pad_q], axis=0)
        qt_ref[0, h] = qn.astype(BF16)
        kh = kv_std[:, h * HEAD_SLOT:(h + 1) * HEAD_SLOT] + k_rope
        ssk = jnp.sum(kh * kh, axis=-1, keepdims=True)
        k_ref[0, h] = (kh * lax.rsqrt(ssk * (1.0 / MLA_QK) + EPS) * gk_mla).astype(BF16)

    @pl.when(ti == 0)
    def _():
        km_ref[...] = jnp.zeros_like(km_ref)

    gq_moba = gqmoba_ref[...]
    gk_moba = gkmoba_ref[...]
    nb = km_ref.shape[1]
    row_blk = ti * nsub + lax.broadcasted_iota(jnp.int32, (tile, HEAD_SLOT), 0) // MOBA_BLOCK
    lane = lax.broadcasted_iota(jnp.int32, (tile, HEAD_SLOT), 1)
    onehot = jnp.where(lane == MOBA_HEAD_DIM + row_blk, 1.0, 0.0)
    cur = (ti * tile + lax.broadcasted_iota(jnp.int32, (1, tile), 1)) // MOBA_BLOCK
    jidx = lax.broadcasted_iota(jnp.int32, (nb, tile), 0)
    pad_m = jnp.zeros((HEAD_SLOT - MOBA_HEAD_DIM - nb, tile), F32)
    lower = [jnp.where(jp < jidx, 1.0, 0.0) for jp in range(nb)]
    for h in range(MOBA_HEADS):
        kh = pstd[:, kb_off + h * HEAD_SLOT:kb_off + (h + 1) * HEAD_SLOT]
        ssk = jnp.sum(kh * kh, axis=-1, keepdims=True)
        kn = kh * lax.rsqrt(ssk * (1.0 / MOBA_HEAD_DIM) + EPS) * gk_moba
        for sb in range(nsub):
            km_ref[h, pl.ds(ti * nsub + sb, 1), :] = jnp.mean(
                kn[sb * MOBA_BLOCK:(sb + 1) * MOBA_BLOCK], axis=0, keepdims=True)
        k_ref[0, MLA_HEADS + h] = (kn + onehot).astype(BF16)

        qh = pt[h * MOBA_HEAD_DIM:(h + 1) * MOBA_HEAD_DIM]
        ssq = jnp.sum(qh * qh, axis=0, keepdims=True)
        qn = qh * lax.rsqrt(ssq * (1.0 / MOBA_HEAD_DIM) + EPS) * gq_moba
        gate = _dot(km_ref[h][:, 0:MOBA_HEAD_DIM], qn, precision=lax.Precision.HIGHEST)
        cnt = jnp.zeros((nb, tile), F32)
        for jp in range(nb):
            gj = gate[jp:jp + 1, :]
            tie = jnp.where(gj == gate, lower[jp], 0.0)
            cnt = cnt + jnp.where(jp < cur, jnp.where(gj > gate, 1.0, tie), 0.0)
        bias = jnp.where(jidx < cur, jnp.where(cnt < float(MOBA_TOPK), 0.0, MASK_VALUE),
                         jnp.where(jidx == cur, 0.0, MASK_VALUE))
        qt_ref[0, MLA_HEADS + h] = jnp.concatenate([qn, bias, pad_m], axis=0).astype(BF16)


def _attn_proj(x, gmix, p, tabs):
    B, S, D = x.shape
    T = ATT_TILE
    nt = S // T
    nb = S // MOBA_BLOCK
    consts = [gmix, p["w_std"], p["w_t"], p["g_cq"], p["g_ckv"], p["w_uq_t"], p["w_ukv_std"],
              p["w_ukv_vt"]]
    tail = [p["gq_mla"], p["gk_mla"], p["gq_moba"], p["gk_moba"]]
    in_specs = ([pl.BlockSpec((1, T, D), lambda b, t: (b, t, 0))]
                + [_const_spec(c.shape) for c in consts]
                + [pl.BlockSpec((MLA_ROPE, T), lambda b, t: (0, t)),
                   pl.BlockSpec((MLA_ROPE, T), lambda b, t: (0, t)),
                   pl.BlockSpec((T, HEAD_SLOT), lambda b, t: (t, 0)),
                   pl.BlockSpec((T, HEAD_SLOT), lambda b, t: (t, 0))]
                + [_const_spec(c.shape) for c in tail])
    out_shape = (jax.ShapeDtypeStruct((B, N_HEADS, HEAD_SLOT, S), BF16),
                 jax.ShapeDtypeStruct((B, N_HEADS, S, HEAD_SLOT), BF16),
                 jax.ShapeDtypeStruct((B, nt, N_HEADS * MLA_V, T), BF16))
    out_specs = (pl.BlockSpec((1, N_HEADS, HEAD_SLOT, T), lambda b, t: (b, 0, 0, t)),
                 pl.BlockSpec((1, N_HEADS, T, HEAD_SLOT), lambda b, t: (b, 0, t, 0)),
                 pl.BlockSpec((1, 1, N_HEADS * MLA_V, T), lambda b, t: (b, t, 0, 0)))
    return pl.pallas_call(
        functools.partial(_attn_proj_kernel, tile=T),
        grid=(B, nt),
        in_specs=in_specs,
        out_specs=out_specs,
        out_shape=out_shape,
        scratch_shapes=[pltpu.VMEM((MOBA_HEADS, nb, HEAD_SLOT), F32)],
        compiler_params=pltpu.CompilerParams(
            dimension_semantics=("arbitrary", "arbitrary"), vmem_limit_bytes=VMEM_LIMIT),
        name="attn_proj",
    )(x, *consts, tabs["cos_t"], tabs["sin_t"], tabs["cos_r"], tabs["sin_r"], *tail)


def _flash_kernel(qt_ref, k_ref, vt_ref, o_ref, s_scr, p_scr, acc_scr, *, tile):
    qi = pl.program_id(2)
    nchunk = tile // SCORE_ROWS

    def score_chunk(hh, j, r, cmax):
        k0 = pl.multiple_of(j * tile, tile) + r * SCORE_ROWS
        s = _dot(k_ref[0, hh, pl.ds(k0, SCORE_ROWS), :], qt_ref[0, hh])
        s_scr[hh, r * SCORE_ROWS:(r + 1) * SCORE_ROWS, :] = s
        c = jnp.max(s, axis=0, keepdims=True)
        return c if cmax is None else jnp.maximum(cmax, c)

    def step(j, ms, cmaxes, produce, diagonal):
        ones = jnp.ones((SUM_ROWS, tile), BF16)
        new_ms, new_cmaxes = [], []
        for hh in range(2):
            def chunk(r):
                s = s_scr[hh, r * SCORE_ROWS:(r + 1) * SCORE_ROWS, :]
                if diagonal:
                    key = lax.broadcasted_iota(jnp.int32, s.shape, 0) + r * SCORE_ROWS
                    qry = lax.broadcasted_iota(jnp.int32, s.shape, 1)
                    s = jnp.where(key <= qry, s, MASK_VALUE)
                return s
            cmax = cmaxes[hh]
            if diagonal:
                cmax = functools.reduce(jnp.maximum, [jnp.max(chunk(r), axis=0, keepdims=True)
                                                      for r in range(nchunk)])
            m_new = jnp.maximum(ms[hh], cmax)
            nxt = None
            for r in range(nchunk):
                p_scr[hh, r * SCORE_ROWS:(r + 1) * SCORE_ROWS, :] = (
                    jnp.exp2(chunk(r) - m_new).astype(BF16))
                if produce:
                    nxt = score_chunk(hh, j + 1, r, nxt)
            v_ext = jnp.concatenate([vt_ref[0, j, hh * MLA_V:(hh + 1) * MLA_V, :], ones], axis=0)
            acc_scr[hh] = jnp.exp2(ms[hh] - m_new) * acc_scr[hh] + _dot(v_ext, p_scr[hh])
            new_ms.append(m_new)
            new_cmaxes.append(nxt)
        return new_ms, new_cmaxes

    acc_scr[...] = jnp.zeros_like(acc_scr)
    m0 = jnp.full((1, tile), MASK_VALUE, F32)
    cmax0 = []
    for hh in range(2):
        c = None
        for r in range(nchunk):
            c = score_chunk(hh, 0, r, c)
        cmax0.append(c)

    def body(t, carry):
        ms, cmaxes = step(t, carry[0:2], carry[2:4], True, False)
        return ms[0], ms[1], cmaxes[0], cmaxes[1]

    carry = lax.fori_loop(0, qi, body, (m0, m0, cmax0[0], cmax0[1]))
    step(qi, carry[0:2], carry[2:4], False, True)

    outs = []
    for hh in range(2):
        acc = acc_scr[hh]
        outs.append(acc[0:MLA_V] * (1.0 / acc[MLA_V:MLA_V + 1]))
    o_ref[0] = jnp.concatenate(outs, axis=0).T.astype(BF16)


def _flash(qt, k, vt):
    B, H, _, S = qt.shape
    T = ATT_TILE
    nt = S // T
    return pl.pallas_call(
        functools.partial(_flash_kernel, tile=T),
        grid=(B, H // 2, nt),
        in_specs=[pl.BlockSpec((1, 2, HEAD_SLOT, T), lambda b, p, q: (b, p, 0, q)),
                  pl.BlockSpec((1, 2, S, HEAD_SLOT), lambda b, p, q: (b, p, 0, 0)),
                  pl.BlockSpec((1, nt, 2 * MLA_V, T), lambda b, p, q: (b, 0, p, 0))],
        out_specs=pl.BlockSpec((1, T, 2 * MLA_V), lambda b, p, q: (b, q, p)),
        out_shape=jax.ShapeDtypeStruct((B, S, H * MLA_V), BF16),
        scratch_shapes=[pltpu.VMEM((2, T, T), F32), pltpu.VMEM((2, T, T), BF16),
                        pltpu.VMEM((2, MLA_V + SUM_ROWS, T), F32)],
        compiler_params=pltpu.CompilerParams(
            dimension_semantics=("arbitrary", "arbitrary", "arbitrary"),
            vmem_limit_bytes=VMEM_LIMIT),
        name="flash",
    )(qt, k, vt)


def _ffn(x1, gffn_ref, w1_ref, w2_ref):
    hb = _rms_rows(x1, gffn_ref[...]).astype(BF16)
    acc = x1
    for c in range(D_FF // FF_CHUNK):
        a = _dot(hb, w1_ref[:, c * FF_CHUNK:(c + 1) * FF_CHUNK])
        a = jnp.square(jnp.maximum(a, 0.0)).astype(BF16)
        acc = acc + _dot(a, w2_ref[c * FF_CHUNK:(c + 1) * FF_CHUNK, :])
    return acc


def _emit_chunk_rows(x2, gnext_ref, u_scr, uc_ref, tile):
    u = _rms_rows(x2, gnext_ref[...])
    rows = tile // S5_L
    for j in range(N_SLABS):
        u_scr[j] = u[:, j * LANE:(j + 1) * LANE]
        for l in range(S5_L):
            uc_ref[j, 0, :, l * LANE:(l + 1) * LANE] = (
                u_scr[j, pl.ds(l, rows, stride=S5_L), :].astype(BF16))


def _attn_out_ffn_kernel(x_ref, o_ref, wo_ref, gffn_ref, w1_ref, w2_ref, gnext_ref,
                         x2_ref, uc_ref, u_scr, *, tile):
    x1 = x_ref[0] + _dot(o_ref[0], wo_ref[...])
    x2 = _ffn(x1, gffn_ref, w1_ref, w2_ref)
    x2_ref[0] = x2
    _emit_chunk_rows(x2, gnext_ref, u_scr, uc_ref, tile)


def _bgroup(B):
    return 8 if B % 8 == 0 else B


def _attn_out_ffn(x, o, wo, gffn, w1, w2, gnext):
    B, S, D = x.shape
    T = FFN_TILE
    row_w = S5_L * LANE
    consts_a = [wo, gffn, w1, w2, gnext]
    return pl.pallas_call(
        functools.partial(_attn_out_ffn_kernel, tile=T),
        grid=(B, S // T),
        in_specs=[pl.BlockSpec((1, T, D), lambda b, t: (b, t, 0)),
                  pl.BlockSpec((1, T, D), lambda b, t: (b, t, 0))]
                 + [_const_spec(c.shape) for c in consts_a],
        out_specs=(pl.BlockSpec((1, T, D), lambda b, t: (b, t, 0)),
                   pl.BlockSpec((N_SLABS, 1, T // S5_L, row_w), lambda b, t: (0, b, t, 0))),
        out_shape=(jax.ShapeDtypeStruct((B, S, D), F32),
                   jax.ShapeDtypeStruct((N_SLABS, B, S // S5_L, row_w), BF16)),
        scratch_shapes=[pltpu.VMEM((N_SLABS, T, LANE), F32)],
        compiler_params=pltpu.CompilerParams(
            dimension_semantics=("arbitrary", "arbitrary"), vmem_limit_bytes=VMEM_LIMIT),
        name="attn_out_ffn",
    )(x, o, *consts_a)


def _s5_kernel(u_ref, wt_ref, min_ref, mout_ref, a_ref, y_ref, v_scr, h_scr, st_scr):
    @pl.when(pl.program_id(2) == 0)
    def _():
        st_scr[...] = jnp.zeros_like(st_scr)

    bg, rc, row_w = u_ref.shape[1:]
    n_tiles = st_scr.shape[0]
    half = n_tiles // 2
    u = u_ref[0].reshape(bg * rc, row_w)
    v = _dot(u, min_ref[0])
    for t in range(n_tiles):
        v_scr[t] = v[:, t * LANE:(t + 1) * LANE]
    a = a_ref[0]
    a_t = [jnp.broadcast_to(a[:, t * LANE:(t + 1) * LANE], (bg, LANE)) for t in range(n_tiles)]

    def step(c, h):
        new = []
        for t in range(n_tiles):
            h_scr[t, pl.ds(pl.multiple_of(c * bg, bg), bg), :] = h[t]
        for t in range(half):
            v_re = v_scr[t, pl.ds(c, bg, stride=rc), :]
            new.append(a_t[t] * h[t] - a_t[half + t] * h[half + t] + v_re)
        for t in range(half):
            v_im = v_scr[half + t, pl.ds(c, bg, stride=rc), :]
            new.append(a_t[t] * h[half + t] + a_t[half + t] * h[t] + v_im)
        return tuple(new)

    h = lax.fori_loop(0, rc, step, tuple(st_scr[t] for t in range(n_tiles)))
    for t in range(n_tiles):
        st_scr[t] = h[t]
    h_in = jnp.concatenate(
        [jnp.concatenate([h_scr[t, pl.ds(s, rc, stride=bg), :] for t in range(n_tiles)], axis=1)
         for s in range(bg)], axis=0)
    y = _dot(u, wt_ref[0]) + _dot(h_in.astype(BF16), mout_ref[0])
    y_ref[0] = y.reshape(bg, rc, row_w)


def _s5_core(uc, p):
    n_slab, B, n_chunks, row_w = uc.shape
    bg = _bgroup(B)
    rc = min(S5_ROWS // bg, n_chunks)
    n_state = 2 * S5_SLAB * S5_STATE
    n_tiles = n_state // LANE
    wspec = lambda shape: pl.BlockSpec((1,) + shape, lambda j, g, r: (j, 0, 0))
    return pl.pallas_call(
        _s5_kernel,
        grid=(n_slab, B // bg, n_chunks // rc),
        in_specs=[pl.BlockSpec((1, bg, rc, row_w), lambda j, g, r: (j, g, r, 0)),
                  wspec((row_w, row_w)), wspec((row_w, n_state)), wspec((n_state, row_w)),
                  wspec((1, n_state))],
        out_specs=pl.BlockSpec((1, bg, rc, row_w), lambda j, g, r: (j, g, r, 0)),
        out_shape=jax.ShapeDtypeStruct((n_slab, B, n_chunks, row_w), F32),
        scratch_shapes=[pltpu.VMEM((n_tiles, bg * rc, LANE), F32),
                        pltpu.VMEM((n_tiles, bg * rc, LANE), F32),
                        pltpu.VMEM((n_tiles, bg, LANE), F32)],
        compiler_params=pltpu.CompilerParams(
            dimension_semantics=("arbitrary", "arbitrary", "arbitrary"),
            vmem_limit_bytes=VMEM_LIMIT),
        name="s5_core",
    )(uc, p["w_toep"], p["m_in"], p["m_out"], p["a_chunk"])


def _gelu_tanh(y):
    c = math.sqrt(2.0 / math.pi)
    return 0.5 * y * (1.0 + jnp.tanh(c * (y + 0.044715 * (y * y * y))))


def _s5_out_ffn_kernel(x_ref, yc_ref, gmix_ref, dskip_ref, wglu_ref, gffn_ref, w1_ref, w2_ref,
                       x2_ref, y_scr, *, tile):
    x = x_ref[0]
    rows = tile // S5_L
    for j in range(N_SLABS):
        for l in range(S5_L):
            y_scr[j, pl.ds(l, rows, stride=S5_L), :] = yc_ref[j, 0, :, l * LANE:(l + 1) * LANE]
    y = jnp.concatenate([y_scr[j] for j in range(N_SLABS)], axis=1)
    u = _rms_rows(x, gmix_ref[...])
    g = _gelu_tanh(y + dskip_ref[...] * u).astype(BF16)
    vg = _dot(g, wglu_ref[...])
    x1 = x + vg[:, :D_MODEL] * jax.nn.sigmoid(vg[:, D_MODEL:])
    x2_ref[0] = _ffn(x1, gffn_ref, w1_ref, w2_ref)


def _s5_out_ffn(x, yc, gmix, dskip, wglu, gffn, w1, w2):
    B, S, D = x.shape
    T = FFN_TILE
    row_w = S5_L * LANE
    consts = [gmix, dskip, wglu, gffn, w1, w2]
    return pl.pallas_call(
        functools.partial(_s5_out_ffn_kernel, tile=T),
        grid=(B, S // T),
        in_specs=[pl.BlockSpec((1, T, D), lambda b, t: (b, t, 0)),
                  pl.BlockSpec((N_SLABS, 1, T // S5_L, row_w), lambda b, t: (0, b, t, 0))]
                 + [_const_spec(c.shape) for c in consts],
        out_specs=pl.BlockSpec((1, T, D), lambda b, t: (b, t, 0)),
        out_shape=jax.ShapeDtypeStruct((B, S, D), F32),
        scratch_shapes=[pltpu.VMEM((N_SLABS, T, LANE), F32)],
        compiler_params=pltpu.CompilerParams(
            dimension_semantics=("arbitrary", "arbitrary"), vmem_limit_bytes=VMEM_LIMIT),
        name="s5_out_ffn",
    )(x, yc, *consts)


def _rot_cols(w):
    half = w.shape[1] // 2
    return jnp.concatenate([-w[:, half:], w[:, :half]], axis=1)


def _slot_cols(w, n_heads, width, offset=0):
    k = w.shape[0]
    w = w.reshape(k, n_heads, width)
    w = jnp.pad(w, ((0, 0), (0, 0), (offset, HEAD_SLOT - width - offset)))
    return w.reshape(k, n_heads * HEAD_SLOT)


def _attn_params(w_in, g_cq, w_uq, g_ckv, w_ukv, g_qn_mla, g_kn_mla, g_qn_moba, g_kn_moba):
    sizes = [MLA_Q_RANK, MLA_KV_RANK, MLA_ROPE, MOBA_W, MOBA_W, MOBA_W]
    offs = [0]
    for s in sizes:
        offs.append(offs[-1] + s)
    w_cq, w_ckv, w_kr, w_qb, w_kb, w_vb = [w_in[:, offs[i]:offs[i + 1]] for i in range(6)]
    w_std = jnp.concatenate([w_cq, w_ckv,
                             _slot_cols(w_kr, 1, MLA_ROPE, MLA_NOPE),
                             _slot_cols(_rot_cols(w_kr), 1, MLA_ROPE, MLA_NOPE),
                             _slot_cols(w_kb, MOBA_HEADS, MOBA_HEAD_DIM)], axis=1)
    w_t = jnp.concatenate([w_qb, w_vb], axis=1).T
    wq = w_uq.reshape(MLA_Q_RANK, MLA_HEADS, MLA_QK)
    wq_nope, wq_rope = wq[:, :, :MLA_NOPE], wq[:, :, MLA_NOPE:]
    wq_rot = jnp.concatenate([-wq_rope[:, :, MLA_ROPE // 2:], wq_rope[:, :, :MLA_ROPE // 2]], axis=2)
    w_uq_t = jnp.concatenate([wq_nope, wq_rope, wq_rot], axis=2).reshape(
        MLA_Q_RANK, MLA_HEADS * HEAD_SLOT).T
    wkv = w_ukv.reshape(MLA_KV_RANK, MLA_HEADS, MLA_NOPE + MLA_V)
    w_ukv_std = _slot_cols(wkv[:, :, :MLA_NOPE].reshape(MLA_KV_RANK, MLA_HEADS * MLA_NOPE),
                           MLA_HEADS, MLA_NOPE)
    w_ukv_vt = wkv[:, :, MLA_NOPE:].reshape(MLA_KV_RANK, MLA_HEADS * MLA_V).T
    return {
        "w_std": w_std.astype(BF16), "w_t": w_t.astype(BF16),
        "g_cq": g_cq.reshape(1, -1), "g_ckv": g_ckv.reshape(1, -1),
        "w_uq_t": w_uq_t.astype(BF16), "w_ukv_std": w_ukv_std.astype(BF16),
        "w_ukv_vt": w_ukv_vt.astype(BF16),
        "gq_mla": (g_qn_mla * (MLA_QK ** -0.5 * LOG2E)).reshape(-1, 1),
        "gk_mla": jnp.pad(g_kn_mla, (0, HEAD_SLOT - MLA_QK)).reshape(1, -1),
        "gq_moba": (g_qn_moba * (MOBA_HEAD_DIM ** -0.5 * LOG2E)).reshape(-1, 1),
        "gk_moba": jnp.pad(g_kn_moba, (0, HEAD_SLOT - MOBA_HEAD_DIM)).reshape(1, -1),
    }


def _rope_tables(S):
    half = MLA_ROPE // 2
    inv = ROPE_THETA ** (-jnp.arange(half, dtype=F32) / half)
    ang = jnp.arange(S).astype(F32)[:, None] * inv[None, :]
    cos = jnp.tile(jnp.cos(ang), (1, 2))
    sin = jnp.tile(jnp.sin(ang), (1, 2))
    pad = ((0, 0), (MLA_NOPE, HEAD_SLOT - MLA_QK))
    return {"cos_t": cos.T, "sin_t": sin.T, "cos_r": jnp.pad(cos, pad), "sin_r": jnp.pad(sin, pad)}


def _s5_params(lam_re, lam_im, log_dt, b_re, b_im, c_re, c_im):
    hi = lax.Precision.HIGHEST
    G, P, L = S5_GROUPS, S5_STATE, S5_L
    dt = jnp.exp(log_dt)[:, None]

    def a_pow(s):
        mag = jnp.exp(lam_re * dt * s)
        return mag * jnp.cos(lam_im * dt * s), mag * jnp.sin(lam_im * dt * s)

    a_re, a_im = a_pow(1.0)
    den = lam_re * lam_re + lam_im * lam_im
    k_re = ((a_re - 1.0) * lam_re + a_im * lam_im) / den
    k_im = (a_im * lam_re - (a_re - 1.0) * lam_im) / den
    bb_re = k_re[..., None] * b_re - k_im[..., None] * b_im
    bb_im = k_re[..., None] * b_im + k_im[..., None] * b_re
    pows = [a_pow(float(s)) for s in range(L + 1)]

    lag = []
    for s in range(L):
        pr, pi = pows[s]
        e_re = pr[..., None] * bb_re - pi[..., None] * bb_im
        e_im = pr[..., None] * bb_im + pi[..., None] * bb_re
        lag.append(jnp.einsum("gop,gpi->goi", c_re, e_re, precision=hi)
                   - jnp.einsum("gop,gpi->goi", c_im, e_im, precision=hi))
    zero = jnp.zeros_like(lag[0])
    toep = jnp.stack([jnp.stack([lag[lo - li] if lo >= li else zero for lo in range(L)], axis=1)
                      for li in range(L)], axis=1)
    toep = toep.transpose(0, 1, 4, 2, 3)
    eye = jnp.eye(S5_SLAB, dtype=F32)
    I = S5_GROUP
    t5 = toep.reshape(N_SLABS, S5_SLAB, L, I, L, I)
    w_toep = jnp.einsum("jaxiyo,ab->jxaiybo", t5, eye).reshape(N_SLABS, L * LANE, L * LANE)

    f_re, f_im = [], []
    for l in range(L):
        pr, pi = pows[L - 1 - l]
        f_re.append(pr[..., None] * bb_re - pi[..., None] * bb_im)
        f_im.append(pr[..., None] * bb_im + pi[..., None] * bb_re)
    f = jnp.stack([jnp.stack(f_re, axis=1), jnp.stack(f_im, axis=1)], axis=0)
    f = f.reshape(2, N_SLABS, S5_SLAB, L, P, I)
    m_in = jnp.einsum("cjalpi,ab->jlaicbp", f, eye).reshape(N_SLABS, L * LANE, 2 * S5_SLAB * P)

    q_re, q_im = [], []
    for l in range(L):
        pr, pi = pows[l + 1]
        q_re.append(c_re * pr[:, None, :] - c_im * pi[:, None, :])
        q_im.append(-(c_re * pi[:, None, :] + c_im * pr[:, None, :]))
    q = jnp.stack([jnp.stack(q_re, axis=1), jnp.stack(q_im, axis=1)], axis=0)
    q = q.reshape(2, N_SLABS, S5_SLAB, L, I, P)
    m_out = jnp.einsum("cjalop,ab->jcaplbo", q, eye).reshape(N_SLABS, 2 * S5_SLAB * P, L * LANE)

    al_re, al_im = pows[L]
    a_chunk = jnp.concatenate([al_re.reshape(N_SLABS, 1, S5_SLAB * P),
                               al_im.reshape(N_SLABS, 1, S5_SLAB * P)], axis=2)
    return {"w_toep": w_toep.astype(BF16), "m_in": m_in.astype(BF16),
            "m_out": m_out.astype(BF16), "a_chunk": a_chunk}


def kernel(x, mix_norm_g, ffn_norm_g, w_in, g_cq, w_uq, g_ckv, w_ukv, g_qn_mla, g_kn_mla,
           g_qn_moba, g_kn_moba, w_o, lam_re, lam_im, log_dt, b_re, b_im, c_re, c_im,
           d_skip, w_glu, w_ff1, w_ff2):
    B, S, D = x.shape
    depth = mix_norm_g.shape[0]
    assert D == D_MODEL and S % ATT_TILE == 0 and S % FFN_TILE == 0 and depth % 2 == 0
    tabs = _rope_tables(S)
    row = lambda v: v.reshape(1, -1)
    for layer in range(0, depth, 2):
        i = layer // 2
        ap = _attn_params(w_in[i], g_cq[i], w_uq[i], g_ckv[i], w_ukv[i], g_qn_mla[i], g_kn_mla[i],
                          g_qn_moba[i], g_kn_moba[i])
        qt, k, vt = _attn_proj(x, row(mix_norm_g[layer]), ap, tabs)
        o = _flash(qt, k, vt)
        x, uc = _attn_out_ffn(x, o, w_o[i].astype(BF16), row(ffn_norm_g[layer]),
                              w_ff1[layer].astype(BF16), w_ff2[layer].astype(BF16),
                              row(mix_norm_g[layer + 1]))
        sp = _s5_params(lam_re[i], lam_im[i], log_dt[i], b_re[i], b_im[i], c_re[i], c_im[i])
        yc = _s5_core(uc, sp)
        x = _s5_out_ffn(x, yc, row(mix_norm_g[layer + 1]), row(d_skip[i]), w_glu[i].astype(BF16),
                        row(ffn_norm_g[layer + 1]), w_ff1[layer + 1].astype(BF16),
                        w_ff2[layer + 1].astype(BF16))
    return x
```

```python
import functools
import math

import jax
import jax.numpy as jnp
from jax import lax
from jax.experimental import pallas as pl
from jax.experimental.pallas import tpu as pltpu

F32 = jnp.float32
BF16 = jnp.bfloat16

D_MODEL = 1024
D_FF = 4 * D_MODEL
EPS = 1e-6
MLA_HEADS = 8
MLA_NOPE = 64
MLA_ROPE = 32
MLA_V = 64
MLA_QK = MLA_NOPE + MLA_ROPE
MLA_Q_RANK = 256
MLA_KV_RANK = 128
ROPE_THETA = 10000.0
MOBA_HEADS = 8
MOBA_HEAD_DIM = 64
MOBA_W = MOBA_HEADS * MOBA_HEAD_DIM
MOBA_BLOCK = 256
MOBA_TOPK = 3
S5_GROUP = 16
S5_GROUPS = D_MODEL // S5_GROUP
S5_STATE = 64

LANE = 128
HEAD_SLOT = 128
N_HEADS = MLA_HEADS + MOBA_HEADS
MASK_VALUE = -1e30
SUM_ROWS = 16
LOG2E = math.log2(math.e)
ATT_TILE = 512
SCORE_ROWS = 256
FFN_TILE = 512
FF_CHUNK = 1024
S5_L = 8
S5_SLAB = LANE // S5_GROUP
N_SLABS = D_MODEL // LANE
S5_ROWS = 512
VMEM_LIMIT = 56 * 1024 * 1024

_NT = (((1,), (1,)), ((), ()))


def _dot(a, b, precision=None):
    return jnp.dot(a, b, preferred_element_type=F32, precision=precision)


def _dot_nt(a, b):
    return lax.dot_general(a, b, _NT, preferred_element_type=F32)


def _rms_rows(x, gain_row):
    ms = jnp.mean(x * x, axis=-1, keepdims=True)
    return x * lax.rsqrt(ms + EPS) * gain_row


def _const_spec(shape):
    nd = len(shape)
    return pl.BlockSpec(shape, lambda *_: (0,) * nd, pipeline_mode=pl.Buffered(1))


def _attn_proj_kernel(x_ref, gmix_ref, wstd_ref, wt_ref, gcq_ref, gckv_ref, wuqt_ref, wukv_ref,
                      wukvvt_ref, cost_ref, sint_ref, cosr_ref, sinr_ref, gqmla_ref, gkmla_ref,
                      gqmoba_ref, gkmoba_ref, qt_ref, k_ref, vt_ref, km_ref, *, tile):
    ti = pl.program_id(1)
    nsub = tile // MOBA_BLOCK
    x = x_ref[0]
    hb = _rms_rows(x, gmix_ref[...]).astype(BF16)
    pstd = _dot(hb, wstd_ref[...])
    pt = _dot_nt(wt_ref[...], hb)

    o = 0
    cq = pstd[:, o:o + MLA_Q_RANK]; o += MLA_Q_RANK
    ckv = pstd[:, o:o + MLA_KV_RANK]; o += MLA_KV_RANK
    kr_a = pstd[:, o:o + HEAD_SLOT]; o += HEAD_SLOT
    kr_b = pstd[:, o:o + HEAD_SLOT]; o += HEAD_SLOT
    kb_off = o

    cqn = _rms_rows(cq, gcq_ref[...]).astype(BF16)
    ckvn = _rms_rows(ckv, gckv_ref[...]).astype(BF16)
    qup_t = _dot_nt(wuqt_ref[...], cqn)
    kv_std = _dot(ckvn, wukv_ref[...])
    vt_ref[0, 0, 0:MLA_HEADS * MLA_V, :] = _dot_nt(wukvvt_ref[...], ckvn).astype(BF16)
    vt_ref[0, 0, MLA_HEADS * MLA_V:, :] = pt[MOBA_W:, :].astype(BF16)
    k_rope = kr_a * cosr_ref[...] + kr_b * sinr_ref[...]

    cos_t = cost_ref[...]
    sin_t = sint_ref[...]
    gq_mla = gqmla_ref[...]
    gk_mla = gkmla_ref[...]
    pad_q = jnp.zeros((HEAD_SLOT - MLA_QK, tile), F32)
    for h in range(MLA_HEADS):
        blk = qup_t[h * HEAD_SLOT:(h + 1) * HEAD_SLOT]
        nope = blk[0:MLA_NOPE]
        rope = (blk[MLA_NOPE:MLA_QK] * cos_t + blk[MLA_QK:MLA_QK + MLA_ROPE] * sin_t)
        ssq = (jnp.sum(nope * nope, axis=0, keepdims=True)
               + jnp.sum(rope * rope, axis=0, keepdims=True))
        r = lax.rsqrt(ssq * (1.0 / MLA_QK) + EPS)
        qn = jnp.concatenate([nope * r * gq_mla[0:MLA_NOPE], rope * r * gq_mla[MLA_NOPE:MLA_QK],
                              pad_q], axis=0)
        qt_ref[0, h] = qn.astype(BF16)
        kh = kv_std[:, h * HEAD_SLOT:(h + 1) * HEAD_SLOT] + k_rope
        ssk = jnp.sum(kh * kh, axis=-1, keepdims=True)
        k_ref[0, h] = (kh * lax.rsqrt(ssk * (1.0 / MLA_QK) + EPS) * gk_mla).astype(BF16)

    @pl.when(ti == 0)
    def _():
        km_ref[...] = jnp.zeros_like(km_ref)

    gq_moba = gqmoba_ref[...]
    gk_moba = gkmoba_ref[...]
    nb = km_ref.shape[1]
    row_blk = ti * nsub + lax.broadcasted_iota(jnp.int32, (tile, HEAD_SLOT), 0) // MOBA_BLOCK
    lane = lax.broadcasted_iota(jnp.int32, (tile, HEAD_SLOT), 1)
    onehot = jnp.where(lane == MOBA_HEAD_DIM + row_blk, 1.0, 0.0)
    cur = (ti * tile + lax.broadcasted_iota(jnp.int32, (1, tile), 1)) // MOBA_BLOCK
    jidx = lax.broadcasted_iota(jnp.int32, (nb, tile), 0)
    pad_m = jnp.zeros((HEAD_SLOT - MOBA_HEAD_DIM - nb, tile), F32)
    lower = [jnp.where(jp < jidx, 1.0, 0.0) for jp in range(nb)]
    for h in range(MOBA_HEADS):
        kh = pstd[:, kb_off + h * HEAD_SLOT:kb_off + (h + 1) * HEAD_SLOT]
        ssk = jnp.sum(kh * kh, axis=-1, keepdims=True)
        kn = kh * lax.rsqrt(ssk * (1.0 / MOBA_HEAD_DIM) + EPS) * gk_moba
        for sb in range(nsub):
            km_ref[h, pl.ds(ti * nsub + sb, 1), :] = jnp.mean(
                kn[sb * MOBA_BLOCK:(sb + 1) * MOBA_BLOCK], axis=0, keepdims=True)
        k_ref[0, MLA_HEADS + h] = (kn + onehot).astype(BF16)

        qh = pt[h * MOBA_HEAD_DIM:(h + 1) * MOBA_HEAD_DIM]
        ssq = jnp.sum(qh * qh, axis=0, keepdims=True)
        qn = qh * lax.rsqrt(ssq * (1.0 / MOBA_HEAD_DIM) + EPS) * gq_moba
        gate = _dot(km_ref[h][:, 0:MOBA_HEAD_DIM], qn, precision=lax.Precision.HIGHEST)
        cnt = jnp.zeros((nb, tile), F32)
        for jp in range(nb):
            gj = gate[jp:jp + 1, :]
            tie = jnp.where(gj == gate, lower[jp], 0.0)
            cnt = cnt + jnp.where(jp < cur, jnp.where(gj > gate, 1.0, tie), 0.0)
        bias = jnp.where(jidx < cur, jnp.where(cnt < float(MOBA_TOPK), 0.0, MASK_VALUE),
                         jnp.where(jidx == cur, 0.0, MASK_VALUE))
        qt_ref[0, MLA_HEADS + h] = jnp.concatenate([qn, bias, pad_m], axis=0).astype(BF16)


def _attn_proj(x, gmix, p, tabs):
    B, S, D = x.shape
    T = ATT_TILE
    nt = S // T
    nb = S // MOBA_BLOCK
    consts = [gmix, p["w_std"], p["w_t"], p["g_cq"], p["g_ckv"], p["w_uq_t"], p["w_ukv_std"],
              p["w_ukv_vt"]]
    tail = [p["gq_mla"], p["gk_mla"], p["gq_moba"], p["gk_moba"]]
    in_specs = ([pl.BlockSpec((1, T, D), lambda b, t: (b, t, 0))]
                + [_const_spec(c.shape) for c in consts]
                + [pl.BlockSpec((MLA_ROPE, T), lambda b, t: (0, t)),
                   pl.BlockSpec((MLA_ROPE, T), lambda b, t: (0, t)),
                   pl.BlockSpec((T, HEAD_SLOT), lambda b, t: (t, 0)),
                   pl.BlockSpec((T, HEAD_SLOT), lambda b, t: (t, 0))]
                + [_const_spec(c.shape) for c in tail])
    out_shape = (jax.ShapeDtypeStruct((B, N_HEADS, HEAD_SLOT, S), BF16),
                 jax.ShapeDtypeStruct((B, N_HEADS, S, HEAD_SLOT), BF16),
                 jax.ShapeDtypeStruct((B, nt, N_HEADS * MLA_V, T), BF16))
    out_specs = (pl.BlockSpec((1, N_HEADS, HEAD_SLOT, T), lambda b, t: (b, 0, 0, t)),
                 pl.BlockSpec((1, N_HEADS, T, HEAD_SLOT), lambda b, t: (b, 0, t, 0)),
                 pl.BlockSpec((1, 1, N_HEADS * MLA_V, T), lambda b, t: (b, t, 0, 0)))
    return pl.pallas_call(
        functools.partial(_attn_proj_kernel, tile=T),
        grid=(B, nt),
        in_specs=in_specs,
        out_specs=out_specs,
        out_shape=out_shape,
        scratch_shapes=[pltpu.VMEM((MOBA_HEADS, nb, HEAD_SLOT), F32)],
        compiler_params=pltpu.CompilerParams(
            dimension_semantics=("arbitrary", "arbitrary"), vmem_limit_bytes=VMEM_LIMIT),
        name="attn_proj",
    )(x, *consts, tabs["cos_t"], tabs["sin_t"], tabs["cos_r"], tabs["sin_r"], *tail)


def _flash_kernel(qt_ref, qtn_ref, k_ref, vt_ref, o_ref, s_scr, p_scr, acc_scr, cmax_scr, *, tile):
    qi = pl.program_id(2)
    nchunk = tile // SCORE_ROWS

    def score_chunk(hh, q_ref, j, r, cmax):
        k0 = pl.multiple_of(j * tile, tile) + r * SCORE_ROWS
        s = _dot(k_ref[0, hh, pl.ds(k0, SCORE_ROWS), :], q_ref[0, hh])
        s_scr[hh, r * SCORE_ROWS:(r + 1) * SCORE_ROWS, :] = s
        c = jnp.max(s, axis=0, keepdims=True)
        return c if cmax is None else jnp.maximum(cmax, c)

    def step(j, ms, cmaxes, diagonal):
        ones = jnp.ones((SUM_ROWS, tile), BF16)
        new_ms, new_cmaxes = [], []
        for hh in range(2):
            def chunk(r):
                s = s_scr[hh, r * SCORE_ROWS:(r + 1) * SCORE_ROWS, :]
                if diagonal:
                    key = lax.broadcasted_iota(jnp.int32, s.shape, 0) + r * SCORE_ROWS
                    qry = lax.broadcasted_iota(jnp.int32, s.shape, 1)
                    s = jnp.where(key <= qry, s, MASK_VALUE)
                return s
            cmax = cmaxes[hh]
            if diagonal:
                cmax = functools.reduce(jnp.maximum, [jnp.max(chunk(r), axis=0, keepdims=True)
                                                      for r in range(nchunk)])
            m_new = jnp.maximum(ms[hh], cmax)
            nxt = None
            for r in range(nchunk):
                p_scr[hh, r * SCORE_ROWS:(r + 1) * SCORE_ROWS, :] = (
                    jnp.exp2(chunk(r) - m_new).astype(BF16))
                if diagonal:
                    nxt = score_chunk(hh, qtn_ref, 0, r, nxt)
                else:
                    nxt = score_chunk(hh, qt_ref, j + 1, r, nxt)
            v_ext = jnp.concatenate([vt_ref[0, j, hh * MLA_V:(hh + 1) * MLA_V, :], ones], axis=0)
            acc_scr[hh] = jnp.exp2(ms[hh] - m_new) * acc_scr[hh] + _dot(v_ext, p_scr[hh])
            new_ms.append(m_new)
            new_cmaxes.append(nxt)
        return new_ms, new_cmaxes

    @pl.when(qi == 0)
    def _():
        for hh in range(2):
            c = None
            for r in range(nchunk):
                c = score_chunk(hh, qt_ref, 0, r, c)
            cmax_scr[hh] = jnp.broadcast_to(c, cmax_scr.shape[1:])

    acc_scr[...] = jnp.zeros_like(acc_scr)
    m0 = jnp.full((1, tile), MASK_VALUE, F32)

    def body(t, carry):
        ms, cmaxes = step(t, carry[0:2], carry[2:4], False)
        return ms[0], ms[1], cmaxes[0], cmaxes[1]

    carry = lax.fori_loop(0, qi, body, (m0, m0, cmax_scr[0, 0:1, :], cmax_scr[1, 0:1, :]))
    _, cmaxes = step(qi, carry[0:2], carry[2:4], True)
    for hh in range(2):
        cmax_scr[hh] = jnp.broadcast_to(cmaxes[hh], cmax_scr.shape[1:])

    outs = []
    for hh in range(2):
        acc = acc_scr[hh]
        outs.append(acc[0:MLA_V] * (1.0 / acc[MLA_V:MLA_V + 1]))
    o_ref[0] = jnp.concatenate(outs, axis=0).T.astype(BF16)


def _flash(qt, k, vt):
    B, H, _, S = qt.shape
    T = ATT_TILE
    nt = S // T
    return pl.pallas_call(
        functools.partial(_flash_kernel, tile=T),
        grid=(B, H // 2, nt),
        in_specs=[pl.BlockSpec((1, 2, HEAD_SLOT, T), lambda b, p, q: (b, p, 0, q)),
                  pl.BlockSpec((1, 2, HEAD_SLOT, T),
                               lambda b, p, q: (b, p, 0, jnp.minimum(q + 1, nt - 1))),
                  pl.BlockSpec((1, 2, S, HEAD_SLOT), lambda b, p, q: (b, p, 0, 0)),
                  pl.BlockSpec((1, nt, 2 * MLA_V, T), lambda b, p, q: (b, 0, p, 0))],
        out_specs=pl.BlockSpec((1, T, 2 * MLA_V), lambda b, p, q: (b, q, p)),
        out_shape=jax.ShapeDtypeStruct((B, S, H * MLA_V), BF16),
        scratch_shapes=[pltpu.VMEM((2, T, T), F32), pltpu.VMEM((2, T, T), BF16),
                        pltpu.VMEM((2, MLA_V + SUM_ROWS, T), F32),
                        pltpu.VMEM((2, 8, T), F32)],
        compiler_params=pltpu.CompilerParams(
            dimension_semantics=("arbitrary", "arbitrary", "arbitrary"),
            vmem_limit_bytes=VMEM_LIMIT),
        name="flash",
    )(qt, qt, k, vt)


def _ffn(x1, gffn_ref, w1_ref, w2_ref):
    hb = _rms_rows(x1, gffn_ref[...]).astype(BF16)
    acc = x1
    for c in range(D_FF // FF_CHUNK):
        a = _dot(hb, w1_ref[:, c * FF_CHUNK:(c + 1) * FF_CHUNK])
        a = jnp.square(jnp.maximum(a, 0.0)).astype(BF16)
        acc = acc + _dot(a, w2_ref[c * FF_CHUNK:(c + 1) * FF_CHUNK, :])
    return acc


def _emit_chunk_rows(x2, gnext_ref, u_scr, uc_ref, tile):
    u = _rms_rows(x2, gnext_ref[...])
    rows = tile // S5_L
    for j in range(N_SLABS):
        u_scr[j] = u[:, j * LANE:(j + 1) * LANE]
        for l in range(S5_L):
            uc_ref[j, 0, :, l * LANE:(l + 1) * LANE] = (
                u_scr[j, pl.ds(l, rows, stride=S5_L), :].astype(BF16))


def _attn_out_ffn_kernel(x_ref, o_ref, wo_ref, gffn_ref, w1_ref, w2_ref, gnext_ref,
                         x2_ref, uc_ref, u_scr, *, tile):
    x1 = x_ref[0] + _dot(o_ref[0], wo_ref[...])
    x2 = _ffn(x1, gffn_ref, w1_ref, w2_ref)
    x2_ref[0] = x2
    _emit_chunk_rows(x2, gnext_ref, u_scr, uc_ref, tile)


def _bgroup(B):
    return 8 if B % 8 == 0 else B


def _attn_out_ffn(x, o, wo, gffn, w1, w2, gnext):
    B, S, D = x.shape
    T = FFN_TILE
    row_w = S5_L * LANE
    consts_a = [wo, gffn, w1, w2, gnext]
    return pl.pallas_call(
        functools.partial(_attn_out_ffn_kernel, tile=T),
        grid=(B, S // T),
        in_specs=[pl.BlockSpec((1, T, D), lambda b, t: (b, t, 0)),
                  pl.BlockSpec((1, T, D), lambda b, t: (b, t, 0))]
                 + [_const_spec(c.shape) for c in consts_a],
        out_specs=(pl.BlockSpec((1, T, D), lambda b, t: (b, t, 0)),
                   pl.BlockSpec((N_SLABS, 1, T // S5_L, row_w), lambda b, t: (0, b, t, 0))),
        out_shape=(jax.ShapeDtypeStruct((B, S, D), F32),
                   jax.ShapeDtypeStruct((N_SLABS, B, S // S5_L, row_w), BF16)),
        scratch_shapes=[pltpu.VMEM((N_SLABS, T, LANE), F32)],
        compiler_params=pltpu.CompilerParams(
            dimension_semantics=("arbitrary", "arbitrary"), vmem_limit_bytes=VMEM_LIMIT),
        name="attn_out_ffn",
    )(x, o, *consts_a)


def _s5_expand(kcat_ref, fcat_ref, qcat_ref, wt_scr, min_scr, mout_scr):
    L, I, P = S5_L, S5_GROUP, S5_STATE
    half = S5_SLAB * P

    def same_group(shape, rows_per_group, cols_per_group):
        r = lax.broadcasted_iota(jnp.int32, shape, 0) // rows_per_group
        c = lax.broadcasted_iota(jnp.int32, shape, 1) // cols_per_group
        return r == c

    def tiled(block, mask):
        return jnp.where(mask, jnp.concatenate([block] * S5_SLAB, axis=0), 0.0).astype(BF16)

    m_kk = same_group((LANE, LANE), I, I)
    m_in = same_group((LANE, half), I, P)
    m_out = same_group((half, LANE), P, I)
    zeros = jnp.zeros((LANE, LANE), BF16)
    lag_blocks = [tiled(kcat_ref[0, s], m_kk) for s in range(L)]
    for li in range(L):
        for lo in range(L):
            wt_scr[li * LANE:(li + 1) * LANE, lo * LANE:(lo + 1) * LANE] = (
                lag_blocks[lo - li] if lo >= li else zeros)
    for c in range(2):
        for l in range(L):
            min_scr[l * LANE:(l + 1) * LANE, c * half:(c + 1) * half] = tiled(fcat_ref[0, c, l], m_in)
            mout_scr[c * half:(c + 1) * half, l * LANE:(l + 1) * LANE] = tiled(qcat_ref[0, c, l], m_out)


def _s5_kernel(u_ref, kcat_ref, fcat_ref, qcat_ref, a_ref, y_ref,
               wt_scr, min_scr, mout_scr, v_scr, h_scr, st_scr):
    @pl.when((pl.program_id(1) == 0) & (pl.program_id(2) == 0))
    def _():
        _s5_expand(kcat_ref, fcat_ref, qcat_ref, wt_scr, min_scr, mout_scr)

    @pl.when(pl.program_id(2) == 0)
    def _():
        st_scr[...] = jnp.zeros_like(st_scr)

    bg, rc, row_w = u_ref.shape[1:]
    n_tiles = st_scr.shape[0]
    half = n_tiles // 2
    u = u_ref[0].reshape(bg * rc, row_w)
    v = _dot(u, min_scr[...])
    for t in range(n_tiles):
        for s in range(bg):
            v_scr[t, pl.ds(s, rc, stride=bg), :] = v[s * rc:(s + 1) * rc, t * LANE:(t + 1) * LANE]
    a = a_ref[0]
    a_t = [jnp.broadcast_to(a[:, t * LANE:(t + 1) * LANE], (bg, LANE)) for t in range(n_tiles)]

    def step(c, h):
        r0 = pl.multiple_of(c * bg, bg)
        new = []
        for t in range(n_tiles):
            h_scr[t, pl.ds(r0, bg), :] = h[t]
        for t in range(half):
            new.append(a_t[t] * h[t] - a_t[half + t] * h[half + t] + v_scr[t, pl.ds(r0, bg), :])
        for t in range(half):
            new.append(a_t[t] * h[half + t] + a_t[half + t] * h[t] + v_scr[half + t, pl.ds(r0, bg), :])
        return tuple(new)

    h = lax.fori_loop(0, rc, step, tuple(st_scr[t] for t in range(n_tiles)))
    for t in range(n_tiles):
        st_scr[t] = h[t]
    h_in = jnp.concatenate(
        [jnp.concatenate([h_scr[t, pl.ds(s, rc, stride=bg), :] for t in range(n_tiles)], axis=1)
         for s in range(bg)], axis=0)
    y = _dot(u, wt_scr[...]) + _dot(h_in.astype(BF16), mout_scr[...])
    y_ref[0] = y.reshape(bg, rc, row_w)


def _s5_core(uc, p):
    n_slab, B, n_chunks, row_w = uc.shape
    bg = _bgroup(B)
    rc = min(S5_ROWS // bg, n_chunks)
    n_state = 2 * S5_SLAB * S5_STATE
    n_tiles = n_state // LANE
    tables = [p["kcat"], p["fcat"], p["qcat"], p["a_chunk"]]

    def slab_spec(a):
        nd = a.ndim - 1
        return pl.BlockSpec((1,) + a.shape[1:], lambda j, g, r: (j,) + (0,) * nd)

    return pl.pallas_call(
        _s5_kernel,
        grid=(n_slab, B // bg, n_chunks // rc),
        in_specs=[pl.BlockSpec((1, bg, rc, row_w), lambda j, g, r: (j, g, r, 0))]
                 + [slab_spec(a) for a in tables],
        out_specs=pl.BlockSpec((1, bg, rc, row_w), lambda j, g, r: (j, g, r, 0)),
        out_shape=jax.ShapeDtypeStruct((n_slab, B, n_chunks, row_w), F32),
        scratch_shapes=[pltpu.VMEM((row_w, row_w), BF16), pltpu.VMEM((row_w, n_state), BF16),
                        pltpu.VMEM((n_state, row_w), BF16),
                        pltpu.VMEM((n_tiles, bg * rc, LANE), F32),
                        pltpu.VMEM((n_tiles, bg * rc, LANE), F32),
                        pltpu.VMEM((n_tiles, bg, LANE), F32)],
        compiler_params=pltpu.CompilerParams(
            dimension_semantics=("arbitrary", "arbitrary", "arbitrary"),
            vmem_limit_bytes=VMEM_LIMIT),
        name="s5_core",
    )(uc, *tables)


def _gelu_tanh(y):
    c = math.sqrt(2.0 / math.pi)
    return 0.5 * y * (1.0 + jnp.tanh(c * (y + 0.044715 * (y * y * y))))


def _s5_out_ffn_kernel(x_ref, yc_ref, gmix_ref, dskip_ref, wglu_ref, gffn_ref, w1_ref, w2_ref,
                       x2_ref, y_scr, *, tile):
    x = x_ref[0]
    rows = tile // S5_L
    for j in range(N_SLABS):
        for l in range(S5_L):
            y_scr[j, pl.ds(l, rows, stride=S5_L), :] = yc_ref[j, 0, :, l * LANE:(l + 1) * LANE]
    y = jnp.concatenate([y_scr[j] for j in range(N_SLABS)], axis=1)
    u = _rms_rows(x, gmix_ref[...])
    g = _gelu_tanh(y + dskip_ref[...] * u).astype(BF16)
    vg = _dot(g, wglu_ref[...])
    x1 = x + vg[:, :D_MODEL] * jax.nn.sigmoid(vg[:, D_MODEL:])
    x2_ref[0] = _ffn(x1, gffn_ref, w1_ref, w2_ref)


def _s5_out_ffn(x, yc, gmix, dskip, wglu, gffn, w1, w2):
    B, S, D = x.shape
    T = FFN_TILE
    row_w = S5_L * LANE
    consts = [gmix, dskip, wglu, gffn, w1, w2]
    return pl.pallas_call(
        functools.partial(_s5_out_ffn_kernel, tile=T),
        grid=(B, S // T),
        in_specs=[pl.BlockSpec((1, T, D), lambda b, t: (b, t, 0)),
                  pl.BlockSpec((N_SLABS, 1, T // S5_L, row_w), lambda b, t: (0, b, t, 0))]
                 + [_const_spec(c.shape) for c in consts],
        out_specs=pl.BlockSpec((1, T, D), lambda b, t: (b, t, 0)),
        out_shape=jax.ShapeDtypeStruct((B, S, D), F32),
        scratch_shapes=[pltpu.VMEM((N_SLABS, T, LANE), F32)],
        compiler_params=pltpu.CompilerParams(
            dimension_semantics=("arbitrary", "arbitrary"), vmem_limit_bytes=VMEM_LIMIT),
        name="s5_out_ffn",
    )(x, yc, *consts)


def _rot_cols(w):
    half = w.shape[1] // 2
    return jnp.concatenate([-w[:, half:], w[:, :half]], axis=1)


def _slot_cols(w, n_heads, width, offset=0):
    k = w.shape[0]
    w = w.reshape(k, n_heads, width)
    w = jnp.pad(w, ((0, 0), (0, 0), (offset, HEAD_SLOT - width - offset)))
    return w.reshape(k, n_heads * HEAD_SLOT)


def _attn_params(w_in, g_cq, w_uq, g_ckv, w_ukv, g_qn_mla, g_kn_mla, g_qn_moba, g_kn_moba):
    sizes = [MLA_Q_RANK, MLA_KV_RANK, MLA_ROPE, MOBA_W, MOBA_W, MOBA_W]
    offs = [0]
    for s in sizes:
        offs.append(offs[-1] + s)
    w_cq, w_ckv, w_kr, w_qb, w_kb, w_vb = [w_in[:, offs[i]:offs[i + 1]] for i in range(6)]
    w_std = jnp.concatenate([w_cq, w_ckv,
                             _slot_cols(w_kr, 1, MLA_ROPE, MLA_NOPE),
                             _slot_cols(_rot_cols(w_kr), 1, MLA_ROPE, MLA_NOPE),
                             _slot_cols(w_kb, MOBA_HEADS, MOBA_HEAD_DIM)], axis=1)
    w_t = jnp.concatenate([w_qb, w_vb], axis=1).T
    wq = w_uq.reshape(MLA_Q_RANK, MLA_HEADS, MLA_QK)
    wq_nope, wq_rope = wq[:, :, :MLA_NOPE], wq[:, :, MLA_NOPE:]
    wq_rot = jnp.concatenate([-wq_rope[:, :, MLA_ROPE // 2:], wq_rope[:, :, :MLA_ROPE // 2]], axis=2)
    w_uq_t = jnp.concatenate([wq_nope, wq_rope, wq_rot], axis=2).reshape(
        MLA_Q_RANK, MLA_HEADS * HEAD_SLOT).T
    wkv = w_ukv.reshape(MLA_KV_RANK, MLA_HEADS, MLA_NOPE + MLA_V)
    w_ukv_std = _slot_cols(wkv[:, :, :MLA_NOPE].reshape(MLA_KV_RANK, MLA_HEADS * MLA_NOPE),
                           MLA_HEADS, MLA_NOPE)
    w_ukv_vt = wkv[:, :, MLA_NOPE:].reshape(MLA_KV_RANK, MLA_HEADS * MLA_V).T
    return {
        "w_std": w_std.astype(BF16), "w_t": w_t.astype(BF16),
        "g_cq": g_cq.reshape(1, -1), "g_ckv": g_ckv.reshape(1, -1),
        "w_uq_t": w_uq_t.astype(BF16), "w_ukv_std": w_ukv_std.astype(BF16),
        "w_ukv_vt": w_ukv_vt.astype(BF16),
        "gq_mla": (g_qn_mla * (MLA_QK ** -0.5 * LOG2E)).reshape(-1, 1),
        "gk_mla": jnp.pad(g_kn_mla, (0, HEAD_SLOT - MLA_QK)).reshape(1, -1),
        "gq_moba": (g_qn_moba * (MOBA_HEAD_DIM ** -0.5 * LOG2E)).reshape(-1, 1),
        "gk_moba": jnp.pad(g_kn_moba, (0, HEAD_SLOT - MOBA_HEAD_DIM)).reshape(1, -1),
    }


def _rope_tables(S):
    half = MLA_ROPE // 2
    inv = ROPE_THETA ** (-jnp.arange(half, dtype=F32) / half)
    ang = jnp.arange(S).astype(F32)[:, None] * inv[None, :]
    cos = jnp.tile(jnp.cos(ang), (1, 2))
    sin = jnp.tile(jnp.sin(ang), (1, 2))
    pad = ((0, 0), (MLA_NOPE, HEAD_SLOT - MLA_QK))
    return {"cos_t": cos.T, "sin_t": sin.T, "cos_r": jnp.pad(cos, pad), "sin_r": jnp.pad(sin, pad)}


def _s5_params(lam_re, lam_im, log_dt, b_re, b_im, c_re, c_im):
    hi = lax.Precision.HIGHEST
    G, P, L = S5_GROUPS, S5_STATE, S5_L
    dt = jnp.exp(log_dt)[:, None]

    def a_pow(s):
        mag = jnp.exp(lam_re * dt * s)
        return mag * jnp.cos(lam_im * dt * s), mag * jnp.sin(lam_im * dt * s)

    a_re, a_im = a_pow(1.0)
    den = lam_re * lam_re + lam_im * lam_im
    k_re = ((a_re - 1.0) * lam_re + a_im * lam_im) / den
    k_im = (a_im * lam_re - (a_re - 1.0) * lam_im) / den
    bb_re = k_re[..., None] * b_re - k_im[..., None] * b_im
    bb_im = k_re[..., None] * b_im + k_im[..., None] * b_re
    pows = [a_pow(float(s)) for s in range(L + 1)]

    lag = []
    for s in range(L):
        pr, pi = pows[s]
        e_re = pr[..., None] * bb_re - pi[..., None] * bb_im
        e_im = pr[..., None] * bb_im + pi[..., None] * bb_re
        lag.append(jnp.einsum("gop,gpi->goi", c_re, e_re, precision=hi)
                   - jnp.einsum("gop,gpi->goi", c_im, e_im, precision=hi))
    I = S5_GROUP
    kcat = jnp.stack(lag, axis=0).reshape(L, N_SLABS, S5_SLAB, I, I)
    kcat = kcat.transpose(1, 0, 4, 2, 3).reshape(N_SLABS, L, I, S5_SLAB * I)

    f_re, f_im = [], []
    for l in range(L):
        pr, pi = pows[L - 1 - l]
        f_re.append(pr[..., None] * bb_re - pi[..., None] * bb_im)
        f_im.append(pr[..., None] * bb_im + pi[..., None] * bb_re)
    f = jnp.stack([jnp.stack(f_re, axis=1), jnp.stack(f_im, axis=1)], axis=0)
    f = f.reshape(2, N_SLABS, S5_SLAB, L, P, I)
    fcat = f.transpose(1, 0, 3, 5, 2, 4).reshape(N_SLABS, 2, L, I, S5_SLAB * P)

    q_re, q_im = [], []
    for l in range(L):
        pr, pi = pows[l + 1]
        q_re.append(c_re * pr[:, None, :] - c_im * pi[:, None, :])
        q_im.append(-(c_re * pi[:, None, :] + c_im * pr[:, None, :]))
    q = jnp.stack([jnp.stack(q_re, axis=1), jnp.stack(q_im, axis=1)], axis=0)
    q = q.reshape(2, N_SLABS, S5_SLAB, L, I, P)
    qcat = q.transpose(1, 0, 3, 5, 2, 4).reshape(N_SLABS, 2, L, P, S5_SLAB * I)

    al_re, al_im = pows[L]
    a_chunk = jnp.concatenate([al_re.reshape(N_SLABS, 1, S5_SLAB * P),
                               al_im.reshape(N_SLABS, 1, S5_SLAB * P)], axis=2)
    return {"kcat": kcat, "fcat": fcat, "qcat": qcat, "a_chunk": a_chunk}


def kernel(x, mix_norm_g, ffn_norm_g, w_in, g_cq, w_uq, g_ckv, w_ukv, g_qn_mla, g_kn_mla,
           g_qn_moba, g_kn_moba, w_o, lam_re, lam_im, log_dt, b_re, b_im, c_re, c_im,
           d_skip, w_glu, w_ff1, w_ff2):
    B, S, D = x.shape
    depth = mix_norm_g.shape[0]
    assert D == D_MODEL and S % ATT_TILE == 0 and S % FFN_TILE == 0 and depth % 2 == 0
    tabs = _rope_tables(S)
    row = lambda v: v.reshape(1, -1)
    for layer in range(0, depth, 2):
        i = layer // 2
        ap = _attn_params(w_in[i], g_cq[i], w_uq[i], g_ckv[i], w_ukv[i], g_qn_mla[i], g_kn_mla[i],
                          g_qn_moba[i], g_kn_moba[i])
        qt, k, vt = _attn_proj(x, row(mix_norm_g[layer]), ap, tabs)
        o = _flash(qt, k, vt)
        x, uc = _attn_out_ffn(x, o, w_o[i].astype(BF16), row(ffn_norm_g[layer]),
                              w_ff1[layer].astype(BF16), w_ff2[layer].astype(BF16),
                              row(mix_norm_g[layer + 1]))
        sp = _s5_params(lam_re[i], lam_im[i], log_dt[i], b_re[i], b_im[i], c_re[i], c_im[i])
        yc = _s5_core(uc, sp)
        x = _s5_out_ffn(x, yc, row(mix_norm_g[layer + 1]), row(d_skip[i]), w_glu[i].astype(BF16),
                        row(ffn_norm_g[layer + 1]), w_ff1[layer + 1].astype(BF16),
                        w_ff2[layer + 1].astype(BF16))
    return x
```

```python
import functools
import math

import jax
import jax.numpy as jnp
from jax import lax
from jax.experimental import pallas as pl
from jax.experimental.pallas import tpu as pltpu

F32 = jnp.float32
BF16 = jnp.bfloat16

D_MODEL = 1024
D_FF = 4 * D_MODEL
EPS = 1e-6
MLA_HEADS = 8
MLA_NOPE = 64
MLA_ROPE = 32
MLA_V = 64
MLA_QK = MLA_NOPE + MLA_ROPE
MLA_Q_RANK = 256
MLA_KV_RANK = 128
ROPE_THETA = 10000.0
MOBA_HEADS = 8
MOBA_HEAD_DIM = 64
MOBA_W = MOBA_HEADS * MOBA_HEAD_DIM
MOBA_BLOCK = 256
MOBA_TOPK = 3
S5_GROUP = 16
S5_GROUPS = D_MODEL // S5_GROUP
S5_STATE = 64

LANE = 128
HEAD_SLOT = 128
N_HEADS = MLA_HEADS + MOBA_HEADS
MASK_VALUE = -1e30
SUM_ROWS = 16
LOG2E = math.log2(math.e)
ATT_TILE = 512
SCORE_ROWS = 256
FFN_TILE = 512
FF_CHUNK = 1024
S5_L = 8
S5_SLAB = LANE // S5_GROUP
N_SLABS = D_MODEL // LANE
S5_ROWS = 512
VMEM_LIMIT = 56 * 1024 * 1024

_NT = (((1,), (1,)), ((), ()))


def _dot(a, b, precision=None):
    return jnp.dot(a, b, preferred_element_type=F32, precision=precision)


def _dot_nt(a, b):
    return lax.dot_general(a, b, _NT, preferred_element_type=F32)


def _rms_rows(x, gain_row):
    ms = jnp.mean(x * x, axis=-1, keepdims=True)
    return x * lax.rsqrt(ms + EPS) * gain_row


def _const_spec(shape):
    nd = len(shape)
    return pl.BlockSpec(shape, lambda *_: (0,) * nd, pipeline_mode=pl.Buffered(1))


def _attn_proj_kernel(x_ref, gmix_ref, wstd_ref, wt_ref, gcq_ref, gckv_ref, wuqt_ref, wukv_ref,
                      wukvvt_ref, cost_ref, sint_ref, cosr_ref, sinr_ref, gqmla_ref, gkmla_ref,
                      gqmoba_ref, gkmoba_ref, qt_ref, k_ref, vt_ref, km_ref, *, tile):
    ti = pl.program_id(1)
    nsub = tile // MOBA_BLOCK
    x = x_ref[0]
    hb = _rms_rows(x, gmix_ref[...]).astype(BF16)
    pstd = _dot(hb, wstd_ref[...])
    pt = _dot_nt(wt_ref[...], hb)

    o = 0
    cq = pstd[:, o:o + MLA_Q_RANK]; o += MLA_Q_RANK
    ckv = pstd[:, o:o + MLA_KV_RANK]; o += MLA_KV_RANK
    kr_a = pstd[:, o:o + HEAD_SLOT]; o += HEAD_SLOT
    kr_b = pstd[:, o:o + HEAD_SLOT]; o += HEAD_SLOT
    kb_off = o

    cqn = _rms_rows(cq, gcq_ref[...]).astype(BF16)
    ckvn = _rms_rows(ckv, gckv_ref[...]).astype(BF16)
    qup_t = _dot_nt(wuqt_ref[...], cqn)
    kv_std = _dot(ckvn, wukv_ref[...])
    vt_ref[0, 0, 0:MLA_HEADS * MLA_V, :] = _dot_nt(wukvvt_ref[...], ckvn).astype(BF16)
    vt_ref[0, 0, MLA_HEADS * MLA_V:, :] = pt[MOBA_W:, :].astype(BF16)
    k_rope = kr_a * cosr_ref[...] + kr_b * sinr_ref[...]

    cos_t = cost_ref[...]
    sin_t = sint_ref[...]
    gq_mla = gqmla_ref[...]
    gk_mla = gkmla_ref[...]
    pad_q = jnp.zeros((HEAD_SLOT - MLA_QK, tile), F32)
    for h in range(MLA_HEADS):
        blk = qup_t[h * HEAD_SLOT:(h + 1) * HEAD_SLOT]
        nope = blk[0:MLA_NOPE]
        rope = (blk[MLA_NOPE:MLA_QK] * cos_t + blk[MLA_QK:MLA_QK + MLA_ROPE] * sin_t)
        ssq = (jnp.sum(nope * nope, axis=0, keepdims=True)
               + jnp.sum(rope * rope, axis=0, keepdims=True))
        r = lax.rsqrt(ssq * (1.0 / MLA_QK) + EPS)
        qn = jnp.concatenate([nope * r * gq_mla[0:MLA_NOPE], rope * r * gq_mla[MLA_NOPE:MLA_QK],
                              pad_q], axis=0)
        qt_ref[0, h, 0] = qn.astype(BF16)
        kh = kv_std[:, h * HEAD_SLOT:(h + 1) * HEAD_SLOT] + k_rope
        ssk = jnp.sum(kh * kh, axis=-1, keepdims=True)
        k_ref[0, h] = (kh * lax.rsqrt(ssk * (1.0 / MLA_QK) + EPS) * gk_mla).astype(BF16)

    @pl.when(ti == 0)
    def _():
        km_ref[...] = jnp.zeros_like(km_ref)

    gq_moba = gqmoba_ref[...]
    gk_moba = gkmoba_ref[...]
    nb = km_ref.shape[1]
    row_blk = ti * nsub + lax.broadcasted_iota(jnp.int32, (tile, HEAD_SLOT), 0) // MOBA_BLOCK
    lane = lax.broadcasted_iota(jnp.int32, (tile, HEAD_SLOT), 1)
    onehot = jnp.where(lane == MOBA_HEAD_DIM + row_blk, 1.0, 0.0)
    cur = (ti * tile + lax.broadcasted_iota(jnp.int32, (1, tile), 1)) // MOBA_BLOCK
    jidx = lax.broadcasted_iota(jnp.int32, (nb, tile), 0)
    pad_m = jnp.zeros((HEAD_SLOT - MOBA_HEAD_DIM - nb, tile), F32)
    lower = [jnp.where(jp < jidx, 1.0, 0.0) for jp in range(nb)]
    for h in range(MOBA_HEADS):
        kh = pstd[:, kb_off + h * HEAD_SLOT:kb_off + (h + 1) * HEAD_SLOT]
        ssk = jnp.sum(kh * kh, axis=-1, keepdims=True)
        kn = kh * lax.rsqrt(ssk * (1.0 / MOBA_HEAD_DIM) + EPS) * gk_moba
        for sb in range(nsub):
            km_ref[h, pl.ds(ti * nsub + sb, 1), :] = jnp.mean(
                kn[sb * MOBA_BLOCK:(sb + 1) * MOBA_BLOCK], axis=0, keepdims=True)
        k_ref[0, MLA_HEADS + h] = (kn + onehot).astype(BF16)

        qh = pt[h * MOBA_HEAD_DIM:(h + 1) * MOBA_HEAD_DIM]
        ssq = jnp.sum(qh * qh, axis=0, keepdims=True)
        qn = qh * lax.rsqrt(ssq * (1.0 / MOBA_HEAD_DIM) + EPS) * gq_moba
        gate = _dot(km_ref[h][:, 0:MOBA_HEAD_DIM], qn, precision=lax.Precision.HIGHEST)
        cnt = jnp.zeros((nb, tile), F32)
        for jp in range(nb):
            gj = gate[jp:jp + 1, :]
            tie = jnp.where(gj == gate, lower[jp], 0.0)
            cnt = cnt + jnp.where(jp < cur, jnp.where(gj > gate, 1.0, tie), 0.0)
        bias = jnp.where(jidx < cur, jnp.where(cnt < float(MOBA_TOPK), 0.0, MASK_VALUE),
                         jnp.where(jidx == cur, 0.0, MASK_VALUE))
        qt_ref[0, MLA_HEADS + h, 0] = jnp.concatenate([qn, bias, pad_m], axis=0).astype(BF16)


def _attn_proj(x, gmix, p, tabs):
    B, S, D = x.shape
    T = ATT_TILE
    nt = S // T
    nb = S // MOBA_BLOCK
    consts = [gmix, p["w_std"], p["w_t"], p["g_cq"], p["g_ckv"], p["w_uq_t"], p["w_ukv_std"],
              p["w_ukv_vt"]]
    tail = [p["gq_mla"], p["gk_mla"], p["gq_moba"], p["gk_moba"]]
    in_specs = ([pl.BlockSpec((1, T, D), lambda b, t: (b, t, 0))]
                + [_const_spec(c.shape) for c in consts]
                + [pl.BlockSpec((MLA_ROPE, T), lambda b, t: (0, t)),
                   pl.BlockSpec((MLA_ROPE, T), lambda b, t: (0, t)),
                   pl.BlockSpec((T, HEAD_SLOT), lambda b, t: (t, 0)),
                   pl.BlockSpec((T, HEAD_SLOT), lambda b, t: (t, 0))]
                + [_const_spec(c.shape) for c in tail])
    out_shape = (jax.ShapeDtypeStruct((B, N_HEADS, nt, HEAD_SLOT, T), BF16),
                 jax.ShapeDtypeStruct((B, N_HEADS, S, HEAD_SLOT), BF16),
                 jax.ShapeDtypeStruct((B, nt, N_HEADS * MLA_V, T), BF16))
    out_specs = (pl.BlockSpec((1, N_HEADS, 1, HEAD_SLOT, T), lambda b, t: (b, 0, t, 0, 0)),
                 pl.BlockSpec((1, N_HEADS, T, HEAD_SLOT), lambda b, t: (b, 0, t, 0)),
                 pl.BlockSpec((1, 1, N_HEADS * MLA_V, T), lambda b, t: (b, t, 0, 0)))
    return pl.pallas_call(
        functools.partial(_attn_proj_kernel, tile=T),
        grid=(B, nt),
        in_specs=in_specs,
        out_specs=out_specs,
        out_shape=out_shape,
        scratch_shapes=[pltpu.VMEM((MOBA_HEADS, nb, HEAD_SLOT), F32)],
        compiler_params=pltpu.CompilerParams(
            dimension_semantics=("arbitrary", "arbitrary"), vmem_limit_bytes=VMEM_LIMIT),
        name="attn_proj",
    )(x, *consts, tabs["cos_t"], tabs["sin_t"], tabs["cos_r"], tabs["sin_r"], *tail)


def _flash_kernel(qt_ref, k_ref, vt_ref, o_ref, s_scr, p_scr, acc_scr):
    nt, _, tile = qt_ref.shape[2:]
    nchunk = tile // SCORE_ROWS

    def score_chunk(hh, qidx, j, r, cmax):
        k0 = pl.multiple_of(j * tile, tile) + r * SCORE_ROWS
        s = _dot(k_ref[0, hh, pl.ds(k0, SCORE_ROWS), :], qt_ref[0, hh, qidx])
        s_scr[hh, r * SCORE_ROWS:(r + 1) * SCORE_ROWS, :] = s
        c = jnp.max(s, axis=0, keepdims=True)
        return c if cmax is None else jnp.maximum(cmax, c)

    def step(qi, j, ms, cmaxes, diagonal):
        ones = jnp.ones((SUM_ROWS, tile), BF16)
        new_ms, new_cmaxes = [], []
        for hh in range(2):
            def chunk(r):
                s = s_scr[hh, r * SCORE_ROWS:(r + 1) * SCORE_ROWS, :]
                if diagonal:
                    key = lax.broadcasted_iota(jnp.int32, s.shape, 0) + r * SCORE_ROWS
                    qry = lax.broadcasted_iota(jnp.int32, s.shape, 1)
                    s = jnp.where(key <= qry, s, MASK_VALUE)
                return s
            cmax = cmaxes[hh]
            if diagonal:
                cmax = functools.reduce(jnp.maximum, [jnp.max(chunk(r), axis=0, keepdims=True)
                                                      for r in range(nchunk)])
            m_new = jnp.maximum(ms[hh], cmax)
            nxt = None
            for r in range(nchunk):
                p_scr[hh, r * SCORE_ROWS:(r + 1) * SCORE_ROWS, :] = (
                    jnp.exp2(chunk(r) - m_new).astype(BF16))
                if diagonal:
                    nxt = score_chunk(hh, jnp.minimum(qi + 1, nt - 1), 0, r, nxt)
                else:
                    nxt = score_chunk(hh, qi, j + 1, r, nxt)
            v_ext = jnp.concatenate([vt_ref[0, j, hh * MLA_V:(hh + 1) * MLA_V, :], ones], axis=0)
            acc_scr[hh] = jnp.exp2(ms[hh] - m_new) * acc_scr[hh] + _dot(v_ext, p_scr[hh])
            new_ms.append(m_new)
            new_cmaxes.append(nxt)
        return new_ms, new_cmaxes

    cmax0 = []
    for hh in range(2):
        c = None
        for r in range(nchunk):
            c = score_chunk(hh, 0, 0, r, c)
        cmax0.append(c)
    m0 = jnp.full((1, tile), MASK_VALUE, F32)

    def query_tile(qi, cmaxes):
        acc_scr[...] = jnp.zeros_like(acc_scr)

        def body(t, carry):
            ms, cms = step(qi, t, carry[0:2], carry[2:4], False)
            return ms[0], ms[1], cms[0], cms[1]

        carry = lax.fori_loop(0, qi, body, (m0, m0, cmaxes[0], cmaxes[1]))
        _, cms = step(qi, qi, carry[0:2], carry[2:4], True)
        outs = []
        for hh in range(2):
            acc = acc_scr[hh]
            outs.append(acc[0:MLA_V] * (1.0 / acc[MLA_V:MLA_V + 1]))
        o_ref[0, pl.ds(pl.multiple_of(qi * tile, tile), tile), :] = (
            jnp.concatenate(outs, axis=0).T.astype(BF16))
        return cms[0], cms[1]

    lax.fori_loop(0, nt, query_tile, (cmax0[0], cmax0[1]))


def _flash(qt, k, vt):
    B, H, nt, _, T = qt.shape
    S = nt * T
    return pl.pallas_call(
        _flash_kernel,
        grid=(B, H // 2),
        in_specs=[pl.BlockSpec((1, 2, nt, HEAD_SLOT, T), lambda b, p: (b, p, 0, 0, 0)),
                  pl.BlockSpec((1, 2, S, HEAD_SLOT), lambda b, p: (b, p, 0, 0)),
                  pl.BlockSpec((1, nt, 2 * MLA_V, T), lambda b, p: (b, 0, p, 0))],
        out_specs=pl.BlockSpec((1, S, 2 * MLA_V), lambda b, p: (b, 0, p)),
        out_shape=jax.ShapeDtypeStruct((B, S, H * MLA_V), BF16),
        scratch_shapes=[pltpu.VMEM((2, T, T), F32), pltpu.VMEM((2, T, T), BF16),
                        pltpu.VMEM((2, MLA_V + SUM_ROWS, T), F32)],
        compiler_params=pltpu.CompilerParams(
            dimension_semantics=("arbitrary", "arbitrary"), vmem_limit_bytes=VMEM_LIMIT),
        name="flash",
    )(qt, k, vt)


def _ffn(x1, gffn_ref, w1_ref, w2_ref):
    hb = _rms_rows(x1, gffn_ref[...]).astype(BF16)
    acc = x1
    for c in range(D_FF // FF_CHUNK):
        a = _dot(hb, w1_ref[:, c * FF_CHUNK:(c + 1) * FF_CHUNK])
        a = jnp.square(jnp.maximum(a, 0.0)).astype(BF16)
        acc = acc + _dot(a, w2_ref[c * FF_CHUNK:(c + 1) * FF_CHUNK, :])
    return acc


def _emit_chunk_rows(x2, gnext_ref, u_scr, uc_ref, tile):
    u = _rms_rows(x2, gnext_ref[...])
    rows = tile // S5_L
    for j in range(N_SLABS):
        u_scr[j] = u[:, j * LANE:(j + 1) * LANE]
        for l in range(S5_L):
            uc_ref[j, 0, :, l * LANE:(l + 1) * LANE] = (
                u_scr[j, pl.ds(l, rows, stride=S5_L), :].astype(BF16))


def _attn_out_ffn_kernel(x_ref, o_ref, wo_ref, gffn_ref, w1_ref, w2_ref, gnext_ref,
                         x2_ref, uc_ref, u_scr, *, tile):
    x1 = x_ref[0] + _dot(o_ref[0], wo_ref[...])
    x2 = _ffn(x1, gffn_ref, w1_ref, w2_ref)
    x2_ref[0] = x2
    _emit_chunk_rows(x2, gnext_ref, u_scr, uc_ref, tile)


def _bgroup(B):
    return 8 if B % 8 == 0 else B


def _attn_out_ffn(x, o, wo, gffn, w1, w2, gnext):
    B, S, D = x.shape
    T = FFN_TILE
    row_w = S5_L * LANE
    consts_a = [wo, gffn, w1, w2, gnext]
    return pl.pallas_call(
        functools.partial(_attn_out_ffn_kernel, tile=T),
        grid=(B, S // T),
        in_specs=[pl.BlockSpec((1, T, D), lambda b, t: (b, t, 0)),
                  pl.BlockSpec((1, T, D), lambda b, t: (b, t, 0))]
                 + [_const_spec(c.shape) for c in consts_a],
        out_specs=(pl.BlockSpec((1, T, D), lambda b, t: (b, t, 0)),
                   pl.BlockSpec((N_SLABS, 1, T // S5_L, row_w), lambda b, t: (0, b, t, 0))),
        out_shape=(jax.ShapeDtypeStruct((B, S, D), F32),
                   jax.ShapeDtypeStruct((N_SLABS, B, S // S5_L, row_w), BF16)),
        scratch_shapes=[pltpu.VMEM((N_SLABS, T, LANE), F32)],
        compiler_params=pltpu.CompilerParams(
            dimension_semantics=("arbitrary", "arbitrary"), vmem_limit_bytes=VMEM_LIMIT),
        name="attn_out_ffn",
    )(x, o, *consts_a)


def _s5_expand(kcat_ref, fcat_ref, qcat_ref, wt_scr, min_scr, mout_scr):
    L, I, P = S5_L, S5_GROUP, S5_STATE
    half = S5_SLAB * P

    def same_group(shape, rows_per_group, cols_per_group):
        r = lax.broadcasted_iota(jnp.int32, shape, 0) // rows_per_group
        c = lax.broadcasted_iota(jnp.int32, shape, 1) // cols_per_group
        return r == c

    def tiled(block, mask):
        return jnp.where(mask, jnp.concatenate([block] * S5_SLAB, axis=0), 0.0).astype(BF16)

    m_kk = same_group((LANE, LANE), I, I)
    m_in = same_group((LANE, half), I, P)
    m_out = same_group((half, LANE), P, I)
    zeros = jnp.zeros((LANE, LANE), BF16)
    lag_blocks = [tiled(kcat_ref[0, s], m_kk) for s in range(L)]
    for li in range(L):
        for lo in range(L):
            wt_scr[li * LANE:(li + 1) * LANE, lo * LANE:(lo + 1) * LANE] = (
                lag_blocks[lo - li] if lo >= li else zeros)
    for c in range(2):
        for l in range(L):
            min_scr[l * LANE:(l + 1) * LANE, c * half:(c + 1) * half] = tiled(fcat_ref[0, c, l], m_in)
            mout_scr[c * half:(c + 1) * half, l * LANE:(l + 1) * LANE] = tiled(qcat_ref[0, c, l], m_out)


def _s5_kernel(u_ref, kcat_ref, fcat_ref, qcat_ref, a_ref, y_ref,
               wt_scr, min_scr, mout_scr, v_scr, h_scr, st_scr):
    @pl.when((pl.program_id(1) == 0) & (pl.program_id(2) == 0))
    def _():
        _s5_expand(kcat_ref, fcat_ref, qcat_ref, wt_scr, min_scr, mout_scr)

    @pl.when(pl.program_id(2) == 0)
    def _():
        st_scr[...] = jnp.zeros_like(st_scr)

    bg, rc, row_w = u_ref.shape[1:]
    n_tiles = st_scr.shape[0]
    half = n_tiles // 2
    u = u_ref[0].reshape(bg * rc, row_w)
    v = _dot(u, min_scr[...])
    for t in range(n_tiles):
        for s in range(bg):
            v_scr[t, pl.ds(s, rc, stride=bg), :] = v[s * rc:(s + 1) * rc, t * LANE:(t + 1) * LANE]
    a = a_ref[0]
    a_t = [jnp.broadcast_to(a[:, t * LANE:(t + 1) * LANE], (bg, LANE)) for t in range(n_tiles)]

    def step(c, h):
        r0 = pl.multiple_of(c * bg, bg)
        new = []
        for t in range(n_tiles):
            h_scr[t, pl.ds(r0, bg), :] = h[t]
        for t in range(half):
            new.append(a_t[t] * h[t] - a_t[half + t] * h[half + t] + v_scr[t, pl.ds(r0, bg), :])
        for t in range(half):
            new.append(a_t[t] * h[half + t] + a_t[half + t] * h[t] + v_scr[half + t, pl.ds(r0, bg), :])
        return tuple(new)

    h = lax.fori_loop(0, rc, step, tuple(st_scr[t] for t in range(n_tiles)))
    for t in range(n_tiles):
        st_scr[t] = h[t]
    h_in = jnp.concatenate(
        [jnp.concatenate([h_scr[t, pl.ds(s, rc, stride=bg), :] for t in range(n_tiles)], axis=1)
         for s in range(bg)], axis=0)
    y = _dot(u, wt_scr[...]) + _dot(h_in.astype(BF16), mout_scr[...])
    y_ref[0] = y.reshape(bg, rc, row_w)


def _s5_core(uc, p):
    n_slab, B, n_chunks, row_w = uc.shape
    bg = _bgroup(B)
    rc = min(S5_ROWS // bg, n_chunks)
    n_state = 2 * S5_SLAB * S5_STATE
    n_tiles = n_state // LANE
    tables = [p["kcat"], p["fcat"], p["qcat"], p["a_chunk"]]

    def slab_spec(a):
        nd = a.ndim - 1
        return pl.BlockSpec((1,) + a.shape[1:], lambda j, g, r: (j,) + (0,) * nd)

    return pl.pallas_call(
        _s5_kernel,
        grid=(n_slab, B // bg, n_chunks // rc),
        in_specs=[pl.BlockSpec((1, bg, rc, row_w), lambda j, g, r: (j, g, r, 0))]
                 + [slab_spec(a) for a in tables],
        out_specs=pl.BlockSpec((1, bg, rc, row_w), lambda j, g, r: (j, g, r, 0)),
        out_shape=jax.ShapeDtypeStruct((n_slab, B, n_chunks, row_w), F32),
        scratch_shapes=[pltpu.VMEM((row_w, row_w), BF16), pltpu.VMEM((row_w, n_state), BF16),
                        pltpu.VMEM((n_state, row_w), BF16),
                        pltpu.VMEM((n_tiles, bg * rc, LANE), F32),
                        pltpu.VMEM((n_tiles, bg * rc, LANE), F32),
                        pltpu.VMEM((n_tiles, bg, LANE), F32)],
        compiler_params=pltpu.CompilerParams(
            dimension_semantics=("arbitrary", "arbitrary", "arbitrary"),
            vmem_limit_bytes=VMEM_LIMIT),
        name="s5_core",
    )(uc, *tables)


def _gelu_tanh(y):
    c = math.sqrt(2.0 / math.pi)
    return 0.5 * y * (1.0 + jnp.tanh(c * (y + 0.044715 * (y * y * y))))


def _s5_out_ffn_kernel(x_ref, yc_ref, gmix_ref, dskip_ref, wglu_ref, gffn_ref, w1_ref, w2_ref,
                       x2_ref, y_scr, *, tile):
    x = x_ref[0]
    rows = tile // S5_L
    for j in range(N_SLABS):
        for l in range(S5_L):
            y_scr[j, pl.ds(l, rows, stride=S5_L), :] = yc_ref[j, 0, :, l * LANE:(l + 1) * LANE]
    y = jnp.concatenate([y_scr[j] for j in range(N_SLABS)], axis=1)
    u = _rms_rows(x, gmix_ref[...])
    g = _gelu_tanh(y + dskip_ref[...] * u).astype(BF16)
    vg = _dot(g, wglu_ref[...])
    x1 = x + vg[:, :D_MODEL] * jax.nn.sigmoid(vg[:, D_MODEL:])
    x2_ref[0] = _ffn(x1, gffn_ref, w1_ref, w2_ref)


def _s5_out_ffn(x, yc, gmix, dskip, wglu, gffn, w1, w2):
    B, S, D = x.shape
    T = FFN_TILE
    row_w = S5_L * LANE
    consts = [gmix, dskip, wglu, gffn, w1, w2]
    return pl.pallas_call(
        functools.partial(_s5_out_ffn_kernel, tile=T),
        grid=(B, S // T),
        in_specs=[pl.BlockSpec((1, T, D), lambda b, t: (b, t, 0)),
                  pl.BlockSpec((N_SLABS, 1, T // S5_L, row_w), lambda b, t: (0, b, t, 0))]
                 + [_const_spec(c.shape) for c in consts],
        out_specs=pl.BlockSpec((1, T, D), lambda b, t: (b, t, 0)),
        out_shape=jax.ShapeDtypeStruct((B, S, D), F32),
        scratch_shapes=[pltpu.VMEM((N_SLABS, T, LANE), F32)],
        compiler_params=pltpu.CompilerParams(
            dimension_semantics=("arbitrary", "arbitrary"), vmem_limit_bytes=VMEM_LIMIT),
        name="s5_out_ffn",
    )(x, yc, *consts)


def _rot_cols(w):
    half = w.shape[1] // 2
    return jnp.concatenate([-w[:, half:], w[:, :half]], axis=1)


def _slot_cols(w, n_heads, width, offset=0):
    k = w.shape[0]
    w = w.reshape(k, n_heads, width)
    w = jnp.pad(w, ((0, 0), (0, 0), (offset, HEAD_SLOT - width - offset)))
    return w.reshape(k, n_heads * HEAD_SLOT)


def _attn_params(w_in, g_cq, w_uq, g_ckv, w_ukv, g_qn_mla, g_kn_mla, g_qn_moba, g_kn_moba):
    sizes = [MLA_Q_RANK, MLA_KV_RANK, MLA_ROPE, MOBA_W, MOBA_W, MOBA_W]
    offs = [0]
    for s in sizes:
        offs.append(offs[-1] + s)
    w_cq, w_ckv, w_kr, w_qb, w_kb, w_vb = [w_in[:, offs[i]:offs[i + 1]] for i in range(6)]
    w_std = jnp.concatenate([w_cq, w_ckv,
                             _slot_cols(w_kr, 1, MLA_ROPE, MLA_NOPE),
                             _slot_cols(_rot_cols(w_kr), 1, MLA_ROPE, MLA_NOPE),
                             _slot_cols(w_kb, MOBA_HEADS, MOBA_HEAD_DIM)], axis=1)
    w_t = jnp.concatenate([w_qb, w_vb], axis=1).T
    wq = w_uq.reshape(MLA_Q_RANK, MLA_HEADS, MLA_QK)
    wq_nope, wq_rope = wq[:, :, :MLA_NOPE], wq[:, :, MLA_NOPE:]
    wq_rot = jnp.concatenate([-wq_rope[:, :, MLA_ROPE // 2:], wq_rope[:, :, :MLA_ROPE // 2]], axis=2)
    w_uq_t = jnp.concatenate([wq_nope, wq_rope, wq_rot], axis=2).reshape(
        MLA_Q_RANK, MLA_HEADS * HEAD_SLOT).T
    wkv = w_ukv.reshape(MLA_KV_RANK, MLA_HEADS, MLA_NOPE + MLA_V)
    w_ukv_std = _slot_cols(wkv[:, :, :MLA_NOPE].reshape(MLA_KV_RANK, MLA_HEADS * MLA_NOPE),
                           MLA_HEADS, MLA_NOPE)
    w_ukv_vt = wkv[:, :, MLA_NOPE:].reshape(MLA_KV_RANK, MLA_HEADS * MLA_V).T
    return {
        "w_std": w_std.astype(BF16), "w_t": w_t.astype(BF16),
        "g_cq": g_cq.reshape(1, -1), "g_ckv": g_ckv.reshape(1, -1),
        "w_uq_t": w_uq_t.astype(BF16), "w_ukv_std": w_ukv_std.astype(BF16),
        "w_ukv_vt": w_ukv_vt.astype(BF16),
        "gq_mla": (g_qn_mla * (MLA_QK ** -0.5 * LOG2E)).reshape(-1, 1),
        "gk_mla": jnp.pad(g_kn_mla, (0, HEAD_SLOT - MLA_QK)).reshape(1, -1),
        "gq_moba": (g_qn_moba * (MOBA_HEAD_DIM ** -0.5 * LOG2E)).reshape(-1, 1),
        "gk_moba": jnp.pad(g_kn_moba, (0, HEAD_SLOT - MOBA_HEAD_DIM)).reshape(1, -1),
    }


def _rope_tables(S):
    half = MLA_ROPE // 2
    inv = ROPE_THETA ** (-jnp.arange(half, dtype=F32) / half)
    ang = jnp.arange(S).astype(F32)[:, None] * inv[None, :]
    cos = jnp.tile(jnp.cos(ang), (1, 2))
    sin = jnp.tile(jnp.sin(ang), (1, 2))
    pad = ((0, 0), (MLA_NOPE, HEAD_SLOT - MLA_QK))
    return {"cos_t": cos.T, "sin_t": sin.T, "cos_r": jnp.pad(cos, pad), "sin_r": jnp.pad(sin, pad)}


def _s5_params(lam_re, lam_im, log_dt, b_re, b_im, c_re, c_im):
    hi = lax.Precision.HIGHEST
    G, P, L = S5_GROUPS, S5_STATE, S5_L
    dt = jnp.exp(log_dt)[:, None]

    def a_pow(s):
        mag = jnp.exp(lam_re * dt * s)
        return mag * jnp.cos(lam_im * dt * s), mag * jnp.sin(lam_im * dt * s)

    a_re, a_im = a_pow(1.0)
    den = lam_re * lam_re + lam_im * lam_im
    k_re = ((a_re - 1.0) * lam_re + a_im * lam_im) / den
    k_im = (a_im * lam_re - (a_re - 1.0) * lam_im) / den
    bb_re = k_re[..., None] * b_re - k_im[..., None] * b_im
    bb_im = k_re[..., None] * b_im + k_im[..., None] * b_re
    pows = [a_pow(float(s)) for s in range(L + 1)]

    lag = []
    for s in range(L):
        pr, pi = pows[s]
        e_re = pr[..., None] * bb_re - pi[..., None] * bb_im
        e_im = pr[..., None] * bb_im + pi[..., None] * bb_re
        lag.append(jnp.einsum("gop,gpi->goi", c_re, e_re, precision=hi)
                   - jnp.einsum("gop,gpi->goi", c_im, e_im, precision=hi))
    I = S5_GROUP
    kcat = jnp.stack(lag, axis=0).reshape(L, N_SLABS, S5_SLAB, I, I)
    kcat = kcat.transpose(1, 0, 4, 2, 3).reshape(N_SLABS, L, I, S5_SLAB * I)

    f_re, f_im = [], []
    for l in range(L):
        pr, pi = pows[L - 1 - l]
        f_re.append(pr[..., None] * bb_re - pi[..., None] * bb_im)
        f_im.append(pr[..., None] * bb_im + pi[..., None] * bb_re)
    f = jnp.stack([jnp.stack(f_re, axis=1), jnp.stack(f_im, axis=1)], axis=0)
    f = f.reshape(2, N_SLABS, S5_SLAB, L, P, I)
    fcat = f.transpose(1, 0, 3, 5, 2, 4).reshape(N_SLABS, 2, L, I, S5_SLAB * P)

    q_re, q_im = [], []
    for l in range(L):
        pr, pi = pows[l + 1]
        q_re.append(c_re * pr[:, None, :] - c_im * pi[:, None, :])
        q_im.append(-(c_re * pi[:, None, :] + c_im * pr[:, None, :]))
    q = jnp.stack([jnp.stack(q_re, axis=1), jnp.stack(q_im, axis=1)], axis=0)
    q = q.reshape(2, N_SLABS, S5_SLAB, L, I, P)
    qcat = q.transpose(1, 0, 3, 5, 2, 4).reshape(N_SLABS, 2, L, P, S5_SLAB * I)

    al_re, al_im = pows[L]
    a_chunk = jnp.concatenate([al_re.reshape(N_SLABS, 1, S5_SLAB * P),
                               al_im.reshape(N_SLABS, 1, S5_SLAB * P)], axis=2)
    return {"kcat": kcat, "fcat": fcat, "qcat": qcat, "a_chunk": a_chunk}


def kernel(x, mix_norm_g, ffn_norm_g, w_in, g_cq, w_uq, g_ckv, w_ukv, g_qn_mla, g_kn_mla,
           g_qn_moba, g_kn_moba, w_o, lam_re, lam_im, log_dt, b_re, b_im, c_re, c_im,
           d_skip, w_glu, w_ff1, w_ff2):
    B, S, D = x.shape
    depth = mix_norm_g.shape[0]
    assert D == D_MODEL and S % ATT_TILE == 0 and S % FFN_TILE == 0 and depth % 2 == 0
    tabs = _rope_tables(S)
    row = lambda v: v.reshape(1, -1)
    for layer in range(0, depth, 2):
        i = layer // 2
        ap = _attn_params(w_in[i], g_cq[i], w_uq[i], g_ckv[i], w_ukv[i], g_qn_mla[i], g_kn_mla[i],
                          g_qn_moba[i], g_kn_moba[i])
        qt, k, vt = _attn_proj(x, row(mix_norm_g[layer]), ap, tabs)
        o = _flash(qt, k, vt)
        x, uc = _attn_out_ffn(x, o, w_o[i].astype(BF16), row(ffn_norm_g[layer]),
                              w_ff1[layer].astype(BF16), w_ff2[layer].astype(BF16),
                              row(mix_norm_g[layer + 1]))
        sp = _s5_params(lam_re[i], lam_im[i], log_dt[i], b_re[i], b_im[i], c_re[i], c_im[i])
        yc = _s5_core(uc, sp)
        x = _s5_out_ffn(x, yc, row(mix_norm_g[layer + 1]), row(d_skip[i]), w_glu[i].astype(BF16),
                        row(ffn_norm_g[layer + 1]), w_ff1[layer + 1].astype(BF16),
                        w_ff2[layer + 1].astype(BF16))
    return x
```

```python
import functools
import math

import jax
import jax.numpy as jnp
from jax import lax
from jax.experimental import pallas as pl
from jax.experimental.pallas import tpu as pltpu

F32 = jnp.float32
BF16 = jnp.bfloat16

D_MODEL = 1024
D_FF = 4 * D_MODEL
EPS = 1e-6
MLA_HEADS = 8
MLA_NOPE = 64
MLA_ROPE = 32
MLA_V = 64
MLA_QK = MLA_NOPE + MLA_ROPE
MLA_Q_RANK = 256
MLA_KV_RANK = 128
ROPE_THETA = 10000.0
MOBA_HEADS = 8
MOBA_HEAD_DIM = 64
MOBA_W = MOBA_HEADS * MOBA_HEAD_DIM
MOBA_BLOCK = 256
MOBA_TOPK = 3
S5_GROUP = 16
S5_GROUPS = D_MODEL // S5_GROUP
S5_STATE = 64

LANE = 128
HEAD_SLOT = 128
N_HEADS = MLA_HEADS + MOBA_HEADS
MASK_VALUE = -1e30
SUM_ROWS = 16
LOG2E = math.log2(math.e)
ATT_TILE = 512
SCORE_ROWS = 256
FFN_TILE = 512
FF_CHUNK = 1024
S5_L = 8
S5_SLAB = LANE // S5_GROUP
N_SLABS = D_MODEL // LANE
S5_ROWS = 1024
VMEM_LIMIT = 56 * 1024 * 1024

_NT = (((1,), (1,)), ((), ()))


def _dot(a, b, precision=None):
    return jnp.dot(a, b, preferred_element_type=F32, precision=precision)


def _dot_nt(a, b):
    return lax.dot_general(a, b, _NT, preferred_element_type=F32)


def _rms_rows(x, gain_row):
    ms = jnp.mean(x * x, axis=-1, keepdims=True)
    return x * lax.rsqrt(ms + EPS) * gain_row


def _const_spec(shape):
    nd = len(shape)
    return pl.BlockSpec(shape, lambda *_: (0,) * nd, pipeline_mode=pl.Buffered(1))


def _attn_proj_kernel(x_ref, gmix_ref, wstd_ref, wt_ref, gcq_ref, gckv_ref, wuqt_ref, wukv_ref,
                      wukvvt_ref, cost_ref, sint_ref, cosr_ref, sinr_ref, gqmla_ref, gkmla_ref,
                      gqmoba_ref, gkmoba_ref, qt_ref, k_ref, vt_ref, km_ref, *, tile):
    ti = pl.program_id(1)
    nsub = tile // MOBA_BLOCK
    sub = MOBA_BLOCK

    @pl.when(ti == 0)
    def _():
        km_ref[...] = jnp.zeros_like(km_ref)

    gq_mla = gqmla_ref[...]
    gk_mla = gkmla_ref[...]
    gq_moba = gqmoba_ref[...]
    gk_moba = gkmoba_ref[...]
    nb = km_ref.shape[1]
    pad_q = jnp.zeros((HEAD_SLOT - MLA_QK, sub), F32)
    pad_m = jnp.zeros((HEAD_SLOT - MOBA_HEAD_DIM - nb, sub), F32)
    jidx = lax.broadcasted_iota(jnp.int32, (nb, sub), 0)
    lower = [jnp.where(jp < jidx, 1.0, 0.0) for jp in range(nb)]
    lane = lax.broadcasted_iota(jnp.int32, (sub, HEAD_SLOT), 1)

    for sb in range(nsub):
        rows = slice(sb * sub, (sb + 1) * sub)
        cur = ti * nsub + sb
        hb = _rms_rows(x_ref[0, rows, :], gmix_ref[...]).astype(BF16)
        pstd = _dot(hb, wstd_ref[...])
        pt = _dot_nt(wt_ref[...], hb)

        o = 0
        cq = pstd[:, o:o + MLA_Q_RANK]; o += MLA_Q_RANK
        ckv = pstd[:, o:o + MLA_KV_RANK]; o += MLA_KV_RANK
        kr_a = pstd[:, o:o + HEAD_SLOT]; o += HEAD_SLOT
        kr_b = pstd[:, o:o + HEAD_SLOT]; o += HEAD_SLOT
        kb_off = o

        cqn = _rms_rows(cq, gcq_ref[...]).astype(BF16)
        ckvn = _rms_rows(ckv, gckv_ref[...]).astype(BF16)
        qup_t = _dot_nt(wuqt_ref[...], cqn)
        kv_std = _dot(ckvn, wukv_ref[...])
        vt_ref[0, 0, 0:MLA_HEADS * MLA_V, rows] = _dot_nt(wukvvt_ref[...], ckvn).astype(BF16)
        vt_ref[0, 0, MLA_HEADS * MLA_V:, rows] = pt[MOBA_W:, :].astype(BF16)
        k_rope = kr_a * cosr_ref[rows, :] + kr_b * sinr_ref[rows, :]
        cos_t = cost_ref[:, rows]
        sin_t = sint_ref[:, rows]
        for h in range(MLA_HEADS):
            blk = qup_t[h * HEAD_SLOT:(h + 1) * HEAD_SLOT]
            nope = blk[0:MLA_NOPE]
            rope = (blk[MLA_NOPE:MLA_QK] * cos_t + blk[MLA_QK:MLA_QK + MLA_ROPE] * sin_t)
            ssq = (jnp.sum(nope * nope, axis=0, keepdims=True)
                   + jnp.sum(rope * rope, axis=0, keepdims=True))
            r = lax.rsqrt(ssq * (1.0 / MLA_QK) + EPS)
            qn = jnp.concatenate([nope * r * gq_mla[0:MLA_NOPE], rope * r * gq_mla[MLA_NOPE:MLA_QK],
                                  pad_q], axis=0)
            qt_ref[0, h, 0, :, rows] = qn.astype(BF16)
            kh = kv_std[:, h * HEAD_SLOT:(h + 1) * HEAD_SLOT] + k_rope
            ssk = jnp.sum(kh * kh, axis=-1, keepdims=True)
            k_ref[0, h, rows, :] = (kh * lax.rsqrt(ssk * (1.0 / MLA_QK) + EPS) * gk_mla).astype(BF16)

        onehot = jnp.where(lane == MOBA_HEAD_DIM + cur, 1.0, 0.0)
        for h in range(MOBA_HEADS):
            kh = pstd[:, kb_off + h * HEAD_SLOT:kb_off + (h + 1) * HEAD_SLOT]
            ssk = jnp.sum(kh * kh, axis=-1, keepdims=True)
            kn = kh * lax.rsqrt(ssk * (1.0 / MOBA_HEAD_DIM) + EPS) * gk_moba
            km_ref[h, pl.ds(cur, 1), :] = jnp.mean(kn, axis=0, keepdims=True)
            k_ref[0, MLA_HEADS + h, rows, :] = (kn + onehot).astype(BF16)

            qh = pt[h * MOBA_HEAD_DIM:(h + 1) * MOBA_HEAD_DIM]
            ssq = jnp.sum(qh * qh, axis=0, keepdims=True)
            qn = qh * lax.rsqrt(ssq * (1.0 / MOBA_HEAD_DIM) + EPS) * gq_moba
            gate = _dot(km_ref[h][:, 0:MOBA_HEAD_DIM], qn, precision=lax.Precision.HIGHEST)
            cnt = jnp.zeros((nb, sub), F32)
            for jp in range(nb):
                gj = gate[jp:jp + 1, :]
                tie = jnp.where(gj == gate, lower[jp], 0.0)
                cnt = cnt + jnp.where(jp < cur, jnp.where(gj > gate, 1.0, tie), 0.0)
            bias = jnp.where(jidx < cur, jnp.where(cnt < float(MOBA_TOPK), 0.0, MASK_VALUE),
                             jnp.where(jidx == cur, 0.0, MASK_VALUE))
            qt_ref[0, MLA_HEADS + h, 0, :, rows] = (
                jnp.concatenate([qn, bias, pad_m], axis=0).astype(BF16))


def _attn_proj(x, gmix, p, tabs):
    B, S, D = x.shape
    T = ATT_TILE
    nt = S // T
    nb = S // MOBA_BLOCK
    consts = [gmix, p["w_std"], p["w_t"], p["g_cq"], p["g_ckv"], p["w_uq_t"], p["w_ukv_std"],
              p["w_ukv_vt"]]
    tail = [p["gq_mla"], p["gk_mla"], p["gq_moba"], p["gk_moba"]]
    in_specs = ([pl.BlockSpec((1, T, D), lambda b, t: (b, t, 0))]
                + [_const_spec(c.shape) for c in consts]
                + [pl.BlockSpec((MLA_ROPE, T), lambda b, t: (0, t)),
                   pl.BlockSpec((MLA_ROPE, T), lambda b, t: (0, t)),
                   pl.BlockSpec((T, HEAD_SLOT), lambda b, t: (t, 0)),
                   pl.BlockSpec((T, HEAD_SLOT), lambda b, t: (t, 0))]
                + [_const_spec(c.shape) for c in tail])
    out_shape = (jax.ShapeDtypeStruct((B, N_HEADS, nt, HEAD_SLOT, T), BF16),
                 jax.ShapeDtypeStruct((B, N_HEADS, S, HEAD_SLOT), BF16),
                 jax.ShapeDtypeStruct((B, nt, N_HEADS * MLA_V, T), BF16))
    out_specs = (pl.BlockSpec((1, N_HEADS, 1, HEAD_SLOT, T), lambda b, t: (b, 0, t, 0, 0)),
                 pl.BlockSpec((1, N_HEADS, T, HEAD_SLOT), lambda b, t: (b, 0, t, 0)),
                 pl.BlockSpec((1, 1, N_HEADS * MLA_V, T), lambda b, t: (b, t, 0, 0)))
    return pl.pallas_call(
        functools.partial(_attn_proj_kernel, tile=T),
        grid=(B, nt),
        in_specs=in_specs,
        out_specs=out_specs,
        out_shape=out_shape,
        scratch_shapes=[pltpu.VMEM((MOBA_HEADS, nb, HEAD_SLOT), F32)],
        compiler_params=pltpu.CompilerParams(
            dimension_semantics=("arbitrary", "arbitrary"), vmem_limit_bytes=VMEM_LIMIT),
        name="attn_proj",
    )(x, *consts, tabs["cos_t"], tabs["sin_t"], tabs["cos_r"], tabs["sin_r"], *tail)


def _flash_kernel(qt_ref, k_ref, vt_ref, o_ref, s_scr, p_scr, acc_scr):
    nt, _, tile = qt_ref.shape[2:]
    nchunk = tile // SCORE_ROWS

    def score_chunk(hh, qidx, j, r, cmax):
        k0 = pl.multiple_of(j * tile, tile) + r * SCORE_ROWS
        s = _dot(k_ref[0, hh, pl.ds(k0, SCORE_ROWS), :], qt_ref[0, hh, qidx])
        s_scr[hh, r * SCORE_ROWS:(r + 1) * SCORE_ROWS, :] = s
        c = jnp.max(s, axis=0, keepdims=True)
        return c if cmax is None else jnp.maximum(cmax, c)

    def step(qi, j, ms, cmaxes, diagonal):
        ones = jnp.ones((SUM_ROWS, tile), BF16)
        new_ms, new_cmaxes = [], []
        for hh in range(2):
            def chunk(r):
                s = s_scr[hh, r * SCORE_ROWS:(r + 1) * SCORE_ROWS, :]
                if diagonal:
                    key = lax.broadcasted_iota(jnp.int32, s.shape, 0) + r * SCORE_ROWS
                    qry = lax.broadcasted_iota(jnp.int32, s.shape, 1)
                    s = jnp.where(key <= qry, s, MASK_VALUE)
                return s
            cmax = cmaxes[hh]
            if diagonal:
                cmax = functools.reduce(jnp.maximum, [jnp.max(chunk(r), axis=0, keepdims=True)
                                                      for r in range(nchunk)])
            m_new = jnp.maximum(ms[hh], cmax)
            nxt = None
            for r in range(nchunk):
                p_scr[hh, r * SCORE_ROWS:(r + 1) * SCORE_ROWS, :] = (
                    jnp.exp2(chunk(r) - m_new).astype(BF16))
                if diagonal:
                    nxt = score_chunk(hh, jnp.minimum(qi + 1, nt - 1), 0, r, nxt)
                else:
                    nxt = score_chunk(hh, qi, j + 1, r, nxt)
            v_ext = jnp.concatenate([vt_ref[0, j, hh * MLA_V:(hh + 1) * MLA_V, :], ones], axis=0)
            acc_scr[hh] = jnp.exp2(ms[hh] - m_new) * acc_scr[hh] + _dot(v_ext, p_scr[hh])
            new_ms.append(m_new)
            new_cmaxes.append(nxt)
        return new_ms, new_cmaxes

    cmax0 = []
    for hh in range(2):
        c = None
        for r in range(nchunk):
            c = score_chunk(hh, 0, 0, r, c)
        cmax0.append(c)
    m0 = jnp.full((1, tile), MASK_VALUE, F32)

    def query_tile(qi, cmaxes):
        acc_scr[...] = jnp.zeros_like(acc_scr)

        def body(t, carry):
            ms, cms = step(qi, t, carry[0:2], carry[2:4], False)
            return ms[0], ms[1], cms[0], cms[1]

        carry = lax.fori_loop(0, qi, body, (m0, m0, cmaxes[0], cmaxes[1]))
        _, cms = step(qi, qi, carry[0:2], carry[2:4], True)
        outs = []
        for hh in range(2):
            acc = acc_scr[hh]
            outs.append(acc[0:MLA_V] * (1.0 / acc[MLA_V:MLA_V + 1]))
        o_ref[0, pl.ds(pl.multiple_of(qi * tile, tile), tile), :] = (
            jnp.concatenate(outs, axis=0).T.astype(BF16))
        return cms[0], cms[1]

    lax.fori_loop(0, nt, query_tile, (cmax0[0], cmax0[1]))


def _flash(qt, k, vt):
    B, H, nt, _, T = qt.shape
    S = nt * T
    return pl.pallas_call(
        _flash_kernel,
        grid=(B, H // 2),
        in_specs=[pl.BlockSpec((1, 2, nt, HEAD_SLOT, T), lambda b, p: (b, p, 0, 0, 0)),
                  pl.BlockSpec((1, 2, S, HEAD_SLOT), lambda b, p: (b, p, 0, 0)),
                  pl.BlockSpec((1, nt, 2 * MLA_V, T), lambda b, p: (b, 0, p, 0))],
        out_specs=pl.BlockSpec((1, S, 2 * MLA_V), lambda b, p: (b, 0, p)),
        out_shape=jax.ShapeDtypeStruct((B, S, H * MLA_V), BF16),
        scratch_shapes=[pltpu.VMEM((2, T, T), F32), pltpu.VMEM((2, T, T), BF16),
                        pltpu.VMEM((2, MLA_V + SUM_ROWS, T), F32)],
        compiler_params=pltpu.CompilerParams(
            dimension_semantics=("arbitrary", "arbitrary"), vmem_limit_bytes=VMEM_LIMIT),
        name="flash",
    )(qt, k, vt)


def _ffn(x1, gffn_ref, w1_ref, w2_ref):
    hb = _rms_rows(x1, gffn_ref[...]).astype(BF16)
    acc = x1
    for c in range(D_FF // FF_CHUNK):
        a = _dot(hb, w1_ref[:, c * FF_CHUNK:(c + 1) * FF_CHUNK])
        a = jnp.square(jnp.maximum(a, 0.0)).astype(BF16)
        acc = acc + _dot(a, w2_ref[c * FF_CHUNK:(c + 1) * FF_CHUNK, :])
    return acc


def _emit_chunk_rows(x2, gnext_ref, u_scr, uc_ref, tile):
    u = _rms_rows(x2, gnext_ref[...])
    rows = tile // S5_L
    for j in range(N_SLABS):
        u_scr[j] = u[:, j * LANE:(j + 1) * LANE]
        for l in range(S5_L):
            uc_ref[j, 0, :, l * LANE:(l + 1) * LANE] = (
                u_scr[j, pl.ds(l, rows, stride=S5_L), :].astype(BF16))


def _attn_out_ffn_kernel(x_ref, o_ref, wo_ref, gffn_ref, w1_ref, w2_ref, gnext_ref,
                         x2_ref, uc_ref, u_scr, *, tile):
    x1 = x_ref[0] + _dot(o_ref[0], wo_ref[...])
    x2 = _ffn(x1, gffn_ref, w1_ref, w2_ref)
    x2_ref[0] = x2
    _emit_chunk_rows(x2, gnext_ref, u_scr, uc_ref, tile)


def _bgroup(B):
    return 8 if B % 8 == 0 else B


def _attn_out_ffn(x, o, wo, gffn, w1, w2, gnext):
    B, S, D = x.shape
    T = FFN_TILE
    row_w = S5_L * LANE
    consts_a = [wo, gffn, w1, w2, gnext]
    return pl.pallas_call(
        functools.partial(_attn_out_ffn_kernel, tile=T),
        grid=(B, S // T),
        in_specs=[pl.BlockSpec((1, T, D), lambda b, t: (b, t, 0)),
                  pl.BlockSpec((1, T, D), lambda b, t: (b, t, 0))]
                 + [_const_spec(c.shape) for c in consts_a],
        out_specs=(pl.BlockSpec((1, T, D), lambda b, t: (b, t, 0)),
                   pl.BlockSpec((N_SLABS, 1, T // S5_L, row_w), lambda b, t: (0, b, t, 0))),
        out_shape=(jax.ShapeDtypeStruct((B, S, D), F32),
                   jax.ShapeDtypeStruct((N_SLABS, B, S // S5_L, row_w), BF16)),
        scratch_shapes=[pltpu.VMEM((N_SLABS, T, LANE), F32)],
        compiler_params=pltpu.CompilerParams(
            dimension_semantics=("arbitrary", "arbitrary"), vmem_limit_bytes=VMEM_LIMIT),
        name="attn_out_ffn",
    )(x, o, *consts_a)


def _s5_expand(kcat_ref, fcat_ref, qcat_ref, wt_scr, min_scr, mout_scr):
    L, I, P = S5_L, S5_GROUP, S5_STATE
    half = S5_SLAB * P

    def same_group(shape, rows_per_group, cols_per_group):
        r = lax.broadcasted_iota(jnp.int32, shape, 0) // rows_per_group
        c = lax.broadcasted_iota(jnp.int32, shape, 1) // cols_per_group
        return r == c

    def tiled(block, mask):
        return jnp.where(mask, jnp.concatenate([block] * S5_SLAB, axis=0), 0.0).astype(BF16)

    m_kk = same_group((LANE, LANE), I, I)
    m_in = same_group((LANE, half), I, P)
    m_out = same_group((half, LANE), P, I)
    zeros = jnp.zeros((LANE, LANE), BF16)
    lag_blocks = [tiled(kcat_ref[0, s], m_kk) for s in range(L)]
    for li in range(L):
        for lo in range(L):
            wt_scr[li * LANE:(li + 1) * LANE, lo * LANE:(lo + 1) * LANE] = (
                lag_blocks[lo - li] if lo >= li else zeros)
    for c in range(2):
        for l in range(L):
            min_scr[l * LANE:(l + 1) * LANE, c * half:(c + 1) * half] = tiled(fcat_ref[0, c, l], m_in)
            mout_scr[c * half:(c + 1) * half, l * LANE:(l + 1) * LANE] = tiled(qcat_ref[0, c, l], m_out)


def _s5_kernel(u_ref, kcat_ref, fcat_ref, qcat_ref, a_ref, y_ref,
               wt_scr, min_scr, mout_scr, v_scr, h_scr, st_scr):
    @pl.when((pl.program_id(1) == 0) & (pl.program_id(2) == 0))
    def _():
        _s5_expand(kcat_ref, fcat_ref, qcat_ref, wt_scr, min_scr, mout_scr)

    @pl.when(pl.program_id(2) == 0)
    def _():
        st_scr[...] = jnp.zeros_like(st_scr)

    bg, rc, row_w = u_ref.shape[1:]
    n_tiles = st_scr.shape[0]
    half = n_tiles // 2
    u = u_ref[0].reshape(bg * rc, row_w)
    v = _dot(u, min_scr[...])
    for t in range(n_tiles):
        for s in range(bg):
            v_scr[t, pl.ds(s, rc, stride=bg), :] = v[s * rc:(s + 1) * rc, t * LANE:(t + 1) * LANE]
    a = a_ref[0]
    a_t = [jnp.broadcast_to(a[:, t * LANE:(t + 1) * LANE], (bg, LANE)) for t in range(n_tiles)]

    def step(c, h):
        r0 = pl.multiple_of(c * bg, bg)
        new = []
        for t in range(n_tiles):
            h_scr[t, pl.ds(r0, bg), :] = h[t]
        for t in range(half):
            new.append(a_t[t] * h[t] - a_t[half + t] * h[half + t] + v_scr[t, pl.ds(r0, bg), :])
        for t in range(half):
            new.append(a_t[t] * h[half + t] + a_t[half + t] * h[t] + v_scr[half + t, pl.ds(r0, bg), :])
        return tuple(new)

    h = lax.fori_loop(0, rc, step, tuple(st_scr[t] for t in range(n_tiles)))
    for t in range(n_tiles):
        st_scr[t] = h[t]
    h_in = jnp.concatenate(
        [jnp.concatenate([h_scr[t, pl.ds(s, rc, stride=bg), :] for t in range(n_tiles)], axis=1)
         for s in range(bg)], axis=0)
    y = _dot(u, wt_scr[...]) + _dot(h_in.astype(BF16), mout_scr[...])
    y_ref[0] = y.reshape(bg, rc, row_w)


def _s5_core(uc, p):
    n_slab, B, n_chunks, row_w = uc.shape
    bg = _bgroup(B)
    rc = min(S5_ROWS // bg, n_chunks)
    n_state = 2 * S5_SLAB * S5_STATE
    n_tiles = n_state // LANE
    tables = [p["kcat"], p["fcat"], p["qcat"], p["a_chunk"]]

    def slab_spec(a):
        nd = a.ndim - 1
        return pl.BlockSpec((1,) + a.shape[1:], lambda j, g, r: (j,) + (0,) * nd)

    return pl.pallas_call(
        _s5_kernel,
        grid=(n_slab, B // bg, n_chunks // rc),
        in_specs=[pl.BlockSpec((1, bg, rc, row_w), lambda j, g, r: (j, g, r, 0))]
                 + [slab_spec(a) for a in tables],
        out_specs=pl.BlockSpec((1, bg, rc, row_w), lambda j, g, r: (j, g, r, 0)),
        out_shape=jax.ShapeDtypeStruct((n_slab, B, n_chunks, row_w), F32),
        scratch_shapes=[pltpu.VMEM((row_w, row_w), BF16), pltpu.VMEM((row_w, n_state), BF16),
                        pltpu.VMEM((n_state, row_w), BF16),
                        pltpu.VMEM((n_tiles, bg * rc, LANE), F32),
                        pltpu.VMEM((n_tiles, bg * rc, LANE), F32),
                        pltpu.VMEM((n_tiles, bg, LANE), F32)],
        compiler_params=pltpu.CompilerParams(
            dimension_semantics=("arbitrary", "arbitrary", "arbitrary"),
            vmem_limit_bytes=VMEM_LIMIT),
        name="s5_core",
    )(uc, *tables)


def _gelu_tanh(y):
    c = math.sqrt(2.0 / math.pi)
    return 0.5 * y * (1.0 + jnp.tanh(c * (y + 0.044715 * (y * y * y))))


def _s5_out_ffn_kernel(x_ref, yc_ref, gmix_ref, dskip_ref, wglu_ref, gffn_ref, w1_ref, w2_ref,
                       x2_ref, y_scr, *, tile):
    x = x_ref[0]
    rows = tile // S5_L
    for j in range(N_SLABS):
        for l in range(S5_L):
            y_scr[j, pl.ds(l, rows, stride=S5_L), :] = yc_ref[j, 0, :, l * LANE:(l + 1) * LANE]
    y = jnp.concatenate([y_scr[j] for j in range(N_SLABS)], axis=1)
    u = _rms_rows(x, gmix_ref[...])
    g = _gelu_tanh(y + dskip_ref[...] * u).astype(BF16)
    vg = _dot(g, wglu_ref[...])
    x1 = x + vg[:, :D_MODEL] * jax.nn.sigmoid(vg[:, D_MODEL:])
    x2_ref[0] = _ffn(x1, gffn_ref, w1_ref, w2_ref)


def _s5_out_ffn(x, yc, gmix, dskip, wglu, gffn, w1, w2):
    B, S, D = x.shape
    T = FFN_TILE
    row_w = S5_L * LANE
    consts = [gmix, dskip, wglu, gffn, w1, w2]
    return pl.pallas_call(
        functools.partial(_s5_out_ffn_kernel, tile=T),
        grid=(B, S // T),
        in_specs=[pl.BlockSpec((1, T, D), lambda b, t: (b, t, 0)),
                  pl.BlockSpec((N_SLABS, 1, T // S5_L, row_w), lambda b, t: (0, b, t, 0))]
                 + [_const_spec(c.shape) for c in consts],
        out_specs=pl.BlockSpec((1, T, D), lambda b, t: (b, t, 0)),
        out_shape=jax.ShapeDtypeStruct((B, S, D), F32),
        scratch_shapes=[pltpu.VMEM((N_SLABS, T, LANE), F32)],
        compiler_params=pltpu.CompilerParams(
            dimension_semantics=("arbitrary", "arbitrary"), vmem_limit_bytes=VMEM_LIMIT),
        name="s5_out_ffn",
    )(x, yc, *consts)


def _rot_cols(w):
    half = w.shape[1] // 2
    return jnp.concatenate([-w[:, half:], w[:, :half]], axis=1)


def _slot_cols(w, n_heads, width, offset=0):
    k = w.shape[0]
    w = w.reshape(k, n_heads, width)
    w = jnp.pad(w, ((0, 0), (0, 0), (offset, HEAD_SLOT - width - offset)))
    return w.reshape(k, n_heads * HEAD_SLOT)


def _attn_params(w_in, g_cq, w_uq, g_ckv, w_ukv, g_qn_mla, g_kn_mla, g_qn_moba, g_kn_moba):
    sizes = [MLA_Q_RANK, MLA_KV_RANK, MLA_ROPE, MOBA_W, MOBA_W, MOBA_W]
    offs = [0]
    for s in sizes:
        offs.append(offs[-1] + s)
    w_cq, w_ckv, w_kr, w_qb, w_kb, w_vb = [w_in[:, offs[i]:offs[i + 1]] for i in range(6)]
    w_std = jnp.concatenate([w_cq, w_ckv,
                             _slot_cols(w_kr, 1, MLA_ROPE, MLA_NOPE),
                             _slot_cols(_rot_cols(w_kr), 1, MLA_ROPE, MLA_NOPE),
                             _slot_cols(w_kb, MOBA_HEADS, MOBA_HEAD_DIM)], axis=1)
    w_t = jnp.concatenate([w_qb, w_vb], axis=1).T
    wq = w_uq.reshape(MLA_Q_RANK, MLA_HEADS, MLA_QK)
    wq_nope, wq_rope = wq[:, :, :MLA_NOPE], wq[:, :, MLA_NOPE:]
    wq_rot = jnp.concatenate([-wq_rope[:, :, MLA_ROPE // 2:], wq_rope[:, :, :MLA_ROPE // 2]], axis=2)
    w_uq_t = jnp.concatenate([wq_nope, wq_rope, wq_rot], axis=2).reshape(
        MLA_Q_RANK, MLA_HEADS * HEAD_SLOT).T
    wkv = w_ukv.reshape(MLA_KV_RANK, MLA_HEADS, MLA_NOPE + MLA_V)
    w_ukv_std = _slot_cols(wkv[:, :, :MLA_NOPE].reshape(MLA_KV_RANK, MLA_HEADS * MLA_NOPE),
                           MLA_HEADS, MLA_NOPE)
    w_ukv_vt = wkv[:, :, MLA_NOPE:].reshape(MLA_KV_RANK, MLA_HEADS * MLA_V).T
    return {
        "w_std": w_std.astype(BF16), "w_t": w_t.astype(BF16),
        "g_cq": g_cq.reshape(1, -1), "g_ckv": g_ckv.reshape(1, -1),
        "w_uq_t": w_uq_t.astype(BF16), "w_ukv_std": w_ukv_std.astype(BF16),
        "w_ukv_vt": w_ukv_vt.astype(BF16),
        "gq_mla": (g_qn_mla * (MLA_QK ** -0.5 * LOG2E)).reshape(-1, 1),
        "gk_mla": jnp.pad(g_kn_mla, (0, HEAD_SLOT - MLA_QK)).reshape(1, -1),
        "gq_moba": (g_qn_moba * (MOBA_HEAD_DIM ** -0.5 * LOG2E)).reshape(-1, 1),
        "gk_moba": jnp.pad(g_kn_moba, (0, HEAD_SLOT - MOBA_HEAD_DIM)).reshape(1, -1),
    }


def _rope_tables(S):
    half = MLA_ROPE // 2
    inv = ROPE_THETA ** (-jnp.arange(half, dtype=F32) / half)
    ang = jnp.arange(S).astype(F32)[:, None] * inv[None, :]
    cos = jnp.tile(jnp.cos(ang), (1, 2))
    sin = jnp.tile(jnp.sin(ang), (1, 2))
    pad = ((0, 0), (MLA_NOPE, HEAD_SLOT - MLA_QK))
    return {"cos_t": cos.T, "sin_t": sin.T, "cos_r": jnp.pad(cos, pad), "sin_r": jnp.pad(sin, pad)}


def _s5_params(lam_re, lam_im, log_dt, b_re, b_im, c_re, c_im):
    hi = lax.Precision.HIGHEST
    G, P, L = S5_GROUPS, S5_STATE, S5_L
    dt = jnp.exp(log_dt)[:, None]

    def a_pow(s):
        mag = jnp.exp(lam_re * dt * s)
        return mag * jnp.cos(lam_im * dt * s), mag * jnp.sin(lam_im * dt * s)

    a_re, a_im = a_pow(1.0)
    den = lam_re * lam_re + lam_im * lam_im
    k_re = ((a_re - 1.0) * lam_re + a_im * lam_im) / den
    k_im = (a_im * lam_re - (a_re - 1.0) * lam_im) / den
    bb_re = k_re[..., None] * b_re - k_im[..., None] * b_im
    bb_im = k_re[..., None] * b_im + k_im[..., None] * b_re
    pows = [a_pow(float(s)) for s in range(L + 1)]

    lag = []
    for s in range(L):
        pr, pi = pows[s]
        e_re = pr[..., None] * bb_re - pi[..., None] * bb_im
        e_im = pr[..., None] * bb_im + pi[..., None] * bb_re
        lag.append(jnp.einsum("gop,gpi->goi", c_re, e_re, precision=hi)
                   - jnp.einsum("gop,gpi->goi", c_im, e_im, precision=hi))
    I = S5_GROUP
    kcat = jnp.stack(lag, axis=0).reshape(L, N_SLABS, S5_SLAB, I, I)
    kcat = kcat.transpose(1, 0, 4, 2, 3).reshape(N_SLABS, L, I, S5_SLAB * I)

    f_re, f_im = [], []
    for l in range(L):
        pr, pi = pows[L - 1 - l]
        f_re.append(pr[..., None] * bb_re - pi[..., None] * bb_im)
        f_im.append(pr[..., None] * bb_im + pi[..., None] * bb_re)
    f = jnp.stack([jnp.stack(f_re, axis=1), jnp.stack(f_im, axis=1)], axis=0)
    f = f.reshape(2, N_SLABS, S5_SLAB, L, P, I)
    fcat = f.transpose(1, 0, 3, 5, 2, 4).reshape(N_SLABS, 2, L, I, S5_SLAB * P)

    q_re, q_im = [], []
    for l in range(L):
        pr, pi = pows[l + 1]
        q_re.append(c_re * pr[:, None, :] - c_im * pi[:, None, :])
        q_im.append(-(c_re * pi[:, None, :] + c_im * pr[:, None, :]))
    q = jnp.stack([jnp.stack(q_re, axis=1), jnp.stack(q_im, axis=1)], axis=0)
    q = q.reshape(2, N_SLABS, S5_SLAB, L, I, P)
    qcat = q.transpose(1, 0, 3, 5, 2, 4).reshape(N_SLABS, 2, L, P, S5_SLAB * I)

    al_re, al_im = pows[L]
    a_chunk = jnp.concatenate([al_re.reshape(N_SLABS, 1, S5_SLAB * P),
                               al_im.reshape(N_SLABS, 1, S5_SLAB * P)], axis=2)
    return {"kcat": kcat, "fcat": fcat, "qcat": qcat, "a_chunk": a_chunk}


def kernel(x, mix_norm_g, ffn_norm_g, w_in, g_cq, w_uq, g_ckv, w_ukv, g_qn_mla, g_kn_mla,
           g_qn_moba, g_kn_moba, w_o, lam_re, lam_im, log_dt, b_re, b_im, c_re, c_im,
           d_skip, w_glu, w_ff1, w_ff2):
    B, S, D = x.shape
    depth = mix_norm_g.shape[0]
    assert D == D_MODEL and S % ATT_TILE == 0 and S % FFN_TILE == 0 and depth % 2 == 0
    tabs = _rope_tables(S)
    row = lambda v: v.reshape(1, -1)
    for layer in range(0, depth, 2):
        i = layer // 2
        ap = _attn_params(w_in[i], g_cq[i], w_uq[i], g_ckv[i], w_ukv[i], g_qn_mla[i], g_kn_mla[i],
                          g_qn_moba[i], g_kn_moba[i])
        qt, k, vt = _attn_proj(x, row(mix_norm_g[layer]), ap, tabs)
        o = _flash(qt, k, vt)
        x, uc = _attn_out_ffn(x, o, w_o[i].astype(BF16), row(ffn_norm_g[layer]),
                              w_ff1[layer].astype(BF16), w_ff2[layer].astype(BF16),
                              row(mix_norm_g[layer + 1]))
        sp = _s5_params(lam_re[i], lam_im[i], log_dt[i], b_re[i], b_im[i], c_re[i], c_im[i])
        yc = _s5_core(uc, sp)
        x = _s5_out_ffn(x, yc, row(mix_norm_g[layer + 1]), row(d_skip[i]), w_glu[i].astype(BF16),
                        row(ffn_norm_g[layer + 1]), w_ff1[layer + 1].astype(BF16),
                        w_ff2[layer + 1].astype(BF16))
    return x
```

```python
import functools
import math

import jax
import jax.numpy as jnp
from jax import lax
from jax.experimental import pallas as pl
from jax.experimental.pallas import tpu as pltpu

F32 = jnp.float32
BF16 = jnp.bfloat16

D_MODEL = 1024
D_FF = 4 * D_MODEL
EPS = 1e-6
MLA_HEADS = 8
MLA_NOPE = 64
MLA_ROPE = 32
MLA_V = 64
MLA_QK = MLA_NOPE + MLA_ROPE
MLA_Q_RANK = 256
MLA_KV_RANK = 128
ROPE_THETA = 10000.0
MOBA_HEADS = 8
MOBA_HEAD_DIM = 64
MOBA_W = MOBA_HEADS * MOBA_HEAD_DIM
MOBA_BLOCK = 256
MOBA_TOPK = 3
S5_GROUP = 16
S5_GROUPS = D_MODEL // S5_GROUP
S5_STATE = 64

LANE = 128
HEAD_SLOT = 128
N_HEADS = MLA_HEADS + MOBA_HEADS
MASK_VALUE = -1e30
SUM_ROWS = 16
LOG2E = math.log2(math.e)
ATT_TILE = 512
SCORE_ROWS = 256
FLASH_HEADS = 4
FFN_TILE = 512
FF_CHUNK = 1024
S5_L = 8
S5_SLAB = LANE // S5_GROUP
N_SLABS = D_MODEL // LANE
S5_ROWS = 1024
VMEM_LIMIT = 56 * 1024 * 1024

_NT = (((1,), (1,)), ((), ()))


def _dot(a, b, precision=None):
    return jnp.dot(a, b, preferred_element_type=F32, precision=precision)


def _dot_nt(a, b):
    return lax.dot_general(a, b, _NT, preferred_element_type=F32)


def _rms_rows(x, gain_row):
    ms = jnp.mean(x * x, axis=-1, keepdims=True)
    return x * lax.rsqrt(ms + EPS) * gain_row


def _const_spec(shape):
    nd = len(shape)
    return pl.BlockSpec(shape, lambda *_: (0,) * nd, pipeline_mode=pl.Buffered(1))


def _attn_proj_kernel(x_ref, gmix_ref, wstd_ref, wt_ref, gcq_ref, gckv_ref, wuqt_ref, wukv_ref,
                      wukvvt_ref, cost_ref, sint_ref, cosr_ref, sinr_ref, gqmla_ref, gkmla_ref,
                      gqmoba_ref, gkmoba_ref, qt_ref, k_ref, vt_ref, km_ref, *, tile):
    ti = pl.program_id(1)
    nsub = tile // MOBA_BLOCK
    sub = MOBA_BLOCK

    @pl.when(ti == 0)
    def _():
        km_ref[...] = jnp.zeros_like(km_ref)

    gq_mla = gqmla_ref[...]
    gk_mla = gkmla_ref[...]
    gq_moba = gqmoba_ref[...]
    gk_moba = gkmoba_ref[...]
    nb = km_ref.shape[1]
    pad_q = jnp.zeros((HEAD_SLOT - MLA_QK, sub), F32)
    pad_m = jnp.zeros((HEAD_SLOT - MOBA_HEAD_DIM - nb, sub), F32)
    jidx = lax.broadcasted_iota(jnp.int32, (nb, sub), 0)
    lower = [jnp.where(jp < jidx, 1.0, 0.0) for jp in range(nb)]
    lane = lax.broadcasted_iota(jnp.int32, (sub, HEAD_SLOT), 1)

    for sb in range(nsub):
        rows = slice(sb * sub, (sb + 1) * sub)
        cur = ti * nsub + sb
        hb = _rms_rows(x_ref[0, rows, :], gmix_ref[...]).astype(BF16)
        pstd = _dot(hb, wstd_ref[...])
        pt = _dot_nt(wt_ref[...], hb)

        o = 0
        cq = pstd[:, o:o + MLA_Q_RANK]; o += MLA_Q_RANK
        ckv = pstd[:, o:o + MLA_KV_RANK]; o += MLA_KV_RANK
        kr_a = pstd[:, o:o + HEAD_SLOT]; o += HEAD_SLOT
        kr_b = pstd[:, o:o + HEAD_SLOT]; o += HEAD_SLOT
        kb_off = o

        cqn = _rms_rows(cq, gcq_ref[...]).astype(BF16)
        ckvn = _rms_rows(ckv, gckv_ref[...]).astype(BF16)
        qup_t = _dot_nt(wuqt_ref[...], cqn)
        kv_std = _dot(ckvn, wukv_ref[...])
        vt_ref[0, 0, 0:MLA_HEADS * MLA_V, rows] = _dot_nt(wukvvt_ref[...], ckvn).astype(BF16)
        vt_ref[0, 0, MLA_HEADS * MLA_V:, rows] = pt[MOBA_W:, :].astype(BF16)
        k_rope = kr_a * cosr_ref[rows, :] + kr_b * sinr_ref[rows, :]
        cos_t = cost_ref[:, rows]
        sin_t = sint_ref[:, rows]
        for h in range(MLA_HEADS):
            blk = qup_t[h * HEAD_SLOT:(h + 1) * HEAD_SLOT]
            nope = blk[0:MLA_NOPE]
            rope = (blk[MLA_NOPE:MLA_QK] * cos_t + blk[MLA_QK:MLA_QK + MLA_ROPE] * sin_t)
            ssq = (jnp.sum(nope * nope, axis=0, keepdims=True)
                   + jnp.sum(rope * rope, axis=0, keepdims=True))
            r = lax.rsqrt(ssq * (1.0 / MLA_QK) + EPS)
            qn = jnp.concatenate([nope * r * gq_mla[0:MLA_NOPE], rope * r * gq_mla[MLA_NOPE:MLA_QK],
                                  pad_q], axis=0)
            qt_ref[0, h, 0, :, rows] = qn.astype(BF16)
            kh = kv_std[:, h * HEAD_SLOT:(h + 1) * HEAD_SLOT] + k_rope
            ssk = jnp.sum(kh * kh, axis=-1, keepdims=True)
            k_ref[0, h, rows, :] = (kh * lax.rsqrt(ssk * (1.0 / MLA_QK) + EPS) * gk_mla).astype(BF16)

        onehot = jnp.where(lane == MOBA_HEAD_DIM + cur, 1.0, 0.0)
        for h in range(MOBA_HEADS):
            kh = pstd[:, kb_off + h * HEAD_SLOT:kb_off + (h + 1) * HEAD_SLOT]
            ssk = jnp.sum(kh * kh, axis=-1, keepdims=True)
            kn = kh * lax.rsqrt(ssk * (1.0 / MOBA_HEAD_DIM) + EPS) * gk_moba
            km_ref[h, pl.ds(cur, 1), :] = jnp.mean(kn, axis=0, keepdims=True)
            k_ref[0, MLA_HEADS + h, rows, :] = (kn + onehot).astype(BF16)

            qh = pt[h * MOBA_HEAD_DIM:(h + 1) * MOBA_HEAD_DIM]
            ssq = jnp.sum(qh * qh, axis=0, keepdims=True)
            qn = qh * lax.rsqrt(ssq * (1.0 / MOBA_HEAD_DIM) + EPS) * gq_moba
            gate = _dot(km_ref[h][:, 0:MOBA_HEAD_DIM], qn, precision=lax.Precision.HIGHEST)
            cnt = jnp.zeros((nb, sub), F32)
            for jp in range(nb):
                gj = gate[jp:jp + 1, :]
                tie = jnp.where(gj == gate, lower[jp], 0.0)
                cnt = cnt + jnp.where(jp < cur, jnp.where(gj > gate, 1.0, tie), 0.0)
            bias = jnp.where(jidx < cur, jnp.where(cnt < float(MOBA_TOPK), 0.0, MASK_VALUE),
                             jnp.where(jidx == cur, 0.0, MASK_VALUE))
            qt_ref[0, MLA_HEADS + h, 0, :, rows] = (
                jnp.concatenate([qn, bias, pad_m], axis=0).astype(BF16))


def _attn_proj(x, gmix, p, tabs):
    B, S, D = x.shape
    T = ATT_TILE
    nt = S // T
    nb = S // MOBA_BLOCK
    consts = [gmix, p["w_std"], p["w_t"], p["g_cq"], p["g_ckv"], p["w_uq_t"], p["w_ukv_std"],
              p["w_ukv_vt"]]
    tail = [p["gq_mla"], p["gk_mla"], p["gq_moba"], p["gk_moba"]]
    in_specs = ([pl.BlockSpec((1, T, D), lambda b, t: (b, t, 0))]
                + [_const_spec(c.shape) for c in consts]
                + [pl.BlockSpec((MLA_ROPE, T), lambda b, t: (0, t)),
                   pl.BlockSpec((MLA_ROPE, T), lambda b, t: (0, t)),
                   pl.BlockSpec((T, HEAD_SLOT), lambda b, t: (t, 0)),
                   pl.BlockSpec((T, HEAD_SLOT), lambda b, t: (t, 0))]
                + [_const_spec(c.shape) for c in tail])
    out_shape = (jax.ShapeDtypeStruct((B, N_HEADS, nt, HEAD_SLOT, T), BF16),
                 jax.ShapeDtypeStruct((B, N_HEADS, S, HEAD_SLOT), BF16),
                 jax.ShapeDtypeStruct((B, nt, N_HEADS * MLA_V, T), BF16))
    out_specs = (pl.BlockSpec((1, N_HEADS, 1, HEAD_SLOT, T), lambda b, t: (b, 0, t, 0, 0)),
                 pl.BlockSpec((1, N_HEADS, T, HEAD_SLOT), lambda b, t: (b, 0, t, 0)),
                 pl.BlockSpec((1, 1, N_HEADS * MLA_V, T), lambda b, t: (b, t, 0, 0)))
    return pl.pallas_call(
        functools.partial(_attn_proj_kernel, tile=T),
        grid=(B, nt),
        in_specs=in_specs,
        out_specs=out_specs,
        out_shape=out_shape,
        scratch_shapes=[pltpu.VMEM((MOBA_HEADS, nb, HEAD_SLOT), F32)],
        compiler_params=pltpu.CompilerParams(
            dimension_semantics=("arbitrary", "arbitrary"), vmem_limit_bytes=VMEM_LIMIT),
        name="attn_proj",
    )(x, *consts, tabs["cos_t"], tabs["sin_t"], tabs["cos_r"], tabs["sin_r"], *tail)


def _flash_kernel(qt_ref, k_ref, vt_ref, o_ref, s_scr, p_scr, acc_scr):
    nh, nt, _, tile = qt_ref.shape[1:]
    nchunk = tile // SCORE_ROWS

    def score_chunk(hh, qidx, j, r, cmax):
        k0 = pl.multiple_of(j * tile, tile) + r * SCORE_ROWS
        s = _dot(k_ref[0, hh, pl.ds(k0, SCORE_ROWS), :], qt_ref[0, hh, qidx])
        s_scr[hh, r * SCORE_ROWS:(r + 1) * SCORE_ROWS, :] = s
        c = jnp.max(s, axis=0, keepdims=True)
        return c if cmax is None else jnp.maximum(cmax, c)

    def step(qi, j, ms, cmaxes, diagonal):
        ones = jnp.ones((SUM_ROWS, tile), BF16)
        new_ms, new_cmaxes = [], []
        for hh in range(nh):
            def chunk(r):
                s = s_scr[hh, r * SCORE_ROWS:(r + 1) * SCORE_ROWS, :]
                if diagonal:
                    key = lax.broadcasted_iota(jnp.int32, s.shape, 0) + r * SCORE_ROWS
                    qry = lax.broadcasted_iota(jnp.int32, s.shape, 1)
                    s = jnp.where(key <= qry, s, MASK_VALUE)
                return s
            cmax = cmaxes[hh]
            if diagonal:
                cmax = functools.reduce(jnp.maximum, [jnp.max(chunk(r), axis=0, keepdims=True)
                                                      for r in range(nchunk)])
            m_new = jnp.maximum(ms[hh], cmax)
            nxt = None
            for r in range(nchunk):
                p_scr[hh, r * SCORE_ROWS:(r + 1) * SCORE_ROWS, :] = (
                    jnp.exp2(chunk(r) - m_new).astype(BF16))
                if diagonal:
                    nxt = score_chunk(hh, jnp.minimum(qi + 1, nt - 1), 0, r, nxt)
                else:
                    nxt = score_chunk(hh, qi, j + 1, r, nxt)
            v_ext = jnp.concatenate([vt_ref[0, j, hh * MLA_V:(hh + 1) * MLA_V, :], ones], axis=0)
            acc_scr[hh] = jnp.exp2(ms[hh] - m_new) * acc_scr[hh] + _dot(v_ext, p_scr[hh])
            new_ms.append(m_new)
            new_cmaxes.append(nxt)
        return new_ms, new_cmaxes

    cmax0 = []
    for hh in range(nh):
        c = None
        for r in range(nchunk):
            c = score_chunk(hh, 0, 0, r, c)
        cmax0.append(c)
    m0 = jnp.full((1, tile), MASK_VALUE, F32)

    def query_tile(qi, cmaxes):
        acc_scr[...] = jnp.zeros_like(acc_scr)

        def body(t, carry):
            ms, cms = step(qi, t, carry[0:nh], carry[nh:], False)
            return tuple(ms) + tuple(cms)

        carry = lax.fori_loop(0, qi, body, (m0,) * nh + tuple(cmaxes))
        _, cms = step(qi, qi, carry[0:nh], carry[nh:], True)
        outs = []
        for hh in range(nh):
            acc = acc_scr[hh]
            outs.append(acc[0:MLA_V] * (1.0 / acc[MLA_V:MLA_V + 1]))
        o_ref[0, pl.ds(pl.multiple_of(qi * tile, tile), tile), :] = (
            jnp.concatenate(outs, axis=0).T.astype(BF16))
        return tuple(cms)

    lax.fori_loop(0, nt, query_tile, tuple(cmax0))


def _flash(qt, k, vt):
    B, H, nt, _, T = qt.shape
    S = nt * T
    nh = FLASH_HEADS
    return pl.pallas_call(
        _flash_kernel,
        grid=(B, H // nh),
        in_specs=[pl.BlockSpec((1, nh, nt, HEAD_SLOT, T), lambda b, p: (b, p, 0, 0, 0)),
                  pl.BlockSpec((1, nh, S, HEAD_SLOT), lambda b, p: (b, p, 0, 0)),
                  pl.BlockSpec((1, nt, nh * MLA_V, T), lambda b, p: (b, 0, p, 0))],
        out_specs=pl.BlockSpec((1, S, nh * MLA_V), lambda b, p: (b, 0, p)),
        out_shape=jax.ShapeDtypeStruct((B, S, H * MLA_V), BF16),
        scratch_shapes=[pltpu.VMEM((nh, T, T), F32), pltpu.VMEM((nh, T, T), BF16),
                        pltpu.VMEM((nh, MLA_V + SUM_ROWS, T), F32)],
        compiler_params=pltpu.CompilerParams(
            dimension_semantics=("arbitrary", "arbitrary"), vmem_limit_bytes=VMEM_LIMIT),
        name="flash",
    )(qt, k, vt)


def _ffn(x1, gffn_ref, w1_ref, w2_ref):
    hb = _rms_rows(x1, gffn_ref[...]).astype(BF16)
    acc = x1
    for c in range(D_FF // FF_CHUNK):
        a = _dot(hb, w1_ref[:, c * FF_CHUNK:(c + 1) * FF_CHUNK])
        a = jnp.square(jnp.maximum(a, 0.0)).astype(BF16)
        acc = acc + _dot(a, w2_ref[c * FF_CHUNK:(c + 1) * FF_CHUNK, :])
    return acc


def _emit_chunk_rows(x2, gnext_ref, u_scr, uc_ref, tile):
    u = _rms_rows(x2, gnext_ref[...])
    rows = tile // S5_L
    for j in range(N_SLABS):
        u_scr[j] = u[:, j * LANE:(j + 1) * LANE]
        for l in range(S5_L):
            uc_ref[j, 0, :, l * LANE:(l + 1) * LANE] = (
                u_scr[j, pl.ds(l, rows, stride=S5_L), :].astype(BF16))


def _attn_out_ffn_kernel(x_ref, o_ref, wo_ref, gffn_ref, w1_ref, w2_ref, gnext_ref,
                         x2_ref, uc_ref, u_scr, *, tile):
    x1 = x_ref[0] + _dot(o_ref[0], wo_ref[...])
    x2 = _ffn(x1, gffn_ref, w1_ref, w2_ref)
    x2_ref[0] = x2
    _emit_chunk_rows(x2, gnext_ref, u_scr, uc_ref, tile)


def _bgroup(B):
    return 8 if B % 8 == 0 else B


def _attn_out_ffn(x, o, wo, gffn, w1, w2, gnext):
    B, S, D = x.shape
    T = FFN_TILE
    row_w = S5_L * LANE
    consts_a = [wo, gffn, w1, w2, gnext]
    return pl.pallas_call(
        functools.partial(_attn_out_ffn_kernel, tile=T),
        grid=(B, S // T),
        in_specs=[pl.BlockSpec((1, T, D), lambda b, t: (b, t, 0)),
                  pl.BlockSpec((1, T, D), lambda b, t: (b, t, 0))]
                 + [_const_spec(c.shape) for c in consts_a],
        out_specs=(pl.BlockSpec((1, T, D), lambda b, t: (b, t, 0)),
                   pl.BlockSpec((N_SLABS, 1, T // S5_L, row_w), lambda b, t: (0, b, t, 0))),
        out_shape=(jax.ShapeDtypeStruct((B, S, D), F32),
                   jax.ShapeDtypeStruct((N_SLABS, B, S // S5_L, row_w), BF16)),
        scratch_shapes=[pltpu.VMEM((N_SLABS, T, LANE), F32)],
        compiler_params=pltpu.CompilerParams(
            dimension_semantics=("arbitrary", "arbitrary"), vmem_limit_bytes=VMEM_LIMIT),
        name="attn_out_ffn",
    )(x, o, *consts_a)


def _s5_expand(kcat_ref, fcat_ref, qcat_ref, wt_scr, min_scr, mout_scr):
    L, I, P = S5_L, S5_GROUP, S5_STATE
    half = S5_SLAB * P

    def same_group(shape, rows_per_group, cols_per_group):
        r = lax.broadcasted_iota(jnp.int32, shape, 0) // rows_per_group
        c = lax.broadcasted_iota(jnp.int32, shape, 1) // cols_per_group
        return r == c

    def tiled(block, mask):
        return jnp.where(mask, jnp.concatenate([block] * S5_SLAB, axis=0), 0.0).astype(BF16)

    m_kk = same_group((LANE, LANE), I, I)
    m_in = same_group((LANE, half), I, P)
    m_out = same_group((half, LANE), P, I)
    zeros = jnp.zeros((LANE, LANE), BF16)
    lag_blocks = [tiled(kcat_ref[0, s], m_kk) for s in range(L)]
    for li in range(L):
        for lo in range(L):
            wt_scr[li * LANE:(li + 1) * LANE, lo * LANE:(lo + 1) * LANE] = (
                lag_blocks[lo - li] if lo >= li else zeros)
    for c in range(2):
        for l in range(L):
            min_scr[l * LANE:(l + 1) * LANE, c * half:(c + 1) * half] = tiled(fcat_ref[0, c, l], m_in)
            mout_scr[c * half:(c + 1) * half, l * LANE:(l + 1) * LANE] = tiled(qcat_ref[0, c, l], m_out)


def _s5_kernel(u_ref, kcat_ref, fcat_ref, qcat_ref, a_ref, y_ref,
               wt_scr, min_scr, mout_scr, v_scr, h_scr, st_scr):
    @pl.when((pl.program_id(1) == 0) & (pl.program_id(2) == 0))
    def _():
        _s5_expand(kcat_ref, fcat_ref, qcat_ref, wt_scr, min_scr, mout_scr)

    @pl.when(pl.program_id(2) == 0)
    def _():
        st_scr[...] = jnp.zeros_like(st_scr)

    bg, rc, row_w = u_ref.shape[1:]
    n_tiles = st_scr.shape[0]
    half = n_tiles // 2
    u = u_ref[0].reshape(bg * rc, row_w)
    v = _dot(u, min_scr[...])
    for t in range(n_tiles):
        for s in range(bg):
            v_scr[t, pl.ds(s, rc, stride=bg), :] = v[s * rc:(s + 1) * rc, t * LANE:(t + 1) * LANE]
    a = a_ref[0]
    a_t = [jnp.broadcast_to(a[:, t * LANE:(t + 1) * LANE], (bg, LANE)) for t in range(n_tiles)]

    def step(c, h):
        r0 = pl.multiple_of(c * bg, bg)
        new = []
        for t in range(n_tiles):
            h_scr[t, pl.ds(r0, bg), :] = h[t]
        for t in range(half):
            new.append(a_t[t] * h[t] - a_t[half + t] * h[half + t] + v_scr[t, pl.ds(r0, bg), :])
        for t in range(half):
            new.append(a_t[t] * h[half + t] + a_t[half + t] * h[t] + v_scr[half + t, pl.ds(r0, bg), :])
        return tuple(new)

    h = lax.fori_loop(0, rc, step, tuple(st_scr[t] for t in range(n_tiles)))
    for t in range(n_tiles):
        st_scr[t] = h[t]
    h_in = jnp.concatenate(
        [jnp.concatenate([h_scr[t, pl.ds(s, rc, stride=bg), :] for t in range(n_tiles)], axis=1)
         for s in range(bg)], axis=0)
    y = _dot(u, wt_scr[...]) + _dot(h_in.astype(BF16), mout_scr[...])
    y_ref[0] = y.reshape(bg, rc, row_w)


def _s5_core(uc, p):
    n_slab, B, n_chunks, row_w = uc.shape
    bg = _bgroup(B)
    rc = min(S5_ROWS // bg, n_chunks)
    n_state = 2 * S5_SLAB * S5_STATE
    n_tiles = n_state // LANE
    tables = [p["kcat"], p["fcat"], p["qcat"], p["a_chunk"]]

    def slab_spec(a):
        nd = a.ndim - 1
        return pl.BlockSpec((1,) + a.shape[1:], lambda j, g, r: (j,) + (0,) * nd)

    return pl.pallas_call(
        _s5_kernel,
        grid=(n_slab, B // bg, n_chunks // rc),
        in_specs=[pl.BlockSpec((1, bg, rc, row_w), lambda j, g, r: (j, g, r, 0))]
                 + [slab_spec(a) for a in tables],
        out_specs=pl.BlockSpec((1, bg, rc, row_w), lambda j, g, r: (j, g, r, 0)),
        out_shape=jax.ShapeDtypeStruct((n_slab, B, n_chunks, row_w), F32),
        scratch_shapes=[pltpu.VMEM((row_w, row_w), BF16), pltpu.VMEM((row_w, n_state), BF16),
                        pltpu.VMEM((n_state, row_w), BF16),
                        pltpu.VMEM((n_tiles, bg * rc, LANE), F32),
                        pltpu.VMEM((n_tiles, bg * rc, LANE), F32),
                        pltpu.VMEM((n_tiles, bg, LANE), F32)],
        compiler_params=pltpu.CompilerParams(
            dimension_semantics=("arbitrary", "arbitrary", "arbitrary"),
            vmem_limit_bytes=VMEM_LIMIT),
        name="s5_core",
    )(uc, *tables)


def _gelu_tanh(y):
    c = math.sqrt(2.0 / math.pi)
    return 0.5 * y * (1.0 + jnp.tanh(c * (y + 0.044715 * (y * y * y))))


def _s5_out_ffn_kernel(x_ref, yc_ref, gmix_ref, dskip_ref, wglu_ref, gffn_ref, w1_ref, w2_ref,
                       x2_ref, y_scr, *, tile):
    x = x_ref[0]
    rows = tile // S5_L
    for j in range(N_SLABS):
        for l in range(S5_L):
            y_scr[j, pl.ds(l, rows, stride=S5_L), :] = yc_ref[j, 0, :, l * LANE:(l + 1) * LANE]
    y = jnp.concatenate([y_scr[j] for j in range(N_SLABS)], axis=1)
    u = _rms_rows(x, gmix_ref[...])
    g = _gelu_tanh(y + dskip_ref[...] * u).astype(BF16)
    vg = _dot(g, wglu_ref[...])
    x1 = x + vg[:, :D_MODEL] * jax.nn.sigmoid(vg[:, D_MODEL:])
    x2_ref[0] = _ffn(x1, gffn_ref, w1_ref, w2_ref)


def _s5_out_ffn(x, yc, gmix, dskip, wglu, gffn, w1, w2):
    B, S, D = x.shape
    T = FFN_TILE
    row_w = S5_L * LANE
    consts = [gmix, dskip, wglu, gffn, w1, w2]
    return pl.pallas_call(
        functools.partial(_s5_out_ffn_kernel, tile=T),
        grid=(B, S // T),
        in_specs=[pl.BlockSpec((1, T, D), lambda b, t: (b, t, 0)),
                  pl.BlockSpec((N_SLABS, 1, T // S5_L, row_w), lambda b, t: (0, b, t, 0))]
                 + [_const_spec(c.shape) for c in consts],
        out_specs=pl.BlockSpec((1, T, D), lambda b, t: (b, t, 0)),
        out_shape=jax.ShapeDtypeStruct((B, S, D), F32),
        scratch_shapes=[pltpu.VMEM((N_SLABS, T, LANE), F32)],
        compiler_params=pltpu.CompilerParams(
            dimension_semantics=("arbitrary", "arbitrary"), vmem_limit_bytes=VMEM_LIMIT),
        name="s5_out_ffn",
    )(x, yc, *consts)


def _rot_cols(w):
    half = w.shape[1] // 2
    return jnp.concatenate([-w[:, half:], w[:, :half]], axis=1)


def _slot_cols(w, n_heads, width, offset=0):
    k = w.shape[0]
    w = w.reshape(k, n_heads, width)
    w = jnp.pad(w, ((0, 0), (0, 0), (offset, HEAD_SLOT - width - offset)))
    return w.reshape(k, n_heads * HEAD_SLOT)


def _attn_params(w_in, g_cq, w_uq, g_ckv, w_ukv, g_qn_mla, g_kn_mla, g_qn_moba, g_kn_moba):
    sizes = [MLA_Q_RANK, MLA_KV_RANK, MLA_ROPE, MOBA_W, MOBA_W, MOBA_W]
    offs = [0]
    for s in sizes:
        offs.append(offs[-1] + s)
    w_cq, w_ckv, w_kr, w_qb, w_kb, w_vb = [w_in[:, offs[i]:offs[i + 1]] for i in range(6)]
    w_std = jnp.concatenate([w_cq, w_ckv,
                             _slot_cols(w_kr, 1, MLA_ROPE, MLA_NOPE),
                             _slot_cols(_rot_cols(w_kr), 1, MLA_ROPE, MLA_NOPE),
                             _slot_cols(w_kb, MOBA_HEADS, MOBA_HEAD_DIM)], axis=1)
    w_t = jnp.concatenate([w_qb, w_vb], axis=1).T
    wq = w_uq.reshape(MLA_Q_RANK, MLA_HEADS, MLA_QK)
    wq_nope, wq_rope = wq[:, :, :MLA_NOPE], wq[:, :, MLA_NOPE:]
    wq_rot = jnp.concatenate([-wq_rope[:, :, MLA_ROPE // 2:], wq_rope[:, :, :MLA_ROPE // 2]], axis=2)
    w_uq_t = jnp.concatenate([wq_nope, wq_rope, wq_rot], axis=2).reshape(
        MLA_Q_RANK, MLA_HEADS * HEAD_SLOT).T
    wkv = w_ukv.reshape(MLA_KV_RANK, MLA_HEADS, MLA_NOPE + MLA_V)
    w_ukv_std = _slot_cols(wkv[:, :, :MLA_NOPE].reshape(MLA_KV_RANK, MLA_HEADS * MLA_NOPE),
                           MLA_HEADS, MLA_NOPE)
    w_ukv_vt = wkv[:, :, MLA_NOPE:].reshape(MLA_KV_RANK, MLA_HEADS * MLA_V).T
    return {
        "w_std": w_std.astype(BF16), "w_t": w_t.astype(BF16),
        "g_cq": g_cq.reshape(1, -1), "g_ckv": g_ckv.reshape(1, -1),
        "w_uq_t": w_uq_t.astype(BF16), "w_ukv_std": w_ukv_std.astype(BF16),
        "w_ukv_vt": w_ukv_vt.astype(BF16),
        "gq_mla": (g_qn_mla * (MLA_QK ** -0.5 * LOG2E)).reshape(-1, 1),
        "gk_mla": jnp.pad(g_kn_mla, (0, HEAD_SLOT - MLA_QK)).reshape(1, -1),
        "gq_moba": (g_qn_moba * (MOBA_HEAD_DIM ** -0.5 * LOG2E)).reshape(-1, 1),
        "gk_moba": jnp.pad(g_kn_moba, (0, HEAD_SLOT - MOBA_HEAD_DIM)).reshape(1, -1),
    }


def _rope_tables(S):
    half = MLA_ROPE // 2
    inv = ROPE_THETA ** (-jnp.arange(half, dtype=F32) / half)
    ang = jnp.arange(S).astype(F32)[:, None] * inv[None, :]
    cos = jnp.tile(jnp.cos(ang), (1, 2))
    sin = jnp.tile(jnp.sin(ang), (1, 2))
    pad = ((0, 0), (MLA_NOPE, HEAD_SLOT - MLA_QK))
    return {"cos_t": cos.T, "sin_t": sin.T, "cos_r": jnp.pad(cos, pad), "sin_r": jnp.pad(sin, pad)}


def _s5_params(lam_re, lam_im, log_dt, b_re, b_im, c_re, c_im):
    hi = lax.Precision.HIGHEST
    G, P, L = S5_GROUPS, S5_STATE, S5_L
    dt = jnp.exp(log_dt)[:, None]

    def a_pow(s):
        mag = jnp.exp(lam_re * dt * s)
        return mag * jnp.cos(lam_im * dt * s), mag * jnp.sin(lam_im * dt * s)

    a_re, a_im = a_pow(1.0)
    den = lam_re * lam_re + lam_im * lam_im
    k_re = ((a_re - 1.0) * lam_re + a_im * lam_im) / den
    k_im = (a_im * lam_re - (a_re - 1.0) * lam_im) / den
    bb_re = k_re[..., None] * b_re - k_im[..., None] * b_im
    bb_im = k_re[..., None] * b_im + k_im[..., None] * b_re
    pows = [a_pow(float(s)) for s in range(L + 1)]

    lag = []
    for s in range(L):
        pr, pi = pows[s]
        e_re = pr[..., None] * bb_re - pi[..., None] * bb_im
        e_im = pr[..., None] * bb_im + pi[..., None] * bb_re
        lag.append(jnp.einsum("gop,gpi->goi", c_re, e_re, precision=hi)
                   - jnp.einsum("gop,gpi->goi", c_im, e_im, precision=hi))
    I = S5_GROUP
    kcat = jnp.stack(lag, axis=0).reshape(L, N_SLABS, S5_SLAB, I, I)
    kcat = kcat.transpose(1, 0, 4, 2, 3).reshape(N_SLABS, L, I, S5_SLAB * I)

    f_re, f_im = [], []
    for l in range(L):
        pr, pi = pows[L - 1 - l]
        f_re.append(pr[..., None] * bb_re - pi[..., None] * bb_im)
        f_im.append(pr[..., None] * bb_im + pi[..., None] * bb_re)
    f = jnp.stack([jnp.stack(f_re, axis=1), jnp.stack(f_im, axis=1)], axis=0)
    f = f.reshape(2, N_SLABS, S5_SLAB, L, P, I)
    fcat = f.transpose(1, 0, 3, 5, 2, 4).reshape(N_SLABS, 2, L, I, S5_SLAB * P)

    q_re, q_im = [], []
    for l in range(L):
        pr, pi = pows[l + 1]
        q_re.append(c_re * pr[:, None, :] - c_im * pi[:, None, :])
        q_im.append(-(c_re * pi[:, None, :] + c_im * pr[:, None, :]))
    q = jnp.stack([jnp.stack(q_re, axis=1), jnp.stack(q_im, axis=1)], axis=0)
    q = q.reshape(2, N_SLABS, S5_SLAB, L, I, P)
    qcat = q.transpose(1, 0, 3, 5, 2, 4).reshape(N_SLABS, 2, L, P, S5_SLAB * I)

    al_re, al_im = pows[L]
    a_chunk = jnp.concatenate([al_re.reshape(N_SLABS, 1, S5_SLAB * P),
                               al_im.reshape(N_SLABS, 1, S5_SLAB * P)], axis=2)
    return {"kcat": kcat, "fcat": fcat, "qcat": qcat, "a_chunk": a_chunk}


def kernel(x, mix_norm_g, ffn_norm_g, w_in, g_cq, w_uq, g_ckv, w_ukv, g_qn_mla, g_kn_mla,
           g_qn_moba, g_kn_moba, w_o, lam_re, lam_im, log_dt, b_re, b_im, c_re, c_im,
           d_skip, w_glu, w_ff1, w_ff2):
    B, S, D = x.shape
    depth = mix_norm_g.shape[0]
    assert D == D_MODEL and S % ATT_TILE == 0 and S % FFN_TILE == 0 and depth % 2 == 0
    tabs = _rope_tables(S)
    row = lambda v: v.reshape(1, -1)
    for layer in range(0, depth, 2):
        i = layer // 2
        ap = _attn_params(w_in[i], g_cq[i], w_uq[i], g_ckv[i], w_ukv[i], g_qn_mla[i], g_kn_mla[i],
                          g_qn_moba[i], g_kn_moba[i])
        qt, k, vt = _attn_proj(x, row(mix_norm_g[layer]), ap, tabs)
        o = _flash(qt, k, vt)
        x, uc = _attn_out_ffn(x, o, w_o[i].astype(BF16), row(ffn_norm_g[layer]),
                              w_ff1[layer].astype(BF16), w_ff2[layer].astype(BF16),
                              row(mix_norm_g[layer + 1]))
        sp = _s5_params(lam_re[i], lam_im[i], log_dt[i], b_re[i], b_im[i], c_re[i], c_im[i])
        yc = _s5_core(uc, sp)
        x = _s5_out_ffn(x, yc, row(mix_norm_g[layer + 1]), row(d_skip[i]), w_glu[i].astype(BF16),
                        row(ffn_norm_g[layer + 1]), w_ff1[layer + 1].astype(BF16),
                        w_ff2[layer + 1].astype(BF16))
    return x
```

```python
import functools
import math

import jax
import jax.numpy as jnp
from jax import lax
from jax.experimental import pallas as pl
from jax.experimental.pallas import tpu as pltpu

F32 = jnp.float32
BF16 = jnp.bfloat16

D_MODEL = 1024
D_FF = 4 * D_MODEL
EPS = 1e-6
MLA_HEADS = 8
MLA_NOPE = 64
MLA_ROPE = 32
MLA_V = 64
MLA_QK = MLA_NOPE + MLA_ROPE
MLA_Q_RANK = 256
MLA_KV_RANK = 128
ROPE_THETA = 10000.0
MOBA_HEADS = 8
MOBA_HEAD_DIM = 64
MOBA_W = MOBA_HEADS * MOBA_HEAD_DIM
MOBA_BLOCK = 256
MOBA_TOPK = 3
S5_GROUP = 16
S5_GROUPS = D_MODEL // S5_GROUP
S5_STATE = 64

LANE = 128
HEAD_SLOT = 128
N_HEADS = MLA_HEADS + MOBA_HEADS
MASK_VALUE = -1e30
SUM_ROWS = 16
LOG2E = math.log2(math.e)
ATT_TILE = 512
SCORE_ROWS = 256
FLASH_HEADS = 8
FFN_TILE = 512
FF_CHUNK = 1024
S5_L = 8
S5_SLAB = LANE // S5_GROUP
N_SLABS = D_MODEL // LANE
S5_ROWS = 1024
VMEM_LIMIT = 56 * 1024 * 1024

_NT = (((1,), (1,)), ((), ()))


def _dot(a, b, precision=None):
    return jnp.dot(a, b, preferred_element_type=F32, precision=precision)


def _dot_nt(a, b):
    return lax.dot_general(a, b, _NT, preferred_element_type=F32)


def _rms_rows(x, gain_row):
    ms = jnp.mean(x * x, axis=-1, keepdims=True)
    return x * lax.rsqrt(ms + EPS) * gain_row


def _const_spec(shape):
    nd = len(shape)
    return pl.BlockSpec(shape, lambda *_: (0,) * nd, pipeline_mode=pl.Buffered(1))


def _attn_proj_kernel(x_ref, gmix_ref, wstd_ref, wt_ref, gcq_ref, gckv_ref, wuqt_ref, wukv_ref,
                      wukvvt_ref, cost_ref, sint_ref, cosr_ref, sinr_ref, gqmla_ref, gkmla_ref,
                      gqmoba_ref, gkmoba_ref, qt_ref, k_ref, vt_ref, km_ref, *, tile):
    ti = pl.program_id(1)
    nsub = tile // MOBA_BLOCK
    sub = MOBA_BLOCK

    @pl.when(ti == 0)
    def _():
        km_ref[...] = jnp.zeros_like(km_ref)

    gq_mla = gqmla_ref[...]
    gk_mla = gkmla_ref[...]
    gq_moba = gqmoba_ref[...]
    gk_moba = gkmoba_ref[...]
    nb = km_ref.shape[1]
    pad_q = jnp.zeros((HEAD_SLOT - MLA_QK, sub), F32)
    pad_m = jnp.zeros((HEAD_SLOT - MOBA_HEAD_DIM - nb, sub), F32)
    jidx = lax.broadcasted_iota(jnp.int32, (nb, sub), 0)
    lower = [jnp.where(jp < jidx, 1.0, 0.0) for jp in range(nb)]
    lane = lax.broadcasted_iota(jnp.int32, (sub, HEAD_SLOT), 1)

    for sb in range(nsub):
        rows = slice(sb * sub, (sb + 1) * sub)
        cur = ti * nsub + sb
        hb = _rms_rows(x_ref[0, rows, :], gmix_ref[...]).astype(BF16)
        pstd = _dot(hb, wstd_ref[...])
        pt = _dot_nt(wt_ref[...], hb)

        o = 0
        cq = pstd[:, o:o + MLA_Q_RANK]; o += MLA_Q_RANK
        ckv = pstd[:, o:o + MLA_KV_RANK]; o += MLA_KV_RANK
        kr_a = pstd[:, o:o + HEAD_SLOT]; o += HEAD_SLOT
        kr_b = pstd[:, o:o + HEAD_SLOT]; o += HEAD_SLOT
        kb_off = o

        cqn = _rms_rows(cq, gcq_ref[...]).astype(BF16)
        ckvn = _rms_rows(ckv, gckv_ref[...]).astype(BF16)
        qup_t = _dot_nt(wuqt_ref[...], cqn)
        kv_std = _dot(ckvn, wukv_ref[...])
        vt_ref[0, 0, 0:MLA_HEADS * MLA_V, rows] = _dot_nt(wukvvt_ref[...], ckvn).astype(BF16)
        vt_ref[0, 0, MLA_HEADS * MLA_V:, rows] = pt[MOBA_W:, :].astype(BF16)
        k_rope = kr_a * cosr_ref[rows, :] + kr_b * sinr_ref[rows, :]
        cos_t = cost_ref[:, rows]
        sin_t = sint_ref[:, rows]
        for h in range(MLA_HEADS):
            blk = qup_t[h * HEAD_SLOT:(h + 1) * HEAD_SLOT]
            nope = blk[0:MLA_NOPE]
            rope = (blk[MLA_NOPE:MLA_QK] * cos_t + blk[MLA_QK:MLA_QK + MLA_ROPE] * sin_t)
            ssq = (jnp.sum(nope * nope, axis=0, keepdims=True)
                   + jnp.sum(rope * rope, axis=0, keepdims=True))
            r = lax.rsqrt(ssq * (1.0 / MLA_QK) + EPS)
            qn = jnp.concatenate([nope * r * gq_mla[0:MLA_NOPE], rope * r * gq_mla[MLA_NOPE:MLA_QK],
                                  pad_q], axis=0)
            qt_ref[0, h, 0, :, rows] = qn.astype(BF16)
            kh = kv_std[:, h * HEAD_SLOT:(h + 1) * HEAD_SLOT] + k_rope
            ssk = jnp.sum(kh * kh, axis=-1, keepdims=True)
            k_ref[0, h, rows, :] = (kh * lax.rsqrt(ssk * (1.0 / MLA_QK) + EPS) * gk_mla).astype(BF16)

        onehot = jnp.where(lane == MOBA_HEAD_DIM + cur, 1.0, 0.0)
        for h in range(MOBA_HEADS):
            kh = pstd[:, kb_off + h * HEAD_SLOT:kb_off + (h + 1) * HEAD_SLOT]
            ssk = jnp.sum(kh * kh, axis=-1, keepdims=True)
            kn = kh * lax.rsqrt(ssk * (1.0 / MOBA_HEAD_DIM) + EPS) * gk_moba
            km_ref[h, pl.ds(cur, 1), :] = jnp.mean(kn, axis=0, keepdims=True)
            k_ref[0, MLA_HEADS + h, rows, :] = (kn + onehot).astype(BF16)

            qh = pt[h * MOBA_HEAD_DIM:(h + 1) * MOBA_HEAD_DIM]
            ssq = jnp.sum(qh * qh, axis=0, keepdims=True)
            qn = qh * lax.rsqrt(ssq * (1.0 / MOBA_HEAD_DIM) + EPS) * gq_moba
            gate = _dot(km_ref[h][:, 0:MOBA_HEAD_DIM], qn, precision=lax.Precision.HIGHEST)
            cnt = jnp.zeros((nb, sub), F32)
            for jp in range(nb):
                gj = gate[jp:jp + 1, :]
                tie = jnp.where(gj == gate, lower[jp], 0.0)
                cnt = cnt + jnp.where(jp < cur, jnp.where(gj > gate, 1.0, tie), 0.0)
            bias = jnp.where(jidx < cur, jnp.where(cnt < float(MOBA_TOPK), 0.0, MASK_VALUE),
                             jnp.where(jidx == cur, 0.0, MASK_VALUE))
            qt_ref[0, MLA_HEADS + h, 0, :, rows] = (
                jnp.concatenate([qn, bias, pad_m], axis=0).astype(BF16))


def _attn_proj(x, gmix, p, tabs):
    B, S, D = x.shape
    T = ATT_TILE
    nt = S // T
    nb = S // MOBA_BLOCK
    consts = [gmix, p["w_std"], p["w_t"], p["g_cq"], p["g_ckv"], p["w_uq_t"], p["w_ukv_std"],
              p["w_ukv_vt"]]
    tail = [p["gq_mla"], p["gk_mla"], p["gq_moba"], p["gk_moba"]]
    in_specs = ([pl.BlockSpec((1, T, D), lambda b, t: (b, t, 0))]
                + [_const_spec(c.shape) for c in consts]
                + [pl.BlockSpec((MLA_ROPE, T), lambda b, t: (0, t)),
                   pl.BlockSpec((MLA_ROPE, T), lambda b, t: (0, t)),
                   pl.BlockSpec((T, HEAD_SLOT), lambda b, t: (t, 0)),
                   pl.BlockSpec((T, HEAD_SLOT), lambda b, t: (t, 0))]
                + [_const_spec(c.shape) for c in tail])
    out_shape = (jax.ShapeDtypeStruct((B, N_HEADS, nt, HEAD_SLOT, T), BF16),
                 jax.ShapeDtypeStruct((B, N_HEADS, S, HEAD_SLOT), BF16),
                 jax.ShapeDtypeStruct((B, nt, N_HEADS * MLA_V, T), BF16))
    out_specs = (pl.BlockSpec((1, N_HEADS, 1, HEAD_SLOT, T), lambda b, t: (b, 0, t, 0, 0)),
                 pl.BlockSpec((1, N_HEADS, T, HEAD_SLOT), lambda b, t: (b, 0, t, 0)),
                 pl.BlockSpec((1, 1, N_HEADS * MLA_V, T), lambda b, t: (b, t, 0, 0)))
    return pl.pallas_call(
        functools.partial(_attn_proj_kernel, tile=T),
        grid=(B, nt),
        in_specs=in_specs,
        out_specs=out_specs,
        out_shape=out_shape,
        scratch_shapes=[pltpu.VMEM((MOBA_HEADS, nb, HEAD_SLOT), F32)],
        compiler_params=pltpu.CompilerParams(
            dimension_semantics=("arbitrary", "arbitrary"), vmem_limit_bytes=VMEM_LIMIT),
        name="attn_proj",
    )(x, *consts, tabs["cos_t"], tabs["sin_t"], tabs["cos_r"], tabs["sin_r"], *tail)


def _flash_kernel(qt_ref, k_ref, vt_ref, o_ref, s_scr, p_scr, acc_scr):
    nh, nt, _, tile = qt_ref.shape[1:]
    nchunk = tile // SCORE_ROWS

    def score_chunk(hh, qidx, j, r, cmax):
        k0 = pl.multiple_of(j * tile, tile) + r * SCORE_ROWS
        s = _dot(k_ref[0, hh, pl.ds(k0, SCORE_ROWS), :], qt_ref[0, hh, qidx])
        s_scr[hh, r * SCORE_ROWS:(r + 1) * SCORE_ROWS, :] = s
        c = jnp.max(s, axis=0, keepdims=True)
        return c if cmax is None else jnp.maximum(cmax, c)

    def step(qi, j, ms, cmaxes, diagonal):
        ones = jnp.ones((SUM_ROWS, tile), BF16)
        new_ms, new_cmaxes = [], []
        for hh in range(nh):
            def chunk(r):
                s = s_scr[hh, r * SCORE_ROWS:(r + 1) * SCORE_ROWS, :]
                if diagonal:
                    key = lax.broadcasted_iota(jnp.int32, s.shape, 0) + r * SCORE_ROWS
                    qry = lax.broadcasted_iota(jnp.int32, s.shape, 1)
                    s = jnp.where(key <= qry, s, MASK_VALUE)
                return s
            cmax = cmaxes[hh]
            if diagonal:
                cmax = functools.reduce(jnp.maximum, [jnp.max(chunk(r), axis=0, keepdims=True)
                                                      for r in range(nchunk)])
            m_new = jnp.maximum(ms[hh], cmax)
            nxt = None
            for r in range(nchunk):
                p_scr[hh, r * SCORE_ROWS:(r + 1) * SCORE_ROWS, :] = (
                    jnp.exp2(chunk(r) - m_new).astype(BF16))
                if diagonal:
                    nxt = score_chunk(hh, jnp.minimum(qi + 1, nt - 1), 0, r, nxt)
                else:
                    nxt = score_chunk(hh, qi, j + 1, r, nxt)
            v_ext = jnp.concatenate([vt_ref[0, j, hh * MLA_V:(hh + 1) * MLA_V, :], ones], axis=0)
            acc_scr[hh] = jnp.exp2(ms[hh] - m_new) * acc_scr[hh] + _dot(v_ext, p_scr[hh])
            new_ms.append(m_new)
            new_cmaxes.append(nxt)
        return new_ms, new_cmaxes

    cmax0 = []
    for hh in range(nh):
        c = None
        for r in range(nchunk):
            c = score_chunk(hh, 0, 0, r, c)
        cmax0.append(c)
    m0 = jnp.full((1, tile), MASK_VALUE, F32)

    def query_tile(qi, cmaxes):
        acc_scr[...] = jnp.zeros_like(acc_scr)

        def body(t, carry):
            ms, cms = step(qi, t, carry[0:nh], carry[nh:], False)
            return tuple(ms) + tuple(cms)

        carry = lax.fori_loop(0, qi, body, (m0,) * nh + tuple(cmaxes))
        _, cms = step(qi, qi, carry[0:nh], carry[nh:], True)
        outs = []
        for hh in range(nh):
            acc = acc_scr[hh]
            outs.append(acc[0:MLA_V] * (1.0 / acc[MLA_V:MLA_V + 1]))
        o_ref[0, pl.ds(pl.multiple_of(qi * tile, tile), tile), :] = (
            jnp.concatenate(outs, axis=0).T.astype(BF16))
        return tuple(cms)

    lax.fori_loop(0, nt, query_tile, tuple(cmax0))


def _flash(qt, k, vt):
    B, H, nt, _, T = qt.shape
    S = nt * T
    nh = FLASH_HEADS
    return pl.pallas_call(
        _flash_kernel,
        grid=(B, H // nh),
        in_specs=[pl.BlockSpec((1, nh, nt, HEAD_SLOT, T), lambda b, p: (b, p, 0, 0, 0),
                               pipeline_mode=pl.Buffered(1)),
                  pl.BlockSpec((1, nh, S, HEAD_SLOT), lambda b, p: (b, p, 0, 0)),
                  pl.BlockSpec((1, nt, nh * MLA_V, T), lambda b, p: (b, 0, p, 0))],
        out_specs=pl.BlockSpec((1, S, nh * MLA_V), lambda b, p: (b, 0, p),
                               pipeline_mode=pl.Buffered(1)),
        out_shape=jax.ShapeDtypeStruct((B, S, H * MLA_V), BF16),
        scratch_shapes=[pltpu.VMEM((nh, T, T), F32), pltpu.VMEM((nh, T, T), BF16),
                        pltpu.VMEM((nh, MLA_V + SUM_ROWS, T), F32)],
        compiler_params=pltpu.CompilerParams(
            dimension_semantics=("arbitrary", "arbitrary"), vmem_limit_bytes=VMEM_LIMIT),
        name="flash",
    )(qt, k, vt)


def _ffn(x1, gffn_ref, w1_ref, w2_ref):
    hb = _rms_rows(x1, gffn_ref[...]).astype(BF16)
    acc = x1
    for c in range(D_FF // FF_CHUNK):
        a = _dot(hb, w1_ref[:, c * FF_CHUNK:(c + 1) * FF_CHUNK])
        a = jnp.square(jnp.maximum(a, 0.0)).astype(BF16)
        acc = acc + _dot(a, w2_ref[c * FF_CHUNK:(c + 1) * FF_CHUNK, :])
    return acc


def _emit_chunk_rows(x2, gnext_ref, u_scr, uc_ref, tile):
    u = _rms_rows(x2, gnext_ref[...])
    rows = tile // S5_L
    for j in range(N_SLABS):
        u_scr[j] = u[:, j * LANE:(j + 1) * LANE]
        for l in range(S5_L):
            uc_ref[j, 0, :, l * LANE:(l + 1) * LANE] = (
                u_scr[j, pl.ds(l, rows, stride=S5_L), :].astype(BF16))


def _attn_out_ffn_kernel(x_ref, o_ref, wo_ref, gffn_ref, w1_ref, w2_ref, gnext_ref,
                         x2_ref, uc_ref, u_scr, *, tile):
    x1 = x_ref[0] + _dot(o_ref[0], wo_ref[...])
    x2 = _ffn(x1, gffn_ref, w1_ref, w2_ref)
    x2_ref[0] = x2
    _emit_chunk_rows(x2, gnext_ref, u_scr, uc_ref, tile)


def _bgroup(B):
    return 8 if B % 8 == 0 else B


def _attn_out_ffn(x, o, wo, gffn, w1, w2, gnext):
    B, S, D = x.shape
    T = FFN_TILE
    row_w = S5_L * LANE
    consts_a = [wo, gffn, w1, w2, gnext]
    return pl.pallas_call(
        functools.partial(_attn_out_ffn_kernel, tile=T),
        grid=(B, S // T),
        in_specs=[pl.BlockSpec((1, T, D), lambda b, t: (b, t, 0)),
                  pl.BlockSpec((1, T, D), lambda b, t: (b, t, 0))]
                 + [_const_spec(c.shape) for c in consts_a],
        out_specs=(pl.BlockSpec((1, T, D), lambda b, t: (b, t, 0)),
                   pl.BlockSpec((N_SLABS, 1, T // S5_L, row_w), lambda b, t: (0, b, t, 0))),
        out_shape=(jax.ShapeDtypeStruct((B, S, D), F32),
                   jax.ShapeDtypeStruct((N_SLABS, B, S // S5_L, row_w), BF16)),
        scratch_shapes=[pltpu.VMEM((N_SLABS, T, LANE), F32)],
        compiler_params=pltpu.CompilerParams(
            dimension_semantics=("arbitrary", "arbitrary"), vmem_limit_bytes=VMEM_LIMIT),
        name="attn_out_ffn",
    )(x, o, *consts_a)


def _s5_expand(kcat_ref, fcat_ref, qcat_ref, wt_scr, min_scr, mout_scr):
    L, I, P = S5_L, S5_GROUP, S5_STATE
    half = S5_SLAB * P

    def same_group(shape, rows_per_group, cols_per_group):
        r = lax.broadcasted_iota(jnp.int32, shape, 0) // rows_per_group
        c = lax.broadcasted_iota(jnp.int32, shape, 1) // cols_per_group
        return r == c

    def tiled(block, mask):
        return jnp.where(mask, jnp.concatenate([block] * S5_SLAB, axis=0), 0.0).astype(BF16)

    m_kk = same_group((LANE, LANE), I, I)
    m_in = same_group((LANE, half), I, P)
    m_out = same_group((half, LANE), P, I)
    zeros = jnp.zeros((LANE, LANE), BF16)
    lag_blocks = [tiled(kcat_ref[0, s], m_kk) for s in range(L)]
    for li in range(L):
        for lo in range(L):
            wt_scr[li * LANE:(li + 1) * LANE, lo * LANE:(lo + 1) * LANE] = (
                lag_blocks[lo - li] if lo >= li else zeros)
    for c in range(2):
        for l in range(L):
            min_scr[l * LANE:(l + 1) * LANE, c * half:(c + 1) * half] = tiled(fcat_ref[0, c, l], m_in)
            mout_scr[c * half:(c + 1) * half, l * LANE:(l + 1) * LANE] = tiled(qcat_ref[0, c, l], m_out)


def _s5_kernel(u_ref, kcat_ref, fcat_ref, qcat_ref, a_ref, y_ref,
               wt_scr, min_scr, mout_scr, v_scr, h_scr, st_scr):
    @pl.when((pl.program_id(1) == 0) & (pl.program_id(2) == 0))
    def _():
        _s5_expand(kcat_ref, fcat_ref, qcat_ref, wt_scr, min_scr, mout_scr)

    @pl.when(pl.program_id(2) == 0)
    def _():
        st_scr[...] = jnp.zeros_like(st_scr)

    bg, rc, row_w = u_ref.shape[1:]
    n_tiles = st_scr.shape[0]
    half = n_tiles // 2
    u = u_ref[0].reshape(bg * rc, row_w)
    v = _dot(u, min_scr[...])
    for t in range(n_tiles):
        for s in range(bg):
            v_scr[t, pl.ds(s, rc, stride=bg), :] = v[s * rc:(s + 1) * rc, t * LANE:(t + 1) * LANE]
    a = a_ref[0]
    a_t = [jnp.broadcast_to(a[:, t * LANE:(t + 1) * LANE], (bg, LANE)) for t in range(n_tiles)]

    def step(c, h):
        r0 = pl.multiple_of(c * bg, bg)
        new = []
        for t in range(n_tiles):
            h_scr[t, pl.ds(r0, bg), :] = h[t]
        for t in range(half):
            new.append(a_t[t] * h[t] - a_t[half + t] * h[half + t] + v_scr[t, pl.ds(r0, bg), :])
        for t in range(half):
            new.append(a_t[t] * h[half + t] + a_t[half + t] * h[t] + v_scr[half + t, pl.ds(r0, bg), :])
        return tuple(new)

    h = lax.fori_loop(0, rc, step, tuple(st_scr[t] for t in range(n_tiles)))
    for t in range(n_tiles):
        st_scr[t] = h[t]
    h_in = jnp.concatenate(
        [jnp.concatenate([h_scr[t, pl.ds(s, rc, stride=bg), :] for t in range(n_tiles)], axis=1)
         for s in range(bg)], axis=0)
    y = _dot(u, wt_scr[...]) + _dot(h_in.astype(BF16), mout_scr[...])
    y_ref[0] = y.reshape(bg, rc, row_w)


def _s5_core(uc, p):
    n_slab, B, n_chunks, row_w = uc.shape
    bg = _bgroup(B)
    rc = min(S5_ROWS // bg, n_chunks)
    n_state = 2 * S5_SLAB * S5_STATE
    n_tiles = n_state // LANE
    tables = [p["kcat"], p["fcat"], p["qcat"], p["a_chunk"]]

    def slab_spec(a):
        nd = a.ndim - 1
        return pl.BlockSpec((1,) + a.shape[1:], lambda j, g, r: (j,) + (0,) * nd)

    return pl.pallas_call(
        _s5_kernel,
        grid=(n_slab, B // bg, n_chunks // rc),
        in_specs=[pl.BlockSpec((1, bg, rc, row_w), lambda j, g, r: (j, g, r, 0))]
                 + [slab_spec(a) for a in tables],
        out_specs=pl.BlockSpec((1, bg, rc, row_w), lambda j, g, r: (j, g, r, 0)),
        out_shape=jax.ShapeDtypeStruct((n_slab, B, n_chunks, row_w), F32),
        scratch_shapes=[pltpu.VMEM((row_w, row_w), BF16), pltpu.VMEM((row_w, n_state), BF16),
                        pltpu.VMEM((n_state, row_w), BF16),
                        pltpu.VMEM((n_tiles, bg * rc, LANE), F32),
                        pltpu.VMEM((n_tiles, bg * rc, LANE), F32),
                        pltpu.VMEM((n_tiles, bg, LANE), F32)],
        compiler_params=pltpu.CompilerParams(
            dimension_semantics=("arbitrary", "arbitrary", "arbitrary"),
            vmem_limit_bytes=VMEM_LIMIT),
        name="s5_core",
    )(uc, *tables)


def _gelu_tanh(y):
    c = math.sqrt(2.0 / math.pi)
    return 0.5 * y * (1.0 + jnp.tanh(c * (y + 0.044715 * (y * y * y))))


def _s5_out_ffn_kernel(x_ref, yc_ref, gmix_ref, dskip_ref, wglu_ref, gffn_ref, w1_ref, w2_ref,
                       x2_ref, y_scr, *, tile):
    x = x_ref[0]
    rows = tile // S5_L
    for j in range(N_SLABS):
        for l in range(S5_L):
            y_scr[j, pl.ds(l, rows, stride=S5_L), :] = yc_ref[j, 0, :, l * LANE:(l + 1) * LANE]
    y = jnp.concatenate([y_scr[j] for j in range(N_SLABS)], axis=1)
    u = _rms_rows(x, gmix_ref[...])
    g = _gelu_tanh(y + dskip_ref[...] * u).astype(BF16)
    vg = _dot(g, wglu_ref[...])
    x1 = x + vg[:, :D_MODEL] * jax.nn.sigmoid(vg[:, D_MODEL:])
    x2_ref[0] = _ffn(x1, gffn_ref, w1_ref, w2_ref)


def _s5_out_ffn(x, yc, gmix, dskip, wglu, gffn, w1, w2):
    B, S, D = x.shape
    T = FFN_TILE
    row_w = S5_L * LANE
    consts = [gmix, dskip, wglu, gffn, w1, w2]
    return pl.pallas_call(
        functools.partial(_s5_out_ffn_kernel, tile=T),
        grid=(B, S // T),
        in_specs=[pl.BlockSpec((1, T, D), lambda b, t: (b, t, 0)),
                  pl.BlockSpec((N_SLABS, 1, T // S5_L, row_w), lambda b, t: (0, b, t, 0))]
                 + [_const_spec(c.shape) for c in consts],
        out_specs=pl.BlockSpec((1, T, D), lambda b, t: (b, t, 0)),
        out_shape=jax.ShapeDtypeStruct((B, S, D), F32),
        scratch_shapes=[pltpu.VMEM((N_SLABS, T, LANE), F32)],
        compiler_params=pltpu.CompilerParams(
            dimension_semantics=("arbitrary", "arbitrary"), vmem_limit_bytes=VMEM_LIMIT),
        name="s5_out_ffn",
    )(x, yc, *consts)


def _rot_cols(w):
    half = w.shape[1] // 2
    return jnp.concatenate([-w[:, half:], w[:, :half]], axis=1)


def _slot_cols(w, n_heads, width, offset=0):
    k = w.shape[0]
    w = w.reshape(k, n_heads, width)
    w = jnp.pad(w, ((0, 0), (0, 0), (offset, HEAD_SLOT - width - offset)))
    return w.reshape(k, n_heads * HEAD_SLOT)


def _attn_params(w_in, g_cq, w_uq, g_ckv, w_ukv, g_qn_mla, g_kn_mla, g_qn_moba, g_kn_moba):
    sizes = [MLA_Q_RANK, MLA_KV_RANK, MLA_ROPE, MOBA_W, MOBA_W, MOBA_W]
    offs = [0]
    for s in sizes:
        offs.append(offs[-1] + s)
    w_cq, w_ckv, w_kr, w_qb, w_kb, w_vb = [w_in[:, offs[i]:offs[i + 1]] for i in range(6)]
    w_std = jnp.concatenate([w_cq, w_ckv,
                             _slot_cols(w_kr, 1, MLA_ROPE, MLA_NOPE),
                             _slot_cols(_rot_cols(w_kr), 1, MLA_ROPE, MLA_NOPE),
                             _slot_cols(w_kb, MOBA_HEADS, MOBA_HEAD_DIM)], axis=1)
    w_t = jnp.concatenate([w_qb, w_vb], axis=1).T
    wq = w_uq.reshape(MLA_Q_RANK, MLA_HEADS, MLA_QK)
    wq_nope, wq_rope = wq[:, :, :MLA_NOPE], wq[:, :, MLA_NOPE:]
    wq_rot = jnp.concatenate([-wq_rope[:, :, MLA_ROPE // 2:], wq_rope[:, :, :MLA_ROPE // 2]], axis=2)
    w_uq_t = jnp.concatenate([wq_nope, wq_rope, wq_rot], axis=2).reshape(
        MLA_Q_RANK, MLA_HEADS * HEAD_SLOT).T
    wkv = w_ukv.reshape(MLA_KV_RANK, MLA_HEADS, MLA_NOPE + MLA_V)
    w_ukv_std = _slot_cols(wkv[:, :, :MLA_NOPE].reshape(MLA_KV_RANK, MLA_HEADS * MLA_NOPE),
                           MLA_HEADS, MLA_NOPE)
    w_ukv_vt = wkv[:, :, MLA_NOPE:].reshape(MLA_KV_RANK, MLA_HEADS * MLA_V).T
    return {
        "w_std": w_std.astype(BF16), "w_t": w_t.astype(BF16),
        "g_cq": g_cq.reshape(1, -1), "g_ckv": g_ckv.reshape(1, -1),
        "w_uq_t": w_uq_t.astype(BF16), "w_ukv_std": w_ukv_std.astype(BF16),
        "w_ukv_vt": w_ukv_vt.astype(BF16),
        "gq_mla": (g_qn_mla * (MLA_QK ** -0.5 * LOG2E)).reshape(-1, 1),
        "gk_mla": jnp.pad(g_kn_mla, (0, HEAD_SLOT - MLA_QK)).reshape(1, -1),
        "gq_moba": (g_qn_moba * (MOBA_HEAD_DIM ** -0.5 * LOG2E)).reshape(-1, 1),
        "gk_moba": jnp.pad(g_kn_moba, (0, HEAD_SLOT - MOBA_HEAD_DIM)).reshape(1, -1),
    }


def _rope_tables(S):
    half = MLA_ROPE // 2
    inv = ROPE_THETA ** (-jnp.arange(half, dtype=F32) / half)
    ang = jnp.arange(S).astype(F32)[:, None] * inv[None, :]
    cos = jnp.tile(jnp.cos(ang), (1, 2))
    sin = jnp.tile(jnp.sin(ang), (1, 2))
    pad = ((0, 0), (MLA_NOPE, HEAD_SLOT - MLA_QK))
    return {"cos_t": cos.T, "sin_t": sin.T, "cos_r": jnp.pad(cos, pad), "sin_r": jnp.pad(sin, pad)}


def _s5_params(lam_re, lam_im, log_dt, b_re, b_im, c_re, c_im):
    hi = lax.Precision.HIGHEST
    G, P, L = S5_GROUPS, S5_STATE, S5_L
    dt = jnp.exp(log_dt)[:, None]

    def a_pow(s):
        mag = jnp.exp(lam_re * dt * s)
        return mag * jnp.cos(lam_im * dt * s), mag * jnp.sin(lam_im * dt * s)

    a_re, a_im = a_pow(1.0)
    den = lam_re * lam_re + lam_im * lam_im
    k_re = ((a_re - 1.0) * lam_re + a_im * lam_im) / den
    k_im = (a_im * lam_re - (a_re - 1.0) * lam_im) / den
    bb_re = k_re[..., None] * b_re - k_im[..., None] * b_im
    bb_im = k_re[..., None] * b_im + k_im[..., None] * b_re
    pows = [a_pow(float(s)) for s in range(L + 1)]

    lag = []
    for s in range(L):
        pr, pi = pows[s]
        e_re = pr[..., None] * bb_re - pi[..., None] * bb_im
        e_im = pr[..., None] * bb_im + pi[..., None] * bb_re
        lag.append(jnp.einsum("gop,gpi->goi", c_re, e_re, precision=hi)
                   - jnp.einsum("gop,gpi->goi", c_im, e_im, precision=hi))
    I = S5_GROUP
    kcat = jnp.stack(lag, axis=0).reshape(L, N_SLABS, S5_SLAB, I, I)
    kcat = kcat.transpose(1, 0, 4, 2, 3).reshape(N_SLABS, L, I, S5_SLAB * I)

    f_re, f_im = [], []
    for l in range(L):
        pr, pi = pows[L - 1 - l]
        f_re.append(pr[..., None] * bb_re - pi[..., None] * bb_im)
        f_im.append(pr[..., None] * bb_im + pi[..., None] * bb_re)
    f = jnp.stack([jnp.stack(f_re, axis=1), jnp.stack(f_im, axis=1)], axis=0)
    f = f.reshape(2, N_SLABS, S5_SLAB, L, P, I)
    fcat = f.transpose(1, 0, 3, 5, 2, 4).reshape(N_SLABS, 2, L, I, S5_SLAB * P)

    q_re, q_im = [], []
    for l in range(L):
        pr, pi = pows[l + 1]
        q_re.append(c_re * pr[:, None, :] - c_im * pi[:, None, :])
        q_im.append(-(c_re * pi[:, None, :] + c_im * pr[:, None, :]))
    q = jnp.stack([jnp.stack(q_re, axis=1), jnp.stack(q_im, axis=1)], axis=0)
    q = q.reshape(2, N_SLABS, S5_SLAB, L, I, P)
    qcat = q.transpose(1, 0, 3, 5, 2, 4).reshape(N_SLABS, 2, L, P, S5_SLAB * I)

    al_re, al_im = pows[L]
    a_chunk = jnp.concatenate([al_re.reshape(N_SLABS, 1, S5_SLAB * P),
                               al_im.reshape(N_SLABS, 1, S5_SLAB * P)], axis=2)
    return {"kcat": kcat, "fcat": fcat, "qcat": qcat, "a_chunk": a_chunk}


def kernel(x, mix_norm_g, ffn_norm_g, w_in, g_cq, w_uq, g_ckv, w_ukv, g_qn_mla, g_kn_mla,
           g_qn_moba, g_kn_moba, w_o, lam_re, lam_im, log_dt, b_re, b_im, c_re, c_im,
           d_skip, w_glu, w_ff1, w_ff2):
    B, S, D = x.shape
    depth = mix_norm_g.shape[0]
    assert D == D_MODEL and S % ATT_TILE == 0 and S % FFN_TILE == 0 and depth % 2 == 0
    tabs = _rope_tables(S)
    row = lambda v: v.reshape(1, -1)
    for layer in range(0, depth, 2):
        i = layer // 2
        ap = _attn_params(w_in[i], g_cq[i], w_uq[i], g_ckv[i], w_ukv[i], g_qn_mla[i], g_kn_mla[i],
                          g_qn_moba[i], g_kn_moba[i])
        qt, k, vt = _attn_proj(x, row(mix_norm_g[layer]), ap, tabs)
        o = _flash(qt, k, vt)
        x, uc = _attn_out_ffn(x, o, w_o[i].astype(BF16), row(ffn_norm_g[layer]),
                              w_ff1[layer].astype(BF16), w_ff2[layer].astype(BF16),
                              row(mix_norm_g[layer + 1]))
        sp = _s5_params(lam_re[i], lam_im[i], log_dt[i], b_re[i], b_im[i], c_re[i], c_im[i])
        yc = _s5_core(uc, sp)
        x = _s5_out_ffn(x, yc, row(mix_norm_g[layer + 1]), row(d_skip[i]), w_glu[i].astype(BF16),
                        row(ffn_norm_g[layer + 1]), w_ff1[layer + 1].astype(BF16),
                        w_ff2[layer + 1].astype(BF16))
    return x
```

```python
import functools
import math
from typing import NamedTuple

import jax
import jax.numpy as jnp
from jax import lax
from jax.experimental import pallas as pl
from jax.experimental.pallas import tpu as pltpu

F32 = jnp.float32
BF16 = jnp.bfloat16

D_MODEL = 1024
D_FF = 4 * D_MODEL
EPS = 1e-6
MLA_HEADS = 8
MLA_NOPE = 64
MLA_ROPE = 32
MLA_V = 64
MLA_QK = MLA_NOPE + MLA_ROPE
MLA_Q_RANK = 256
MLA_KV_RANK = 128
ROPE_THETA = 10000.0
MOBA_HEADS = 8
MOBA_HEAD_DIM = 64
MOBA_W = MOBA_HEADS * MOBA_HEAD_DIM
MOBA_BLOCK = 256
MOBA_TOPK = 3
S5_GROUP = 16
S5_GROUPS = D_MODEL // S5_GROUP
S5_STATE = 64

LANE = 128
HEAD_SLOT = 128
N_HEADS = MLA_HEADS + MOBA_HEADS
MASK_VALUE = -1e30
SUM_ROWS = 16
LOG2E = math.log2(math.e)
ATT_TILE = 512
SCORE_ROWS = 256
FLASH_HEADS = 4
FFN_TILE = 512
FF_CHUNK = 1024
S5_L = 8
S5_SLAB = LANE // S5_GROUP
N_SLABS = D_MODEL // LANE
S5_ROWS = 1024
VMEM_LIMIT = 56 * 1024 * 1024
CAST_BLOCK_BYTES = 4 * 1024 * 1024

_NT = (((1,), (1,)), ((), ()))


def _dot(a, b, precision=None):
    return jnp.dot(a, b, preferred_element_type=F32, precision=precision)


def _dot_nt(a, b):
    return lax.dot_general(a, b, _NT, preferred_element_type=F32)


def _rms_rows(x, gain_row):
    ms = jnp.mean(x * x, axis=-1, keepdims=True)
    return x * lax.rsqrt(ms + EPS) * gain_row


def _const_spec(shape):
    nd = len(shape)
    return pl.BlockSpec(shape, lambda *_: (0,) * nd, pipeline_mode=pl.Buffered(1))


class _Layer(NamedTuple):
    stack: jax.Array
    index: int


def _weight_spec(w):
    if isinstance(w, _Layer):
        nd = w.stack.ndim - 1
        return pl.BlockSpec((None,) + w.stack.shape[1:], lambda *_: (w.index,) + (0,) * nd,
                            pipeline_mode=pl.Buffered(1))
    return _const_spec(w.shape)


def _operand(w):
    return w.stack if isinstance(w, _Layer) else w


def _attn_proj_kernel(x_ref, gmix_ref, wstd_ref, wt_ref, gcq_ref, gckv_ref, wuqt_ref, wukv_ref,
                      wukvvt_ref, cost_ref, sint_ref, cosr_ref, sinr_ref, gqmla_ref, gkmla_ref,
                      gqmoba_ref, gkmoba_ref, qt_ref, k_ref, vt_ref, km_ref, *, tile):
    ti = pl.program_id(1)
    nsub = tile // MOBA_BLOCK
    sub = MOBA_BLOCK

    @pl.when(ti == 0)
    def _():
        km_ref[...] = jnp.zeros_like(km_ref)

    gq_mla = gqmla_ref[...]
    gk_mla = gkmla_ref[...]
    gq_moba = gqmoba_ref[...]
    gk_moba = gkmoba_ref[...]
    nb = km_ref.shape[1]
    pad_q = jnp.zeros((HEAD_SLOT - MLA_QK, sub), F32)
    pad_m = jnp.zeros((HEAD_SLOT - MOBA_HEAD_DIM - nb, sub), F32)
    jidx = lax.broadcasted_iota(jnp.int32, (nb, sub), 0)
    lower = [jnp.where(jp < jidx, 1.0, 0.0) for jp in range(nb)]
    lane = lax.broadcasted_iota(jnp.int32, (sub, HEAD_SLOT), 1)

    for sb in range(nsub):
        rows = slice(sb * sub, (sb + 1) * sub)
        cur = ti * nsub + sb
        hb = _rms_rows(x_ref[0, rows, :], gmix_ref[...]).astype(BF16)
        pstd = _dot(hb, wstd_ref[...])
        pt = _dot_nt(wt_ref[...], hb)

        o = 0
        cq = pstd[:, o:o + MLA_Q_RANK]; o += MLA_Q_RANK
        ckv = pstd[:, o:o + MLA_KV_RANK]; o += MLA_KV_RANK
        kr_a = pstd[:, o:o + HEAD_SLOT]; o += HEAD_SLOT
        kr_b = pstd[:, o:o + HEAD_SLOT]; o += HEAD_SLOT
        kb_off = o

        cqn = _rms_rows(cq, gcq_ref[...]).astype(BF16)
        ckvn = _rms_rows(ckv, gckv_ref[...]).astype(BF16)
        qup_t = _dot_nt(wuqt_ref[...], cqn)
        kv_std = _dot(ckvn, wukv_ref[...])
        vt_ref[0, 0, 0:MLA_HEADS * MLA_V, rows] = _dot_nt(wukvvt_ref[...], ckvn).astype(BF16)
        vt_ref[0, 0, MLA_HEADS * MLA_V:, rows] = pt[MOBA_W:, :].astype(BF16)
        k_rope = kr_a * cosr_ref[rows, :] + kr_b * sinr_ref[rows, :]
        cos_t = cost_ref[:, rows]
        sin_t = sint_ref[:, rows]
        for h in range(MLA_HEADS):
            blk = qup_t[h * HEAD_SLOT:(h + 1) * HEAD_SLOT]
            nope = blk[0:MLA_NOPE]
            rope = (blk[MLA_NOPE:MLA_QK] * cos_t + blk[MLA_QK:MLA_QK + MLA_ROPE] * sin_t)
            ssq = (jnp.sum(nope * nope, axis=0, keepdims=True)
                   + jnp.sum(rope * rope, axis=0, keepdims=True))
            r = lax.rsqrt(ssq * (1.0 / MLA_QK) + EPS)
            qn = jnp.concatenate([nope * r * gq_mla[0:MLA_NOPE], rope * r * gq_mla[MLA_NOPE:MLA_QK],
                                  pad_q], axis=0)
            qt_ref[0, h, 0, :, rows] = qn.astype(BF16)
            kh = kv_std[:, h * HEAD_SLOT:(h + 1) * HEAD_SLOT] + k_rope
            ssk = jnp.sum(kh * kh, axis=-1, keepdims=True)
            k_ref[0, h, rows, :] = (kh * lax.rsqrt(ssk * (1.0 / MLA_QK) + EPS) * gk_mla).astype(BF16)

        onehot = jnp.where(lane == MOBA_HEAD_DIM + cur, 1.0, 0.0)
        for h in range(MOBA_HEADS):
            kh = pstd[:, kb_off + h * HEAD_SLOT:kb_off + (h + 1) * HEAD_SLOT]
            ssk = jnp.sum(kh * kh, axis=-1, keepdims=True)
            kn = kh * lax.rsqrt(ssk * (1.0 / MOBA_HEAD_DIM) + EPS) * gk_moba
            km_ref[h, pl.ds(cur, 1), :] = jnp.mean(kn, axis=0, keepdims=True)
            k_ref[0, MLA_HEADS + h, rows, :] = (kn + onehot).astype(BF16)

            qh = pt[h * MOBA_HEAD_DIM:(h + 1) * MOBA_HEAD_DIM]
            ssq = jnp.sum(qh * qh, axis=0, keepdims=True)
            qn = qh * lax.rsqrt(ssq * (1.0 / MOBA_HEAD_DIM) + EPS) * gq_moba
            gate = _dot(km_ref[h][:, 0:MOBA_HEAD_DIM], qn, precision=lax.Precision.HIGHEST)
            cnt = jnp.zeros((nb, sub), F32)
            for jp in range(nb):
                gj = gate[jp:jp + 1, :]
                tie = jnp.where(gj == gate, lower[jp], 0.0)
                cnt = cnt + jnp.where(jp < cur, jnp.where(gj > gate, 1.0, tie), 0.0)
            bias = jnp.where(jidx < cur, jnp.where(cnt < float(MOBA_TOPK), 0.0, MASK_VALUE),
                             jnp.where(jidx == cur, 0.0, MASK_VALUE))
            qt_ref[0, MLA_HEADS + h, 0, :, rows] = (
                jnp.concatenate([qn, bias, pad_m], axis=0).astype(BF16))


def _attn_proj(x, gmix, p, tabs):
    B, S, D = x.shape
    T = ATT_TILE
    nt = S // T
    nb = S // MOBA_BLOCK
    consts = [gmix, p["w_std"], p["w_t"], p["g_cq"], p["g_ckv"], p["w_uq_t"], p["w_ukv_std"],
              p["w_ukv_vt"]]
    tail = [p["gq_mla"], p["gk_mla"], p["gq_moba"], p["gk_moba"]]
    in_specs = ([pl.BlockSpec((1, T, D), lambda b, t: (b, t, 0))]
                + [_const_spec(c.shape) for c in consts]
                + [pl.BlockSpec((MLA_ROPE, T), lambda b, t: (0, t)),
                   pl.BlockSpec((MLA_ROPE, T), lambda b, t: (0, t)),
                   pl.BlockSpec((T, HEAD_SLOT), lambda b, t: (t, 0)),
                   pl.BlockSpec((T, HEAD_SLOT), lambda b, t: (t, 0))]
                + [_const_spec(c.shape) for c in tail])
    out_shape = (jax.ShapeDtypeStruct((B, N_HEADS, nt, HEAD_SLOT, T), BF16),
                 jax.ShapeDtypeStruct((B, N_HEADS, S, HEAD_SLOT), BF16),
                 jax.ShapeDtypeStruct((B, nt, N_HEADS * MLA_V, T), BF16))
    out_specs = (pl.BlockSpec((1, N_HEADS, 1, HEAD_SLOT, T), lambda b, t: (b, 0, t, 0, 0)),
                 pl.BlockSpec((1, N_HEADS, T, HEAD_SLOT), lambda b, t: (b, 0, t, 0)),
                 pl.BlockSpec((1, 1, N_HEADS * MLA_V, T), lambda b, t: (b, t, 0, 0)))
    return pl.pallas_call(
        functools.partial(_attn_proj_kernel, tile=T),
        grid=(B, nt),
        in_specs=in_specs,
        out_specs=out_specs,
        out_shape=out_shape,
        scratch_shapes=[pltpu.VMEM((MOBA_HEADS, nb, HEAD_SLOT), F32)],
        compiler_params=pltpu.CompilerParams(
            dimension_semantics=("arbitrary", "arbitrary"), vmem_limit_bytes=VMEM_LIMIT),
        name="attn_proj",
    )(x, *consts, tabs["cos_t"], tabs["sin_t"], tabs["cos_r"], tabs["sin_r"], *tail)


def _flash_kernel(qt_ref, k_ref, vt_ref, o_ref, s_scr, p_scr, acc_scr):
    nh, nt, _, tile = qt_ref.shape[1:]
    nchunk = tile // SCORE_ROWS

    def score_chunk(hh, qidx, j, r, cmax):
        k0 = pl.multiple_of(j * tile, tile) + r * SCORE_ROWS
        s = _dot(k_ref[0, hh, pl.ds(k0, SCORE_ROWS), :], qt_ref[0, hh, qidx])
        s_scr[hh, r * SCORE_ROWS:(r + 1) * SCORE_ROWS, :] = s
        c = jnp.max(s, axis=0, keepdims=True)
        return c if cmax is None else jnp.maximum(cmax, c)

    def step(qi, j, ms, cmaxes, diagonal):
        ones = jnp.ones((SUM_ROWS, tile), BF16)
        new_ms, new_cmaxes = [], []
        for hh in range(nh):
            def chunk(r):
                s = s_scr[hh, r * SCORE_ROWS:(r + 1) * SCORE_ROWS, :]
                if diagonal:
                    key = lax.broadcasted_iota(jnp.int32, s.shape, 0) + r * SCORE_ROWS
                    qry = lax.broadcasted_iota(jnp.int32, s.shape, 1)
                    s = jnp.where(key <= qry, s, MASK_VALUE)
                return s
            cmax = cmaxes[hh]
            if diagonal:
                cmax = functools.reduce(jnp.maximum, [jnp.max(chunk(r), axis=0, keepdims=True)
                                                      for r in range(nchunk)])
            m_new = jnp.maximum(ms[hh], cmax)
            nxt = None
            for r in range(nchunk):
                p_scr[hh, r * SCORE_ROWS:(r + 1) * SCORE_ROWS, :] = (
                    jnp.exp2(chunk(r) - m_new).astype(BF16))
                if diagonal:
                    nxt = score_chunk(hh, jnp.minimum(qi + 1, nt - 1), 0, r, nxt)
                else:
                    nxt = score_chunk(hh, qi, j + 1, r, nxt)
            v_ext = jnp.concatenate([vt_ref[0, j, hh * MLA_V:(hh + 1) * MLA_V, :], ones], axis=0)
            acc_scr[hh] = jnp.exp2(ms[hh] - m_new) * acc_scr[hh] + _dot(v_ext, p_scr[hh])
            new_ms.append(m_new)
            new_cmaxes.append(nxt)
        return new_ms, new_cmaxes

    cmax0 = []
    for hh in range(nh):
        c = None
        for r in range(nchunk):
            c = score_chunk(hh, 0, 0, r, c)
        cmax0.append(c)
    m0 = jnp.full((1, tile), MASK_VALUE, F32)

    def query_tile(qi, cmaxes):
        acc_scr[...] = jnp.zeros_like(acc_scr)

        def body(t, carry):
            ms, cms = step(qi, t, carry[0:nh], carry[nh:], False)
            return tuple(ms) + tuple(cms)

        carry = lax.fori_loop(0, qi, body, (m0,) * nh + tuple(cmaxes))
        _, cms = step(qi, qi, carry[0:nh], carry[nh:], True)
        outs = []
        for hh in range(nh):
            acc = acc_scr[hh]
            outs.append(acc[0:MLA_V] * (1.0 / acc[MLA_V:MLA_V + 1]))
        o_ref[0, pl.ds(pl.multiple_of(qi * tile, tile), tile), :] = (
            jnp.concatenate(outs, axis=0).T.astype(BF16))
        return tuple(cms)

    lax.fori_loop(0, nt, query_tile, tuple(cmax0))


def _flash(qt, k, vt):
    B, H, nt, _, T = qt.shape
    S = nt * T
    nh = FLASH_HEADS
    return pl.pallas_call(
        _flash_kernel,
        grid=(B, H // nh),
        in_specs=[pl.BlockSpec((1, nh, nt, HEAD_SLOT, T), lambda b, p: (b, p, 0, 0, 0)),
                  pl.BlockSpec((1, nh, S, HEAD_SLOT), lambda b, p: (b, p, 0, 0)),
                  pl.BlockSpec((1, nt, nh * MLA_V, T), lambda b, p: (b, 0, p, 0))],
        out_specs=pl.BlockSpec((1, S, nh * MLA_V), lambda b, p: (b, 0, p)),
        out_shape=jax.ShapeDtypeStruct((B, S, H * MLA_V), BF16),
        scratch_shapes=[pltpu.VMEM((nh, T, T), F32), pltpu.VMEM((nh, T, T), BF16),
                        pltpu.VMEM((nh, MLA_V + SUM_ROWS, T), F32)],
        compiler_params=pltpu.CompilerParams(
            dimension_semantics=("arbitrary", "arbitrary"), vmem_limit_bytes=VMEM_LIMIT),
        name="flash",
    )(qt, k, vt)


def _ffn(x1, gffn_ref, w1_ref, w2_ref):
    hb = _rms_rows(x1, gffn_ref[...]).astype(BF16)
    acc = x1
    for c in range(D_FF // FF_CHUNK):
        a = _dot(hb, w1_ref[:, c * FF_CHUNK:(c + 1) * FF_CHUNK])
        a = jnp.square(jnp.maximum(a, 0.0)).astype(BF16)
        acc = acc + _dot(a, w2_ref[c * FF_CHUNK:(c + 1) * FF_CHUNK, :])
    return acc


def _emit_chunk_rows(x2, gnext_ref, u_scr, uc_ref, tile):
    u = _rms_rows(x2, gnext_ref[...])
    rows = tile // S5_L
    for j in range(N_SLABS):
        u_scr[j] = u[:, j * LANE:(j + 1) * LANE]
        for l in range(S5_L):
            uc_ref[j, 0, :, l * LANE:(l + 1) * LANE] = (
                u_scr[j, pl.ds(l, rows, stride=S5_L), :].astype(BF16))


def _attn_out_ffn_kernel(x_ref, o_ref, wo_ref, gffn_ref, w1_ref, w2_ref, gnext_ref,
                         x2_ref, uc_ref, u_scr, *, tile):
    x1 = x_ref[0] + _dot(o_ref[0], wo_ref[...])
    x2 = _ffn(x1, gffn_ref, w1_ref, w2_ref)
    x2_ref[0] = x2
    _emit_chunk_rows(x2, gnext_ref, u_scr, uc_ref, tile)


def _bgroup(B):
    return 8 if B % 8 == 0 else B


def _attn_out_ffn(x, o, wo, gffn, w1, w2, gnext):
    B, S, D = x.shape
    T = FFN_TILE
    row_w = S5_L * LANE
    consts_a = [wo, gffn, w1, w2, gnext]
    return pl.pallas_call(
        functools.partial(_attn_out_ffn_kernel, tile=T),
        grid=(B, S // T),
        in_specs=[pl.BlockSpec((1, T, D), lambda b, t: (b, t, 0)),
                  pl.BlockSpec((1, T, D), lambda b, t: (b, t, 0))]
                 + [_weight_spec(c) for c in consts_a],
        out_specs=(pl.BlockSpec((1, T, D), lambda b, t: (b, t, 0)),
                   pl.BlockSpec((N_SLABS, 1, T // S5_L, row_w), lambda b, t: (0, b, t, 0))),
        out_shape=(jax.ShapeDtypeStruct((B, S, D), F32),
                   jax.ShapeDtypeStruct((N_SLABS, B, S // S5_L, row_w), BF16)),
        scratch_shapes=[pltpu.VMEM((N_SLABS, T, LANE), F32)],
        compiler_params=pltpu.CompilerParams(
            dimension_semantics=("arbitrary", "arbitrary"), vmem_limit_bytes=VMEM_LIMIT),
        name="attn_out_ffn",
    )(x, o, *[_operand(c) for c in consts_a])


def _s5_expand(kcat_ref, fcat_ref, qcat_ref, wt_scr, min_scr, mout_scr):
    L, I, P = S5_L, S5_GROUP, S5_STATE
    half = S5_SLAB * P

    def same_group(shape, rows_per_group, cols_per_group):
        r = lax.broadcasted_iota(jnp.int32, shape, 0) // rows_per_group
        c = lax.broadcasted_iota(jnp.int32, shape, 1) // cols_per_group
        return r == c

    def tiled(block, mask):
        return jnp.where(mask, jnp.concatenate([block] * S5_SLAB, axis=0), 0.0).astype(BF16)

    m_kk = same_group((LANE, LANE), I, I)
    m_in = same_group((LANE, half), I, P)
    m_out = same_group((half, LANE), P, I)
    zeros = jnp.zeros((LANE, LANE), BF16)
    lag_blocks = [tiled(kcat_ref[0, s], m_kk) for s in range(L)]
    for li in range(L):
        for lo in range(L):
            wt_scr[li * LANE:(li + 1) * LANE, lo * LANE:(lo + 1) * LANE] = (
                lag_blocks[lo - li] if lo >= li else zeros)
    for c in range(2):
        for l in range(L):
            min_scr[l * LANE:(l + 1) * LANE, c * half:(c + 1) * half] = tiled(fcat_ref[0, c, l], m_in)
            mout_scr[c * half:(c + 1) * half, l * LANE:(l + 1) * LANE] = tiled(qcat_ref[0, c, l], m_out)


def _s5_kernel(u_ref, kcat_ref, fcat_ref, qcat_ref, a_ref, y_ref,
               wt_scr, min_scr, mout_scr, v_scr, h_scr, st_scr):
    @pl.when((pl.program_id(1) == 0) & (pl.program_id(2) == 0))
    def _():
        _s5_expand(kcat_ref, fcat_ref, qcat_ref, wt_scr, min_scr, mout_scr)

    @pl.when(pl.program_id(2) == 0)
    def _():
        st_scr[...] = jnp.zeros_like(st_scr)

    bg, rc, row_w = u_ref.shape[1:]
    n_tiles = st_scr.shape[0]
    half = n_tiles // 2
    u = u_ref[0].reshape(bg * rc, row_w)
    v = _dot(u, min_scr[...])
    for t in range(n_tiles):
        for s in range(bg):
            v_scr[t, pl.ds(s, rc, stride=bg), :] = v[s * rc:(s + 1) * rc, t * LANE:(t + 1) * LANE]
    a = a_ref[0]
    a_t = [jnp.broadcast_to(a[:, t * LANE:(t + 1) * LANE], (bg, LANE)) for t in range(n_tiles)]

    def step(c, h):
        r0 = pl.multiple_of(c * bg, bg)
        new = []
        for t in range(n_tiles):
            h_scr[t, pl.ds(r0, bg), :] = h[t]
        for t in range(half):
            new.append(a_t[t] * h[t] - a_t[half + t] * h[half + t] + v_scr[t, pl.ds(r0, bg), :])
        for t in range(half):
            new.append(a_t[t] * h[half + t] + a_t[half + t] * h[t] + v_scr[half + t, pl.ds(r0, bg), :])
        return tuple(new)

    h = lax.fori_loop(0, rc, step, tuple(st_scr[t] for t in range(n_tiles)))
    for t in range(n_tiles):
        st_scr[t] = h[t]
    h_in = jnp.concatenate(
        [jnp.concatenate([h_scr[t, pl.ds(s, rc, stride=bg), :] for t in range(n_tiles)], axis=1)
         for s in range(bg)], axis=0)
    y = _dot(u, wt_scr[...]) + _dot(h_in.astype(BF16), mout_scr[...])
    y_ref[0] = y.reshape(bg, rc, row_w)


def _s5_core(uc, p):
    n_slab, B, n_chunks, row_w = uc.shape
    bg = _bgroup(B)
    rc = min(S5_ROWS // bg, n_chunks)
    n_state = 2 * S5_SLAB * S5_STATE
    n_tiles = n_state // LANE
    tables = [p["kcat"], p["fcat"], p["qcat"], p["a_chunk"]]

    def slab_spec(a):
        nd = a.ndim - 1
        return pl.BlockSpec((1,) + a.shape[1:], lambda j, g, r: (j,) + (0,) * nd)

    return pl.pallas_call(
        _s5_kernel,
        grid=(n_slab, B // bg, n_chunks // rc),
        in_specs=[pl.BlockSpec((1, bg, rc, row_w), lambda j, g, r: (j, g, r, 0))]
                 + [slab_spec(a) for a in tables],
        out_specs=pl.BlockSpec((1, bg, rc, row_w), lambda j, g, r: (j, g, r, 0)),
        out_shape=jax.ShapeDtypeStruct((n_slab, B, n_chunks, row_w), F32),
        scratch_shapes=[pltpu.VMEM((row_w, row_w), BF16), pltpu.VMEM((row_w, n_state), BF16),
                        pltpu.VMEM((n_state, row_w), BF16),
                        pltpu.VMEM((n_tiles, bg * rc, LANE), F32),
                        pltpu.VMEM((n_tiles, bg * rc, LANE), F32),
                        pltpu.VMEM((n_tiles, bg, LANE), F32)],
        compiler_params=pltpu.CompilerParams(
            dimension_semantics=("arbitrary", "arbitrary", "arbitrary"),
            vmem_limit_bytes=VMEM_LIMIT),
        name="s5_core",
    )(uc, *tables)


def _gelu_tanh(y):
    c = math.sqrt(2.0 / math.pi)
    return 0.5 * y * (1.0 + jnp.tanh(c * (y + 0.044715 * (y * y * y))))


def _s5_out_ffn_kernel(x_ref, yc_ref, gmix_ref, dskip_ref, wglu_ref, gffn_ref, w1_ref, w2_ref,
                       x2_ref, y_scr, *, tile):
    x = x_ref[0]
    rows = tile // S5_L
    for j in range(N_SLABS):
        for l in range(S5_L):
            y_scr[j, pl.ds(l, rows, stride=S5_L), :] = yc_ref[j, 0, :, l * LANE:(l + 1) * LANE]
    y = jnp.concatenate([y_scr[j] for j in range(N_SLABS)], axis=1)
    u = _rms_rows(x, gmix_ref[...])
    g = _gelu_tanh(y + dskip_ref[...] * u).astype(BF16)
    vg = _dot(g, wglu_ref[...])
    x1 = x + vg[:, :D_MODEL] * jax.nn.sigmoid(vg[:, D_MODEL:])
    x2_ref[0] = _ffn(x1, gffn_ref, w1_ref, w2_ref)


def _s5_out_ffn(x, yc, gmix, dskip, wglu, gffn, w1, w2):
    B, S, D = x.shape
    T = FFN_TILE
    row_w = S5_L * LANE
    consts = [gmix, dskip, wglu, gffn, w1, w2]
    return pl.pallas_call(
        functools.partial(_s5_out_ffn_kernel, tile=T),
        grid=(B, S // T),
        in_specs=[pl.BlockSpec((1, T, D), lambda b, t: (b, t, 0)),
                  pl.BlockSpec((N_SLABS, 1, T // S5_L, row_w), lambda b, t: (0, b, t, 0))]
                 + [_weight_spec(c) for c in consts],
        out_specs=pl.BlockSpec((1, T, D), lambda b, t: (b, t, 0)),
        out_shape=jax.ShapeDtypeStruct((B, S, D), F32),
        scratch_shapes=[pltpu.VMEM((N_SLABS, T, LANE), F32)],
        compiler_params=pltpu.CompilerParams(
            dimension_semantics=("arbitrary", "arbitrary"), vmem_limit_bytes=VMEM_LIMIT),
        name="s5_out_ffn",
    )(x, yc, *[_operand(c) for c in consts])


def _rot_cols(w):
    half = w.shape[1] // 2
    return jnp.concatenate([-w[:, half:], w[:, :half]], axis=1)


def _slot_cols(w, n_heads, width, offset=0):
    k = w.shape[0]
    w = w.reshape(k, n_heads, width)
    w = jnp.pad(w, ((0, 0), (0, 0), (offset, HEAD_SLOT - width - offset)))
    return w.reshape(k, n_heads * HEAD_SLOT)


def _attn_params(w_in, g_cq, w_uq, g_ckv, w_ukv, g_qn_mla, g_kn_mla, g_qn_moba, g_kn_moba):
    sizes = [MLA_Q_RANK, MLA_KV_RANK, MLA_ROPE, MOBA_W, MOBA_W, MOBA_W]
    offs = [0]
    for s in sizes:
        offs.append(offs[-1] + s)
    w_cq, w_ckv, w_kr, w_qb, w_kb, w_vb = [w_in[:, offs[i]:offs[i + 1]] for i in range(6)]
    w_std = jnp.concatenate([w_cq, w_ckv,
                             _slot_cols(w_kr, 1, MLA_ROPE, MLA_NOPE),
                             _slot_cols(_rot_cols(w_kr), 1, MLA_ROPE, MLA_NOPE),
                             _slot_cols(w_kb, MOBA_HEADS, MOBA_HEAD_DIM)], axis=1)
    w_t = jnp.concatenate([w_qb, w_vb], axis=1).T
    wq = w_uq.reshape(MLA_Q_RANK, MLA_HEADS, MLA_QK)
    wq_nope, wq_rope = wq[:, :, :MLA_NOPE], wq[:, :, MLA_NOPE:]
    wq_rot = jnp.concatenate([-wq_rope[:, :, MLA_ROPE // 2:], wq_rope[:, :, :MLA_ROPE // 2]], axis=2)
    w_uq_t = jnp.concatenate([wq_nope, wq_rope, wq_rot], axis=2).reshape(
        MLA_Q_RANK, MLA_HEADS * HEAD_SLOT).T
    wkv = w_ukv.reshape(MLA_KV_RANK, MLA_HEADS, MLA_NOPE + MLA_V)
    w_ukv_std = _slot_cols(wkv[:, :, :MLA_NOPE].reshape(MLA_KV_RANK, MLA_HEADS * MLA_NOPE),
                           MLA_HEADS, MLA_NOPE)
    w_ukv_vt = wkv[:, :, MLA_NOPE:].reshape(MLA_KV_RANK, MLA_HEADS * MLA_V).T
    return {
        "w_std": w_std.astype(BF16), "w_t": w_t.astype(BF16),
        "g_cq": g_cq.reshape(1, -1), "g_ckv": g_ckv.reshape(1, -1),
        "w_uq_t": w_uq_t.astype(BF16), "w_ukv_std": w_ukv_std.astype(BF16),
        "w_ukv_vt": w_ukv_vt.astype(BF16),
        "gq_mla": (g_qn_mla * (MLA_QK ** -0.5 * LOG2E)).reshape(-1, 1),
        "gk_mla": jnp.pad(g_kn_mla, (0, HEAD_SLOT - MLA_QK)).reshape(1, -1),
        "gq_moba": (g_qn_moba * (MOBA_HEAD_DIM ** -0.5 * LOG2E)).reshape(-1, 1),
        "gk_moba": jnp.pad(g_kn_moba, (0, HEAD_SLOT - MOBA_HEAD_DIM)).reshape(1, -1),
    }


def _rope_tables(S):
    half = MLA_ROPE // 2
    inv = ROPE_THETA ** (-jnp.arange(half, dtype=F32) / half)
    ang = jnp.arange(S).astype(F32)[:, None] * inv[None, :]
    cos = jnp.tile(jnp.cos(ang), (1, 2))
    sin = jnp.tile(jnp.sin(ang), (1, 2))
    pad = ((0, 0), (MLA_NOPE, HEAD_SLOT - MLA_QK))
    return {"cos_t": cos.T, "sin_t": sin.T, "cos_r": jnp.pad(cos, pad), "sin_r": jnp.pad(sin, pad)}


def _s5_params(lam_re, lam_im, log_dt, b_re, b_im, c_re, c_im):
    hi = lax.Precision.HIGHEST
    G, P, L = S5_GROUPS, S5_STATE, S5_L
    dt = jnp.exp(log_dt)[:, None]

    steps = jnp.arange(L + 1, dtype=F32)[:, None, None]
    mag = jnp.exp(lam_re * dt * steps)
    pw_re = mag * jnp.cos(lam_im * dt * steps)
    pw_im = mag * jnp.sin(lam_im * dt * steps)
    a_re, a_im = pw_re[1], pw_im[1]
    den = lam_re * lam_re + lam_im * lam_im
    k_re = ((a_re - 1.0) * lam_re + a_im * lam_im) / den
    k_im = (a_im * lam_re - (a_re - 1.0) * lam_im) / den
    bb_re = k_re[..., None] * b_re - k_im[..., None] * b_im
    bb_im = k_re[..., None] * b_im + k_im[..., None] * b_re

    e_re = pw_re[:L, ..., None] * bb_re - pw_im[:L, ..., None] * bb_im
    e_im = pw_re[:L, ..., None] * bb_im + pw_im[:L, ..., None] * bb_re
    lag = (jnp.einsum("gop,sgpi->sgoi", c_re, e_re, precision=hi)
           - jnp.einsum("gop,sgpi->sgoi", c_im, e_im, precision=hi))
    I = S5_GROUP
    kcat = lag.reshape(L, N_SLABS, S5_SLAB, I, I)
    kcat = kcat.transpose(1, 0, 4, 2, 3).reshape(N_SLABS, L, I, S5_SLAB * I)

    f = jnp.stack([e_re[::-1], e_im[::-1]], axis=0)
    f = f.reshape(2, L, N_SLABS, S5_SLAB, P, I)
    fcat = f.transpose(2, 0, 1, 5, 3, 4).reshape(N_SLABS, 2, L, I, S5_SLAB * P)

    nr, ni = pw_re[1:, :, None, :], pw_im[1:, :, None, :]
    q = jnp.stack([c_re * nr - c_im * ni, -(c_re * ni + c_im * nr)], axis=0)
    q = q.reshape(2, L, N_SLABS, S5_SLAB, I, P)
    qcat = q.transpose(2, 0, 1, 5, 3, 4).reshape(N_SLABS, 2, L, P, S5_SLAB * I)

    al_re, al_im = pw_re[L], pw_im[L]
    a_chunk = jnp.concatenate([al_re.reshape(N_SLABS, 1, S5_SLAB * P),
                               al_im.reshape(N_SLABS, 1, S5_SLAB * P)], axis=2)
    return {"kcat": kcat, "fcat": fcat, "qcat": qcat, "a_chunk": a_chunk}


def _cast_kernel(w_ref, o_ref):
    o_ref[...] = w_ref[...].astype(BF16)


def _to_bf16(w):
    n, r, c = w.shape
    rb = min(r, CAST_BLOCK_BYTES // (4 * c))
    spec = pl.BlockSpec((1, rb, c), lambda i, j: (i, j, 0))
    return pl.pallas_call(
        _cast_kernel, grid=(n, r // rb), in_specs=[spec], out_specs=spec,
        out_shape=jax.ShapeDtypeStruct(w.shape, BF16), name="to_bf16")(w)


def kernel(x, mix_norm_g, ffn_norm_g, w_in, g_cq, w_uq, g_ckv, w_ukv, g_qn_mla, g_kn_mla,
           g_qn_moba, g_kn_moba, w_o, lam_re, lam_im, log_dt, b_re, b_im, c_re, c_im,
           d_skip, w_glu, w_ff1, w_ff2):
    B, S, D = x.shape
    depth = mix_norm_g.shape[0]
    assert D == D_MODEL and S % ATT_TILE == 0 and S % FFN_TILE == 0 and depth % 2 == 0
    tabs = _rope_tables(S)
    row = lambda v: v.reshape(1, -1)
    w_ff1, w_ff2, w_glu, w_o = _to_bf16(w_ff1), _to_bf16(w_ff2), _to_bf16(w_glu), _to_bf16(w_o)
    for layer in range(0, depth, 2):
        i = layer // 2
        ap = _attn_params(w_in[i], g_cq[i], w_uq[i], g_ckv[i], w_ukv[i], g_qn_mla[i], g_kn_mla[i],
                          g_qn_moba[i], g_kn_moba[i])
        qt, k, vt = _attn_proj(x, row(mix_norm_g[layer]), ap, tabs)
        o = _flash(qt, k, vt)
        x, uc = _attn_out_ffn(x, o, _Layer(w_o, i), row(ffn_norm_g[layer]), _Layer(w_ff1, layer),
                              _Layer(w_ff2, layer), row(mix_norm_g[layer + 1]))
        sp = _s5_params(lam_re[i], lam_im[i], log_dt[i], b_re[i], b_im[i], c_re[i], c_im[i])
        yc = _s5_core(uc, sp)
        x = _s5_out_ffn(x, yc, row(mix_norm_g[layer + 1]), row(d_skip[i]), _Layer(w_glu, i),
                        row(ffn_norm_g[layer + 1]), _Layer(w_ff1, layer + 1),
                        _Layer(w_ff2, layer + 1))
    return x
```

```python
import functools
import math
from typing import NamedTuple

import jax
import jax.numpy as jnp
from jax import lax
from jax.experimental import pallas as pl
from jax.experimental.pallas import tpu as pltpu

F32 = jnp.float32
BF16 = jnp.bfloat16

D_MODEL = 1024
D_FF = 4 * D_MODEL
EPS = 1e-6
MLA_HEADS = 8
MLA_NOPE = 64
MLA_ROPE = 32
MLA_V = 64
MLA_QK = MLA_NOPE + MLA_ROPE
MLA_Q_RANK = 256
MLA_KV_RANK = 128
ROPE_THETA = 10000.0
MOBA_HEADS = 8
MOBA_HEAD_DIM = 64
MOBA_W = MOBA_HEADS * MOBA_HEAD_DIM
MOBA_BLOCK = 256
MOBA_TOPK = 3
S5_GROUP = 16
S5_GROUPS = D_MODEL // S5_GROUP
S5_STATE = 64

LANE = 128
HEAD_SLOT = 128
N_HEADS = MLA_HEADS + MOBA_HEADS
MASK_VALUE = -1e30
SUM_ROWS = 16
LOG2E = math.log2(math.e)
ATT_TILE = 512
PROJ_TILE = 1024
SCORE_ROWS = 256
FLASH_HEADS = 4
FFN_TILE = 512
FF_CHUNK = 1024
S5_L = 8
S5_SLAB = LANE // S5_GROUP
N_SLABS = D_MODEL // LANE
S5_ROWS = 1024
VMEM_LIMIT = 56 * 1024 * 1024
CAST_BLOCK_BYTES = 4 * 1024 * 1024

_NT = (((1,), (1,)), ((), ()))


def _dot(a, b, precision=None):
    return jnp.dot(a, b, preferred_element_type=F32, precision=precision)


def _dot_nt(a, b):
    return lax.dot_general(a, b, _NT, preferred_element_type=F32)


def _rms_rows(x, gain_row):
    ms = jnp.mean(x * x, axis=-1, keepdims=True)
    return x * lax.rsqrt(ms + EPS) * gain_row


def _const_spec(shape):
    nd = len(shape)
    return pl.BlockSpec(shape, lambda *_: (0,) * nd, pipeline_mode=pl.Buffered(1))


class _Layer(NamedTuple):
    stack: jax.Array
    index: int


def _weight_spec(w):
    if isinstance(w, _Layer):
        nd = w.stack.ndim - 1
        return pl.BlockSpec((None,) + w.stack.shape[1:], lambda *_: (w.index,) + (0,) * nd,
                            pipeline_mode=pl.Buffered(1))
    return _const_spec(w.shape)


def _operand(w):
    return w.stack if isinstance(w, _Layer) else w


def _attn_proj_kernel(x_ref, gmix_ref, wstd_ref, wt_ref, gcq_ref, gckv_ref, wuqt_ref, wukv_ref,
                      wukvvt_ref, cost_ref, sint_ref, cosr_ref, sinr_ref, gqmla_ref, gkmla_ref,
                      gqmoba_ref, gkmoba_ref, qt_ref, k_ref, vt_ref, km_ref, *, tile):
    ti = pl.program_id(1)
    nsub = tile // MOBA_BLOCK
    sub = MOBA_BLOCK
    per_q = qt_ref.shape[4] // sub

    @pl.when(ti == 0)
    def _():
        km_ref[...] = jnp.zeros_like(km_ref)

    gq_mla = gqmla_ref[...]
    gk_mla = gkmla_ref[...]
    gq_moba = gqmoba_ref[...]
    gk_moba = gkmoba_ref[...]
    nb = km_ref.shape[1]
    pad_q = jnp.zeros((HEAD_SLOT - MLA_QK, sub), F32)
    pad_m = jnp.zeros((HEAD_SLOT - MOBA_HEAD_DIM - nb, sub), F32)
    jidx = lax.broadcasted_iota(jnp.int32, (nb, sub), 0)
    lower = [jnp.where(jp < jidx, 1.0, 0.0) for jp in range(nb)]
    lane = lax.broadcasted_iota(jnp.int32, (sub, HEAD_SLOT), 1)

    def project(sb):
        rows = slice(sb * sub, (sb + 1) * sub)
        hb = _rms_rows(x_ref[0, rows, :], gmix_ref[...]).astype(BF16)
        pstd = _dot(hb, wstd_ref[...])
        pt = _dot_nt(wt_ref[...], hb)
        return pstd, pt

    def finish(sb, pstd, pt):
        rows = slice(sb * sub, (sb + 1) * sub)
        cols = slice((sb % per_q) * sub, (sb % per_q + 1) * sub)
        qtile = sb // per_q
        cur = ti * nsub + sb
        o = 0
        cq = pstd[:, o:o + MLA_Q_RANK]; o += MLA_Q_RANK
        ckv = pstd[:, o:o + MLA_KV_RANK]; o += MLA_KV_RANK
        kr_a = pstd[:, o:o + HEAD_SLOT]; o += HEAD_SLOT
        kr_b = pstd[:, o:o + HEAD_SLOT]; o += HEAD_SLOT
        kb_off = o

        cqn = _rms_rows(cq, gcq_ref[...]).astype(BF16)
        ckvn = _rms_rows(ckv, gckv_ref[...]).astype(BF16)
        qup_t = _dot_nt(wuqt_ref[...], cqn)
        kv_std = _dot(ckvn, wukv_ref[...])
        vt_ref[0, qtile, 0:MLA_HEADS * MLA_V, cols] = _dot_nt(wukvvt_ref[...], ckvn).astype(BF16)
        vt_ref[0, qtile, MLA_HEADS * MLA_V:, cols] = pt[MOBA_W:, :].astype(BF16)
        k_rope = kr_a * cosr_ref[rows, :] + kr_b * sinr_ref[rows, :]
        cos_t = cost_ref[:, rows]
        sin_t = sint_ref[:, rows]
        for h in range(MLA_HEADS):
            blk = qup_t[h * HEAD_SLOT:(h + 1) * HEAD_SLOT]
            nope = blk[0:MLA_NOPE]
            rope = (blk[MLA_NOPE:MLA_QK] * cos_t + blk[MLA_QK:MLA_QK + MLA_ROPE] * sin_t)
            ssq = (jnp.sum(nope * nope, axis=0, keepdims=True)
                   + jnp.sum(rope * rope, axis=0, keepdims=True))
            r = lax.rsqrt(ssq * (1.0 / MLA_QK) + EPS)
            qn = jnp.concatenate([nope * r * gq_mla[0:MLA_NOPE], rope * r * gq_mla[MLA_NOPE:MLA_QK],
                                  pad_q], axis=0)
            qt_ref[0, h, qtile, :, cols] = qn.astype(BF16)
            kh = kv_std[:, h * HEAD_SLOT:(h + 1) * HEAD_SLOT] + k_rope
            ssk = jnp.sum(kh * kh, axis=-1, keepdims=True)
            k_ref[0, h, rows, :] = (kh * lax.rsqrt(ssk * (1.0 / MLA_QK) + EPS) * gk_mla).astype(BF16)

        onehot = jnp.where(lane == MOBA_HEAD_DIM + cur, 1.0, 0.0)
        for h in range(MOBA_HEADS):
            kh = pstd[:, kb_off + h * HEAD_SLOT:kb_off + (h + 1) * HEAD_SLOT]
            ssk = jnp.sum(kh * kh, axis=-1, keepdims=True)
            kn = kh * lax.rsqrt(ssk * (1.0 / MOBA_HEAD_DIM) + EPS) * gk_moba
            km_ref[h, pl.ds(cur, 1), :] = jnp.mean(kn, axis=0, keepdims=True)
            k_ref[0, MLA_HEADS + h, rows, :] = (kn + onehot).astype(BF16)

            qh = pt[h * MOBA_HEAD_DIM:(h + 1) * MOBA_HEAD_DIM]
            ssq = jnp.sum(qh * qh, axis=0, keepdims=True)
            qn = qh * lax.rsqrt(ssq * (1.0 / MOBA_HEAD_DIM) + EPS) * gq_moba
            gate = _dot(km_ref[h][:, 0:MOBA_HEAD_DIM], qn, precision=lax.Precision.HIGHEST)
            cnt = jnp.zeros((nb, sub), F32)
            for jp in range(nb):
                gj = gate[jp:jp + 1, :]
                tie = jnp.where(gj == gate, lower[jp], 0.0)
                cnt = cnt + jnp.where(jp < cur, jnp.where(gj > gate, 1.0, tie), 0.0)
            bias = jnp.where(jidx < cur, jnp.where(cnt < float(MOBA_TOPK), 0.0, MASK_VALUE),
                             jnp.where(jidx == cur, 0.0, MASK_VALUE))
            qt_ref[0, MLA_HEADS + h, qtile, :, cols] = (
                jnp.concatenate([qn, bias, pad_m], axis=0).astype(BF16))

    nxt = project(0)
    for sb in range(nsub):
        cur_vals = nxt
        if sb + 1 < nsub:
            nxt = project(sb + 1)
        finish(sb, *cur_vals)


def _attn_proj(x, gmix, p, tabs):
    B, S, D = x.shape
    T = min(PROJ_TILE, S)
    tf = ATT_TILE
    per = T // tf
    nt = S // T
    nb = S // MOBA_BLOCK
    consts = [gmix, p["w_std"], p["w_t"], p["g_cq"], p["g_ckv"], p["w_uq_t"], p["w_ukv_std"],
              p["w_ukv_vt"]]
    tail = [p["gq_mla"], p["gk_mla"], p["gq_moba"], p["gk_moba"]]
    in_specs = ([pl.BlockSpec((1, T, D), lambda b, t: (b, t, 0))]
                + [_const_spec(c.shape) for c in consts]
                + [pl.BlockSpec((MLA_ROPE, T), lambda b, t: (0, t)),
                   pl.BlockSpec((MLA_ROPE, T), lambda b, t: (0, t)),
                   pl.BlockSpec((T, HEAD_SLOT), lambda b, t: (t, 0)),
                   pl.BlockSpec((T, HEAD_SLOT), lambda b, t: (t, 0))]
                + [_const_spec(c.shape) for c in tail])
    out_shape = (jax.ShapeDtypeStruct((B, N_HEADS, S // tf, HEAD_SLOT, tf), BF16),
                 jax.ShapeDtypeStruct((B, N_HEADS, S, HEAD_SLOT), BF16),
                 jax.ShapeDtypeStruct((B, S // tf, N_HEADS * MLA_V, tf), BF16))
    out_specs = (pl.BlockSpec((1, N_HEADS, per, HEAD_SLOT, tf), lambda b, t: (b, 0, t, 0, 0)),
                 pl.BlockSpec((1, N_HEADS, T, HEAD_SLOT), lambda b, t: (b, 0, t, 0)),
                 pl.BlockSpec((1, per, N_HEADS * MLA_V, tf), lambda b, t: (b, t, 0, 0)))
    return pl.pallas_call(
        functools.partial(_attn_proj_kernel, tile=T),
        grid=(B, nt),
        in_specs=in_specs,
        out_specs=out_specs,
        out_shape=out_shape,
        scratch_shapes=[pltpu.VMEM((MOBA_HEADS, nb, HEAD_SLOT), F32)],
        compiler_params=pltpu.CompilerParams(
            dimension_semantics=("arbitrary", "arbitrary"), vmem_limit_bytes=VMEM_LIMIT),
        name="attn_proj",
    )(x, *consts, tabs["cos_t"], tabs["sin_t"], tabs["cos_r"], tabs["sin_r"], *tail)


def _flash_kernel(qt_ref, k_ref, vt_ref, o_ref, s_scr, p_scr, acc_scr):
    nh, nt, _, tile = qt_ref.shape[1:]
    nchunk = tile // SCORE_ROWS

    def score_chunk(hh, qidx, j, r, cmax):
        k0 = pl.multiple_of(j * tile, tile) + r * SCORE_ROWS
        s = _dot(k_ref[0, hh, pl.ds(k0, SCORE_ROWS), :], qt_ref[0, hh, qidx])
        s_scr[hh, r * SCORE_ROWS:(r + 1) * SCORE_ROWS, :] = s
        c = jnp.max(s, axis=0, keepdims=True)
        return c if cmax is None else jnp.maximum(cmax, c)

    def step(qi, j, ms, cmaxes, diagonal):
        ones = jnp.ones((SUM_ROWS, tile), BF16)
        new_ms, new_cmaxes = [], []
        for hh in range(nh):
            def chunk(r):
                s = s_scr[hh, r * SCORE_ROWS:(r + 1) * SCORE_ROWS, :]
                if diagonal:
                    key = lax.broadcasted_iota(jnp.int32, s.shape, 0) + r * SCORE_ROWS
                    qry = lax.broadcasted_iota(jnp.int32, s.shape, 1)
                    s = jnp.where(key <= qry, s, MASK_VALUE)
                return s
            cmax = cmaxes[hh]
            if diagonal:
                cmax = functools.reduce(jnp.maximum, [jnp.max(chunk(r), axis=0, keepdims=True)
                                                      for r in range(nchunk)])
            m_new = jnp.maximum(ms[hh], cmax)
            nxt = None
            for r in range(nchunk):
                p_scr[hh, r * SCORE_ROWS:(r + 1) * SCORE_ROWS, :] = (
                    jnp.exp2(chunk(r) - m_new).astype(BF16))
                if diagonal:
                    nxt = score_chunk(hh, jnp.minimum(qi + 1, nt - 1), 0, r, nxt)
                else:
                    nxt = score_chunk(hh, qi, j + 1, r, nxt)
            v_ext = jnp.concatenate([vt_ref[0, j, hh * MLA_V:(hh + 1) * MLA_V, :], ones], axis=0)
            acc_scr[hh] = jnp.exp2(ms[hh] - m_new) * acc_scr[hh] + _dot(v_ext, p_scr[hh])
            new_ms.append(m_new)
            new_cmaxes.append(nxt)
        return new_ms, new_cmaxes

    cmax0 = []
    for hh in range(nh):
        c = None
        for r in range(nchunk):
            c = score_chunk(hh, 0, 0, r, c)
        cmax0.append(c)
    m0 = jnp.full((1, tile), MASK_VALUE, F32)

    def query_tile(qi, cmaxes):
        acc_scr[...] = jnp.zeros_like(acc_scr)

        def body(t, carry):
            ms, cms = step(qi, t, carry[0:nh], carry[nh:], False)
            return tuple(ms) + tuple(cms)

        carry = lax.fori_loop(0, qi, body, (m0,) * nh + tuple(cmaxes))
        _, cms = step(qi, qi, carry[0:nh], carry[nh:], True)
        outs = []
        for hh in range(nh):
            acc = acc_scr[hh]
            outs.append(acc[0:MLA_V] * (1.0 / acc[MLA_V:MLA_V + 1]))
        o_ref[0, pl.ds(pl.multiple_of(qi * tile, tile), tile), :] = (
            jnp.concatenate(outs, axis=0).T.astype(BF16))
        return tuple(cms)

    lax.fori_loop(0, nt, query_tile, tuple(cmax0))


def _flash(qt, k, vt):
    B, H, nt, _, T = qt.shape
    S = nt * T
    nh = FLASH_HEADS
    return pl.pallas_call(
        _flash_kernel,
        grid=(B, H // nh),
        in_specs=[pl.BlockSpec((1, nh, nt, HEAD_SLOT, T), lambda b, p: (b, p, 0, 0, 0)),
                  pl.BlockSpec((1, nh, S, HEAD_SLOT), lambda b, p: (b, p, 0, 0)),
                  pl.BlockSpec((1, nt, nh * MLA_V, T), lambda b, p: (b, 0, p, 0))],
        out_specs=pl.BlockSpec((1, S, nh * MLA_V), lambda b, p: (b, 0, p)),
        out_shape=jax.ShapeDtypeStruct((B, S, H * MLA_V), BF16),
        scratch_shapes=[pltpu.VMEM((nh, T, T), F32), pltpu.VMEM((nh, T, T), BF16),
                        pltpu.VMEM((nh, MLA_V + SUM_ROWS, T), F32)],
        compiler_params=pltpu.CompilerParams(
            dimension_semantics=("arbitrary", "arbitrary"), vmem_limit_bytes=VMEM_LIMIT),
        name="flash",
    )(qt, k, vt)


def _ffn(x1, gffn_ref, w1_ref, w2_ref):
    hb = _rms_rows(x1, gffn_ref[...]).astype(BF16)
    acc = x1
    for c in range(D_FF // FF_CHUNK):
        a = _dot(hb, w1_ref[:, c * FF_CHUNK:(c + 1) * FF_CHUNK])
        a = jnp.square(jnp.maximum(a, 0.0)).astype(BF16)
        acc = acc + _dot(a, w2_ref[c * FF_CHUNK:(c + 1) * FF_CHUNK, :])
    return acc


def _emit_chunk_rows(x2, gnext_ref, u_scr, uc_ref, tile):
    u = _rms_rows(x2, gnext_ref[...])
    rows = tile // S5_L
    for j in range(N_SLABS):
        u_scr[j] = u[:, j * LANE:(j + 1) * LANE]
        for l in range(S5_L):
            uc_ref[j, 0, :, l * LANE:(l + 1) * LANE] = (
                u_scr[j, pl.ds(l, rows, stride=S5_L), :].astype(BF16))


def _attn_out_ffn_kernel(x_ref, o_ref, wo_ref, gffn_ref, w1_ref, w2_ref, gnext_ref,
                         x2_ref, uc_ref, u_scr, *, tile):
    x1 = x_ref[0] + _dot(o_ref[0], wo_ref[...])
    x2 = _ffn(x1, gffn_ref, w1_ref, w2_ref)
    x2_ref[0] = x2
    _emit_chunk_rows(x2, gnext_ref, u_scr, uc_ref, tile)


def _bgroup(B):
    return 8 if B % 8 == 0 else B


def _attn_out_ffn(x, o, wo, gffn, w1, w2, gnext):
    B, S, D = x.shape
    T = FFN_TILE
    row_w = S5_L * LANE
    consts_a = [wo, gffn, w1, w2, gnext]
    return pl.pallas_call(
        functools.partial(_attn_out_ffn_kernel, tile=T),
        grid=(B, S // T),
        in_specs=[pl.BlockSpec((1, T, D), lambda b, t: (b, t, 0)),
                  pl.BlockSpec((1, T, D), lambda b, t: (b, t, 0))]
                 + [_weight_spec(c) for c in consts_a],
        out_specs=(pl.BlockSpec((1, T, D), lambda b, t: (b, t, 0)),
                   pl.BlockSpec((N_SLABS, 1, T // S5_L, row_w), lambda b, t: (0, b, t, 0))),
        out_shape=(jax.ShapeDtypeStruct((B, S, D), F32),
                   jax.ShapeDtypeStruct((N_SLABS, B, S // S5_L, row_w), BF16)),
        scratch_shapes=[pltpu.VMEM((N_SLABS, T, LANE), F32)],
        compiler_params=pltpu.CompilerParams(
            dimension_semantics=("arbitrary", "arbitrary"), vmem_limit_bytes=VMEM_LIMIT),
        name="attn_out_ffn",
    )(x, o, *[_operand(c) for c in consts_a])


def _s5_expand(kcat_ref, fcat_ref, qcat_ref, wt_scr, min_scr, mout_scr):
    L, I, P = S5_L, S5_GROUP, S5_STATE
    half = S5_SLAB * P

    def same_group(shape, rows_per_group, cols_per_group):
        r = lax.broadcasted_iota(jnp.int32, shape, 0) // rows_per_group
        c = lax.broadcasted_iota(jnp.int32, shape, 1) // cols_per_group
        return r == c

    def tiled(block, mask):
        return jnp.where(mask, jnp.concatenate([block] * S5_SLAB, axis=0), 0.0).astype(BF16)

    m_kk = same_group((LANE, LANE), I, I)
    m_in = same_group((LANE, half), I, P)
    m_out = same_group((half, LANE), P, I)
    zeros = jnp.zeros((LANE, LANE), BF16)
    lag_blocks = [tiled(kcat_ref[0, s], m_kk) for s in range(L)]
    for li in range(L):
        for lo in range(L):
            wt_scr[li * LANE:(li + 1) * LANE, lo * LANE:(lo + 1) * LANE] = (
                lag_blocks[lo - li] if lo >= li else zeros)
    for c in range(2):
        for l in range(L):
            min_scr[l * LANE:(l + 1) * LANE, c * half:(c + 1) * half] = tiled(fcat_ref[0, c, l], m_in)
            mout_scr[c * half:(c + 1) * half, l * LANE:(l + 1) * LANE] = tiled(qcat_ref[0, c, l], m_out)


def _s5_kernel(u_ref, kcat_ref, fcat_ref, qcat_ref, a_ref, y_ref,
               wt_scr, min_scr, mout_scr, v_scr, h_scr, st_scr):
    @pl.when((pl.program_id(1) == 0) & (pl.program_id(2) == 0))
    def _():
        _s5_expand(kcat_ref, fcat_ref, qcat_ref, wt_scr, min_scr, mout_scr)

    @pl.when(pl.program_id(2) == 0)
    def _():
        st_scr[...] = jnp.zeros_like(st_scr)

    bg, rc, row_w = u_ref.shape[1:]
    n_tiles = st_scr.shape[0]
    half = n_tiles // 2
    u = u_ref[0].reshape(bg * rc, row_w)
    v = _dot(u, min_scr[...])
    for t in range(n_tiles):
        for s in range(bg):
            v_scr[t, pl.ds(s, rc, stride=bg), :] = v[s * rc:(s + 1) * rc, t * LANE:(t + 1) * LANE]
    a = a_ref[0]
    a_t = [jnp.broadcast_to(a[:, t * LANE:(t + 1) * LANE], (bg, LANE)) for t in range(n_tiles)]

    def step(c, h):
        r0 = pl.multiple_of(c * bg, bg)
        new = []
        for t in range(n_tiles):
            h_scr[t, pl.ds(r0, bg), :] = h[t]
        for t in range(half):
            new.append(a_t[t] * h[t] - a_t[half + t] * h[half + t] + v_scr[t, pl.ds(r0, bg), :])
        for t in range(half):
            new.append(a_t[t] * h[half + t] + a_t[half + t] * h[t] + v_scr[half + t, pl.ds(r0, bg), :])
        return tuple(new)

    h = lax.fori_loop(0, rc, step, tuple(st_scr[t] for t in range(n_tiles)))
    for t in range(n_tiles):
        st_scr[t] = h[t]
    h_in = jnp.concatenate(
        [jnp.concatenate([h_scr[t, pl.ds(s, rc, stride=bg), :] for t in range(n_tiles)], axis=1)
         for s in range(bg)], axis=0)
    y = _dot(u, wt_scr[...]) + _dot(h_in.astype(BF16), mout_scr[...])
    y_ref[0] = y.reshape(bg, rc, row_w)


def _s5_core(uc, p):
    n_slab, B, n_chunks, row_w = uc.shape
    bg = _bgroup(B)
    rc = min(S5_ROWS // bg, n_chunks)
    n_state = 2 * S5_SLAB * S5_STATE
    n_tiles = n_state // LANE
    tables = [p["kcat"], p["fcat"], p["qcat"], p["a_chunk"]]

    def slab_spec(a):
        nd = a.ndim - 1
        return pl.BlockSpec((1,) + a.shape[1:], lambda j, g, r: (j,) + (0,) * nd)

    return pl.pallas_call(
        _s5_kernel,
        grid=(n_slab, B // bg, n_chunks // rc),
        in_specs=[pl.BlockSpec((1, bg, rc, row_w), lambda j, g, r: (j, g, r, 0))]
                 + [slab_spec(a) for a in tables],
        out_specs=pl.BlockSpec((1, bg, rc, row_w), lambda j, g, r: (j, g, r, 0)),
        out_shape=jax.ShapeDtypeStruct((n_slab, B, n_chunks, row_w), F32),
        scratch_shapes=[pltpu.VMEM((row_w, row_w), BF16), pltpu.VMEM((row_w, n_state), BF16),
                        pltpu.VMEM((n_state, row_w), BF16),
                        pltpu.VMEM((n_tiles, bg * rc, LANE), F32),
                        pltpu.VMEM((n_tiles, bg * rc, LANE), F32),
                        pltpu.VMEM((n_tiles, bg, LANE), F32)],
        compiler_params=pltpu.CompilerParams(
            dimension_semantics=("arbitrary", "arbitrary", "arbitrary"),
            vmem_limit_bytes=VMEM_LIMIT),
        name="s5_core",
    )(uc, *tables)


def _gelu_tanh(y):
    c = math.sqrt(2.0 / math.pi)
    return 0.5 * y * (1.0 + jnp.tanh(c * (y + 0.044715 * (y * y * y))))


def _s5_out_ffn_kernel(x_ref, yc_ref, gmix_ref, dskip_ref, wglu_ref, gffn_ref, w1_ref, w2_ref,
                       x2_ref, y_scr, *, tile):
    x = x_ref[0]
    rows = tile // S5_L
    for j in range(N_SLABS):
        for l in range(S5_L):
            y_scr[j, pl.ds(l, rows, stride=S5_L), :] = yc_ref[j, 0, :, l * LANE:(l + 1) * LANE]
    y = jnp.concatenate([y_scr[j] for j in range(N_SLABS)], axis=1)
    u = _rms_rows(x, gmix_ref[...])
    g = _gelu_tanh(y + dskip_ref[...] * u).astype(BF16)
    vg = _dot(g, wglu_ref[...])
    x1 = x + vg[:, :D_MODEL] * jax.nn.sigmoid(vg[:, D_MODEL:])
    x2_ref[0] = _ffn(x1, gffn_ref, w1_ref, w2_ref)


def _s5_out_ffn(x, yc, gmix, dskip, wglu, gffn, w1, w2):
    B, S, D = x.shape
    T = FFN_TILE
    row_w = S5_L * LANE
    consts = [gmix, dskip, wglu, gffn, w1, w2]
    return pl.pallas_call(
        functools.partial(_s5_out_ffn_kernel, tile=T),
        grid=(B, S // T),
        in_specs=[pl.BlockSpec((1, T, D), lambda b, t: (b, t, 0)),
                  pl.BlockSpec((N_SLABS, 1, T // S5_L, row_w), lambda b, t: (0, b, t, 0))]
                 + [_weight_spec(c) for c in consts],
        out_specs=pl.BlockSpec((1, T, D), lambda b, t: (b, t, 0)),
        out_shape=jax.ShapeDtypeStruct((B, S, D), F32),
        scratch_shapes=[pltpu.VMEM((N_SLABS, T, LANE), F32)],
        compiler_params=pltpu.CompilerParams(
            dimension_semantics=("arbitrary", "arbitrary"), vmem_limit_bytes=VMEM_LIMIT),
        name="s5_out_ffn",
    )(x, yc, *[_operand(c) for c in consts])


def _rot_cols(w):
    half = w.shape[1] // 2
    return jnp.concatenate([-w[:, half:], w[:, :half]], axis=1)


def _slot_cols(w, n_heads, width, offset=0):
    k = w.shape[0]
    w = w.reshape(k, n_heads, width)
    w = jnp.pad(w, ((0, 0), (0, 0), (offset, HEAD_SLOT - width - offset)))
    return w.reshape(k, n_heads * HEAD_SLOT)


def _attn_params(w_in, g_cq, w_uq, g_ckv, w_ukv, g_qn_mla, g_kn_mla, g_qn_moba, g_kn_moba):
    sizes = [MLA_Q_RANK, MLA_KV_RANK, MLA_ROPE, MOBA_W, MOBA_W, MOBA_W]
    offs = [0]
    for s in sizes:
        offs.append(offs[-1] + s)
    w_cq, w_ckv, w_kr, w_qb, w_kb, w_vb = [w_in[:, offs[i]:offs[i + 1]] for i in range(6)]
    w_std = jnp.concatenate([w_cq, w_ckv,
                             _slot_cols(w_kr, 1, MLA_ROPE, MLA_NOPE),
                             _slot_cols(_rot_cols(w_kr), 1, MLA_ROPE, MLA_NOPE),
                             _slot_cols(w_kb, MOBA_HEADS, MOBA_HEAD_DIM)], axis=1)
    w_t = jnp.concatenate([w_qb, w_vb], axis=1).T
    wq = w_uq.reshape(MLA_Q_RANK, MLA_HEADS, MLA_QK)
    wq_nope, wq_rope = wq[:, :, :MLA_NOPE], wq[:, :, MLA_NOPE:]
    wq_rot = jnp.concatenate([-wq_rope[:, :, MLA_ROPE // 2:], wq_rope[:, :, :MLA_ROPE // 2]], axis=2)
    w_uq_t = jnp.concatenate([wq_nope, wq_rope, wq_rot], axis=2).reshape(
        MLA_Q_RANK, MLA_HEADS * HEAD_SLOT).T
    wkv = w_ukv.reshape(MLA_KV_RANK, MLA_HEADS, MLA_NOPE + MLA_V)
    w_ukv_std = _slot_cols(wkv[:, :, :MLA_NOPE].reshape(MLA_KV_RANK, MLA_HEADS * MLA_NOPE),
                           MLA_HEADS, MLA_NOPE)
    w_ukv_vt = wkv[:, :, MLA_NOPE:].reshape(MLA_KV_RANK, MLA_HEADS * MLA_V).T
    return {
        "w_std": w_std.astype(BF16), "w_t": w_t.astype(BF16),
        "g_cq": g_cq.reshape(1, -1), "g_ckv": g_ckv.reshape(1, -1),
        "w_uq_t": w_uq_t.astype(BF16), "w_ukv_std": w_ukv_std.astype(BF16),
        "w_ukv_vt": w_ukv_vt.astype(BF16),
        "gq_mla": (g_qn_mla * (MLA_QK ** -0.5 * LOG2E)).reshape(-1, 1),
        "gk_mla": jnp.pad(g_kn_mla, (0, HEAD_SLOT - MLA_QK)).reshape(1, -1),
        "gq_moba": (g_qn_moba * (MOBA_HEAD_DIM ** -0.5 * LOG2E)).reshape(-1, 1),
        "gk_moba": jnp.pad(g_kn_moba, (0, HEAD_SLOT - MOBA_HEAD_DIM)).reshape(1, -1),
    }


def _rope_tables(S):
    half = MLA_ROPE // 2
    inv = ROPE_THETA ** (-jnp.arange(half, dtype=F32) / half)
    ang = jnp.arange(S).astype(F32)[:, None] * inv[None, :]
    cos = jnp.tile(jnp.cos(ang), (1, 2))
    sin = jnp.tile(jnp.sin(ang), (1, 2))
    pad = ((0, 0), (MLA_NOPE, HEAD_SLOT - MLA_QK))
    return {"cos_t": cos.T, "sin_t": sin.T, "cos_r": jnp.pad(cos, pad), "sin_r": jnp.pad(sin, pad)}


def _s5_params(lam_re, lam_im, log_dt, b_re, b_im, c_re, c_im):
    hi = lax.Precision.HIGHEST
    G, P, L = S5_GROUPS, S5_STATE, S5_L
    dt = jnp.exp(log_dt)[:, None]

    steps = jnp.arange(L + 1, dtype=F32)[:, None, None]
    mag = jnp.exp(lam_re * dt * steps)
    pw_re = mag * jnp.cos(lam_im * dt * steps)
    pw_im = mag * jnp.sin(lam_im * dt * steps)
    a_re, a_im = pw_re[1], pw_im[1]
    den = lam_re * lam_re + lam_im * lam_im
    k_re = ((a_re - 1.0) * lam_re + a_im * lam_im) / den
    k_im = (a_im * lam_re - (a_re - 1.0) * lam_im) / den
    bb_re = k_re[..., None] * b_re - k_im[..., None] * b_im
    bb_im = k_re[..., None] * b_im + k_im[..., None] * b_re

    e_re = pw_re[:L, ..., None] * bb_re - pw_im[:L, ..., None] * bb_im
    e_im = pw_re[:L, ..., None] * bb_im + pw_im[:L, ..., None] * bb_re
    lag = (jnp.einsum("gop,sgpi->sgoi", c_re, e_re, precision=hi)
           - jnp.einsum("gop,sgpi->sgoi", c_im, e_im, precision=hi))
    I = S5_GROUP
    kcat = lag.reshape(L, N_SLABS, S5_SLAB, I, I)
    kcat = kcat.transpose(1, 0, 4, 2, 3).reshape(N_SLABS, L, I, S5_SLAB * I)

    f = jnp.stack([e_re[::-1], e_im[::-1]], axis=0)
    f = f.reshape(2, L, N_SLABS, S5_SLAB, P, I)
    fcat = f.transpose(2, 0, 1, 5, 3, 4).reshape(N_SLABS, 2, L, I, S5_SLAB * P)

    nr, ni = pw_re[1:, :, None, :], pw_im[1:, :, None, :]
    q = jnp.stack([c_re * nr - c_im * ni, -(c_re * ni + c_im * nr)], axis=0)
    q = q.reshape(2, L, N_SLABS, S5_SLAB, I, P)
    qcat = q.transpose(2, 0, 1, 5, 3, 4).reshape(N_SLABS, 2, L, P, S5_SLAB * I)

    al_re, al_im = pw_re[L], pw_im[L]
    a_chunk = jnp.concatenate([al_re.reshape(N_SLABS, 1, S5_SLAB * P),
                               al_im.reshape(N_SLABS, 1, S5_SLAB * P)], axis=2)
    return {"kcat": kcat, "fcat": fcat, "qcat": qcat, "a_chunk": a_chunk}


def _cast_kernel(w_ref, o_ref):
    o_ref[...] = w_ref[...].astype(BF16)


def _to_bf16(w):
    n, r, c = w.shape
    rb = min(r, CAST_BLOCK_BYTES // (4 * c))
    spec = pl.BlockSpec((1, rb, c), lambda i, j: (i, j, 0))
    return pl.pallas_call(
        _cast_kernel, grid=(n, r // rb), in_specs=[spec], out_specs=spec,
        out_shape=jax.ShapeDtypeStruct(w.shape, BF16), name="to_bf16")(w)


def kernel(x, mix_norm_g, ffn_norm_g, w_in, g_cq, w_uq, g_ckv, w_ukv, g_qn_mla, g_kn_mla,
           g_qn_moba, g_kn_moba, w_o, lam_re, lam_im, log_dt, b_re, b_im, c_re, c_im,
           d_skip, w_glu, w_ff1, w_ff2):
    B, S, D = x.shape
    depth = mix_norm_g.shape[0]
    assert D == D_MODEL and S % ATT_TILE == 0 and S % FFN_TILE == 0 and depth % 2 == 0
    assert S % min(PROJ_TILE, S) == 0 and PROJ_TILE % ATT_TILE == 0
    tabs = _rope_tables(S)
    row = lambda v: v.reshape(1, -1)
    w_ff1, w_ff2, w_glu, w_o = _to_bf16(w_ff1), _to_bf16(w_ff2), _to_bf16(w_glu), _to_bf16(w_o)
    for layer in range(0, depth, 2):
        i = layer // 2
        ap = _attn_params(w_in[i], g_cq[i], w_uq[i], g_ckv[i], w_ukv[i], g_qn_mla[i], g_kn_mla[i],
                          g_qn_moba[i], g_kn_moba[i])
        qt, k, vt = _attn_proj(x, row(mix_norm_g[layer]), ap, tabs)
        o = _flash(qt, k, vt)
        x, uc = _attn_out_ffn(x, o, _Layer(w_o, i), row(ffn_norm_g[layer]), _Layer(w_ff1, layer),
                              _Layer(w_ff2, layer), row(mix_norm_g[layer + 1]))
        sp = _s5_params(lam_re[i], lam_im[i], log_dt[i], b_re[i], b_im[i], c_re[i], c_im[i])
        yc = _s5_core(uc, sp)
        x = _s5_out_ffn(x, yc, row(mix_norm_g[layer + 1]), row(d_skip[i]), _Layer(w_glu, i),
                        row(ffn_norm_g[layer + 1]), _Layer(w_ff1, layer + 1),
                        _Layer(w_ff2, layer + 1))
    return x
```

```python
import functools
import math
from typing import NamedTuple

import jax
import jax.numpy as jnp
from jax import lax
from jax.experimental import pallas as pl
from jax.experimental.pallas import tpu as pltpu

F32 = jnp.float32
BF16 = jnp.bfloat16

D_MODEL = 1024
D_FF = 4 * D_MODEL
EPS = 1e-6
MLA_HEADS = 8
MLA_NOPE = 64
MLA_ROPE = 32
MLA_V = 64
MLA_QK = MLA_NOPE + MLA_ROPE
MLA_Q_RANK = 256
MLA_KV_RANK = 128
ROPE_THETA = 10000.0
MOBA_HEADS = 8
MOBA_HEAD_DIM = 64
MOBA_W = MOBA_HEADS * MOBA_HEAD_DIM
MOBA_BLOCK = 256
MOBA_TOPK = 3
S5_GROUP = 16
S5_GROUPS = D_MODEL // S5_GROUP
S5_STATE = 64

LANE = 128
HEAD_SLOT = 128
N_HEADS = MLA_HEADS + MOBA_HEADS
MASK_VALUE = -1e30
SUM_ROWS = 16
LOG2E = math.log2(math.e)
ATT_TILE = 512
PROJ_TILE = 1024
SCORE_ROWS = 256
FLASH_HEADS = 4
FFN_TILE = 512
FF_CHUNK = 1024
S5_L = 8
S5_SLAB = LANE // S5_GROUP
N_SLABS = D_MODEL // LANE
S5_ROWS = 1024
VMEM_LIMIT = 56 * 1024 * 1024
CAST_BLOCK_BYTES = 4 * 1024 * 1024

_NT = (((1,), (1,)), ((), ()))


def _dot(a, b, precision=None):
    return jnp.dot(a, b, preferred_element_type=F32, precision=precision)


def _dot_nt(a, b):
    return lax.dot_general(a, b, _NT, preferred_element_type=F32)


def _rms_rows(x, gain_row):
    ms = jnp.mean(x * x, axis=-1, keepdims=True)
    return x * lax.rsqrt(ms + EPS) * gain_row


def _const_spec(shape):
    nd = len(shape)
    return pl.BlockSpec(shape, lambda *_: (0,) * nd, pipeline_mode=pl.Buffered(1))


class _Layer(NamedTuple):
    stack: jax.Array
    index: int


def _weight_spec(w):
    if isinstance(w, _Layer):
        nd = w.stack.ndim - 1
        return pl.BlockSpec((None,) + w.stack.shape[1:], lambda *_: (w.index,) + (0,) * nd,
                            pipeline_mode=pl.Buffered(1))
    return _const_spec(w.shape)


def _operand(w):
    return w.stack if isinstance(w, _Layer) else w


def _attn_proj_kernel(x_ref, gmix_ref, wstd_ref, wt_ref, gcq_ref, gckv_ref, wuqt_ref, wukv_ref,
                      wukvvt_ref, cost_ref, sint_ref, cosr_ref, sinr_ref, gqmla_ref, gkmla_ref,
                      gqmoba_ref, gkmoba_ref, qt_ref, k_ref, vt_ref, km_ref, *, tile):
    ti = pl.program_id(1)
    nsub = tile // MOBA_BLOCK
    sub = MOBA_BLOCK
    per_q = qt_ref.shape[4] // sub

    @pl.when(ti == 0)
    def _():
        km_ref[...] = jnp.zeros_like(km_ref)

    gq_mla = gqmla_ref[...]
    gk_mla = gkmla_ref[...]
    gq_moba = gqmoba_ref[...]
    gk_moba = gkmoba_ref[...]
    nb = km_ref.shape[1]
    pad_q = jnp.zeros((HEAD_SLOT - MLA_QK, sub), F32)
    pad_m = jnp.zeros((HEAD_SLOT - MOBA_HEAD_DIM - nb, sub), F32)
    jidx = lax.broadcasted_iota(jnp.int32, (nb, sub), 0)
    lower = [jnp.where(jp < jidx, 1.0, 0.0) for jp in range(nb)]
    lane = lax.broadcasted_iota(jnp.int32, (sub, HEAD_SLOT), 1)

    def project(sb):
        rows = slice(sb * sub, (sb + 1) * sub)
        hb = _rms_rows(x_ref[0, rows, :], gmix_ref[...]).astype(BF16)
        pstd = _dot(hb, wstd_ref[...])
        pt = _dot_nt(wt_ref[...], hb)
        return pstd, pt

    def finish(sb, pstd, pt):
        rows = slice(sb * sub, (sb + 1) * sub)
        cols = slice((sb % per_q) * sub, (sb % per_q + 1) * sub)
        qtile = sb // per_q
        cur = ti * nsub + sb
        o = 0
        cq = pstd[:, o:o + MLA_Q_RANK]; o += MLA_Q_RANK
        ckv = pstd[:, o:o + MLA_KV_RANK]; o += MLA_KV_RANK
        kr_a = pstd[:, o:o + HEAD_SLOT]; o += HEAD_SLOT
        kr_b = pstd[:, o:o + HEAD_SLOT]; o += HEAD_SLOT
        kb_off = o

        cqn = _rms_rows(cq, gcq_ref[...]).astype(BF16)
        ckvn = _rms_rows(ckv, gckv_ref[...]).astype(BF16)
        qup_t = _dot_nt(wuqt_ref[...], cqn)
        kv_std = _dot(ckvn, wukv_ref[...])
        vt_ref[0, qtile, 0:MLA_HEADS * MLA_V, cols] = _dot_nt(wukvvt_ref[...], ckvn).astype(BF16)
        vt_ref[0, qtile, MLA_HEADS * MLA_V:, cols] = pt[MOBA_W:, :].astype(BF16)
        k_rope = kr_a * cosr_ref[rows, :] + kr_b * sinr_ref[rows, :]
        cos_t = cost_ref[:, rows]
        sin_t = sint_ref[:, rows]
        for h in range(MLA_HEADS):
            blk = qup_t[h * HEAD_SLOT:(h + 1) * HEAD_SLOT]
            nope = blk[0:MLA_NOPE]
            rope = (blk[MLA_NOPE:MLA_QK] * cos_t + blk[MLA_QK:MLA_QK + MLA_ROPE] * sin_t)
            ssq = (jnp.sum(nope * nope, axis=0, keepdims=True)
                   + jnp.sum(rope * rope, axis=0, keepdims=True))
            r = lax.rsqrt(ssq * (1.0 / MLA_QK) + EPS)
            qn = jnp.concatenate([nope * r * gq_mla[0:MLA_NOPE], rope * r * gq_mla[MLA_NOPE:MLA_QK],
                                  pad_q], axis=0)
            qt_ref[0, h, qtile, :, cols] = qn.astype(BF16)
            kh = kv_std[:, h * HEAD_SLOT:(h + 1) * HEAD_SLOT] + k_rope
            ssk = jnp.sum(kh * kh, axis=-1, keepdims=True)
            k_ref[0, h, rows, :] = (kh * lax.rsqrt(ssk * (1.0 / MLA_QK) + EPS) * gk_mla).astype(BF16)

        onehot = jnp.where(lane == MOBA_HEAD_DIM + cur, 1.0, 0.0)
        for h in range(MOBA_HEADS):
            kh = pstd[:, kb_off + h * HEAD_SLOT:kb_off + (h + 1) * HEAD_SLOT]
            ssk = jnp.sum(kh * kh, axis=-1, keepdims=True)
            kn = kh * lax.rsqrt(ssk * (1.0 / MOBA_HEAD_DIM) + EPS) * gk_moba
            km_ref[h, pl.ds(cur, 1), :] = jnp.mean(kn, axis=0, keepdims=True)
            k_ref[0, MLA_HEADS + h, rows, :] = (kn + onehot).astype(BF16)

            qh = pt[h * MOBA_HEAD_DIM:(h + 1) * MOBA_HEAD_DIM]
            ssq = jnp.sum(qh * qh, axis=0, keepdims=True)
            qn = qh * lax.rsqrt(ssq * (1.0 / MOBA_HEAD_DIM) + EPS) * gq_moba
            gate = _dot(km_ref[h][:, 0:MOBA_HEAD_DIM], qn, precision=lax.Precision.HIGHEST)
            cnt = jnp.zeros((nb, sub), F32)
            for jp in range(nb):
                gj = gate[jp:jp + 1, :]
                tie = jnp.where(gj == gate, lower[jp], 0.0)
                cnt = cnt + jnp.where(jp < cur, jnp.where(gj > gate, 1.0, tie), 0.0)
            bias = jnp.where(jidx < cur, jnp.where(cnt < float(MOBA_TOPK), 0.0, MASK_VALUE),
                             jnp.where(jidx == cur, 0.0, MASK_VALUE))
            qt_ref[0, MLA_HEADS + h, qtile, :, cols] = (
                jnp.concatenate([qn, bias, pad_m], axis=0).astype(BF16))

    nxt = project(0)
    for sb in range(nsub):
        cur_vals = nxt
        if sb + 1 < nsub:
            nxt = project(sb + 1)
        finish(sb, *cur_vals)


def _attn_proj(x, gmix, p, tabs):
    B, S, D = x.shape
    T = min(PROJ_TILE, S)
    tf = ATT_TILE
    per = T // tf
    nt = S // T
    nb = S // MOBA_BLOCK
    consts = [gmix, p["w_std"], p["w_t"], p["g_cq"], p["g_ckv"], p["w_uq_t"], p["w_ukv_std"],
              p["w_ukv_vt"]]
    tail = [p["gq_mla"], p["gk_mla"], p["gq_moba"], p["gk_moba"]]
    in_specs = ([pl.BlockSpec((1, T, D), lambda b, t: (b, t, 0))]
                + [_const_spec(c.shape) for c in consts]
                + [pl.BlockSpec((MLA_ROPE, T), lambda b, t: (0, t)),
                   pl.BlockSpec((MLA_ROPE, T), lambda b, t: (0, t)),
                   pl.BlockSpec((T, HEAD_SLOT), lambda b, t: (t, 0)),
                   pl.BlockSpec((T, HEAD_SLOT), lambda b, t: (t, 0))]
                + [_const_spec(c.shape) for c in tail])
    out_shape = (jax.ShapeDtypeStruct((B, N_HEADS, S // tf, HEAD_SLOT, tf), BF16),
                 jax.ShapeDtypeStruct((B, N_HEADS, S, HEAD_SLOT), BF16),
                 jax.ShapeDtypeStruct((B, S // tf, N_HEADS * MLA_V, tf), BF16))
    out_specs = (pl.BlockSpec((1, N_HEADS, per, HEAD_SLOT, tf), lambda b, t: (b, 0, t, 0, 0)),
                 pl.BlockSpec((1, N_HEADS, T, HEAD_SLOT), lambda b, t: (b, 0, t, 0)),
                 pl.BlockSpec((1, per, N_HEADS * MLA_V, tf), lambda b, t: (b, t, 0, 0)))
    return pl.pallas_call(
        functools.partial(_attn_proj_kernel, tile=T),
        grid=(B, nt),
        in_specs=in_specs,
        out_specs=out_specs,
        out_shape=out_shape,
        scratch_shapes=[pltpu.VMEM((MOBA_HEADS, nb, HEAD_SLOT), F32)],
        compiler_params=pltpu.CompilerParams(
            dimension_semantics=("arbitrary", "arbitrary"), vmem_limit_bytes=VMEM_LIMIT),
        name="attn_proj",
    )(x, *consts, tabs["cos_t"], tabs["sin_t"], tabs["cos_r"], tabs["sin_r"], *tail)


def _flash_kernel(qt_ref, k_ref, vt_ref, o_ref, s_scr, p_scr, acc_scr):
    nh, nt, _, tile = qt_ref.shape[1:]
    nchunk = tile // SCORE_ROWS

    def score_chunk(hh, qidx, j, r, cmax):
        k0 = pl.multiple_of(j * tile, tile) + r * SCORE_ROWS
        s = _dot(k_ref[0, hh, pl.ds(k0, SCORE_ROWS), :], qt_ref[0, hh, qidx])
        s_scr[hh, r * SCORE_ROWS:(r + 1) * SCORE_ROWS, :] = s
        c = jnp.max(s, axis=0, keepdims=True)
        return c if cmax is None else jnp.maximum(cmax, c)

    def step(qi, j, ms, cmaxes, diagonal):
        ones = jnp.ones((SUM_ROWS, tile), BF16)
        new_ms, new_cmaxes = [], []
        for hh in range(nh):
            def chunk(r):
                s = s_scr[hh, r * SCORE_ROWS:(r + 1) * SCORE_ROWS, :]
                if diagonal:
                    key = lax.broadcasted_iota(jnp.int32, s.shape, 0) + r * SCORE_ROWS
                    qry = lax.broadcasted_iota(jnp.int32, s.shape, 1)
                    s = jnp.where(key <= qry, s, MASK_VALUE)
                return s
            cmax = cmaxes[hh]
            if diagonal:
                cmax = functools.reduce(jnp.maximum, [jnp.max(chunk(r), axis=0, keepdims=True)
                                                      for r in range(nchunk)])
            m_new = jnp.maximum(ms[hh], cmax)
            nxt = None
            for r in range(nchunk):
                p_scr[hh, r * SCORE_ROWS:(r + 1) * SCORE_ROWS, :] = (
                    jnp.exp2(chunk(r) - m_new).astype(BF16))
                if diagonal:
                    nxt = score_chunk(hh, jnp.minimum(qi + 1, nt - 1), 0, r, nxt)
                else:
                    nxt = score_chunk(hh, qi, j + 1, r, nxt)
            v_ext = jnp.concatenate([vt_ref[0, j, hh * MLA_V:(hh + 1) * MLA_V, :], ones], axis=0)
            acc_scr[hh] = jnp.exp2(ms[hh] - m_new) * acc_scr[hh] + _dot(v_ext, p_scr[hh])
            new_ms.append(m_new)
            new_cmaxes.append(nxt)
        return new_ms, new_cmaxes

    cmax0 = []
    for hh in range(nh):
        c = None
        for r in range(nchunk):
            c = score_chunk(hh, 0, 0, r, c)
        cmax0.append(c)
    m0 = jnp.full((1, tile), MASK_VALUE, F32)

    def query_tile(qi, cmaxes):
        acc_scr[...] = jnp.zeros_like(acc_scr)

        def body(t, carry):
            ms, cms = step(qi, t, carry[0:nh], carry[nh:], False)
            return tuple(ms) + tuple(cms)

        carry = lax.fori_loop(0, qi, body, (m0,) * nh + tuple(cmaxes))
        _, cms = step(qi, qi, carry[0:nh], carry[nh:], True)
        outs = []
        for hh in range(nh):
            acc = acc_scr[hh]
            outs.append(acc[0:MLA_V] * (1.0 / acc[MLA_V:MLA_V + 1]))
        o_ref[0, pl.ds(pl.multiple_of(qi * tile, tile), tile), :] = (
            jnp.concatenate(outs, axis=0).T.astype(BF16))
        return tuple(cms)

    lax.fori_loop(0, nt, query_tile, tuple(cmax0))


def _flash(qt, k, vt):
    B, H, nt, _, T = qt.shape
    S = nt * T
    nh = FLASH_HEADS
    return pl.pallas_call(
        _flash_kernel,
        grid=(B, H // nh),
        in_specs=[pl.BlockSpec((1, nh, nt, HEAD_SLOT, T), lambda b, p: (b, p, 0, 0, 0)),
                  pl.BlockSpec((1, nh, S, HEAD_SLOT), lambda b, p: (b, p, 0, 0)),
                  pl.BlockSpec((1, nt, nh * MLA_V, T), lambda b, p: (b, 0, p, 0))],
        out_specs=pl.BlockSpec((1, S, nh * MLA_V), lambda b, p: (b, 0, p)),
        out_shape=jax.ShapeDtypeStruct((B, S, H * MLA_V), BF16),
        scratch_shapes=[pltpu.VMEM((nh, T, T), F32), pltpu.VMEM((nh, T, T), BF16),
                        pltpu.VMEM((nh, MLA_V + SUM_ROWS, T), F32)],
        compiler_params=pltpu.CompilerParams(
            dimension_semantics=("arbitrary", "arbitrary"), vmem_limit_bytes=VMEM_LIMIT),
        name="flash",
    )(qt, k, vt)


def _ffn(x1, gffn_ref, w1_ref, w2_ref):
    hb = _rms_rows(x1, gffn_ref[...]).astype(BF16)
    acc = x1
    for c in range(D_FF // FF_CHUNK):
        a = _dot(hb, w1_ref[:, c * FF_CHUNK:(c + 1) * FF_CHUNK])
        a = jnp.square(jnp.maximum(a, 0.0)).astype(BF16)
        acc = acc + _dot(a, w2_ref[c * FF_CHUNK:(c + 1) * FF_CHUNK, :])
    return acc


def _emit_chunk_rows(x2, gnext_ref, u_scr, uc_ref, tile):
    u = _rms_rows(x2, gnext_ref[...])
    rows = tile // S5_L
    for j in range(N_SLABS):
        u_scr[j] = u[:, j * LANE:(j + 1) * LANE]
        for l in range(S5_L):
            uc_ref[j, 0, :, l * LANE:(l + 1) * LANE] = (
                u_scr[j, pl.ds(l, rows, stride=S5_L), :].astype(BF16))


def _attn_out_ffn_kernel(x_ref, o_ref, wo_ref, gffn_ref, w1_ref, w2_ref, gnext_ref,
                         x2_ref, uc_ref, u_scr, *, tile):
    x1 = x_ref[0] + _dot(o_ref[0], wo_ref[...])
    x2 = _ffn(x1, gffn_ref, w1_ref, w2_ref)
    x2_ref[0] = x2
    _emit_chunk_rows(x2, gnext_ref, u_scr, uc_ref, tile)


def _bgroup(B):
    return 8 if B % 8 == 0 else B


def _attn_out_ffn(x, o, wo, gffn, w1, w2, gnext):
    B, S, D = x.shape
    T = FFN_TILE
    row_w = S5_L * LANE
    consts_a = [wo, gffn, w1, w2, gnext]
    return pl.pallas_call(
        functools.partial(_attn_out_ffn_kernel, tile=T),
        grid=(B, S // T),
        in_specs=[pl.BlockSpec((1, T, D), lambda b, t: (b, t, 0)),
                  pl.BlockSpec((1, T, D), lambda b, t: (b, t, 0))]
                 + [_weight_spec(c) for c in consts_a],
        out_specs=(pl.BlockSpec((1, T, D), lambda b, t: (b, t, 0)),
                   pl.BlockSpec((N_SLABS, 1, T // S5_L, row_w), lambda b, t: (0, b, t, 0))),
        out_shape=(jax.ShapeDtypeStruct((B, S, D), F32),
                   jax.ShapeDtypeStruct((N_SLABS, B, S // S5_L, row_w), BF16)),
        scratch_shapes=[pltpu.VMEM((N_SLABS, T, LANE), F32)],
        compiler_params=pltpu.CompilerParams(
            dimension_semantics=("arbitrary", "arbitrary"), vmem_limit_bytes=VMEM_LIMIT),
        name="attn_out_ffn",
    )(x, o, *[_operand(c) for c in consts_a])


def _s5_expand(kcat_ref, fcat_ref, qcat_ref, wt_scr, min_scr, mout_scr):
    L, I, P = S5_L, S5_GROUP, S5_STATE
    half = S5_SLAB * P

    def same_group(shape, rows_per_group, cols_per_group):
        r = lax.broadcasted_iota(jnp.int32, shape, 0) // rows_per_group
        c = lax.broadcasted_iota(jnp.int32, shape, 1) // cols_per_group
        return r == c

    def tiled(block, mask):
        return jnp.where(mask, jnp.concatenate([block] * S5_SLAB, axis=0), 0.0).astype(BF16)

    m_kk = same_group((LANE, LANE), I, I)
    m_in = same_group((LANE, half), I, P)
    m_out = same_group((half, LANE), P, I)
    zeros = jnp.zeros((LANE, LANE), BF16)
    lag_blocks = [tiled(kcat_ref[0, s], m_kk) for s in range(L)]
    for li in range(L):
        for lo in range(L):
            wt_scr[li * LANE:(li + 1) * LANE, lo * LANE:(lo + 1) * LANE] = (
                lag_blocks[lo - li] if lo >= li else zeros)
    for c in range(2):
        for l in range(L):
            min_scr[l * LANE:(l + 1) * LANE, c * half:(c + 1) * half] = tiled(fcat_ref[0, c, l], m_in)
            mout_scr[c * half:(c + 1) * half, l * LANE:(l + 1) * LANE] = tiled(qcat_ref[0, c, l], m_out)


def _s5_kernel(u_ref, kcat_ref, fcat_ref, qcat_ref, a_ref, y_ref,
               wt_scr, min_scr, mout_scr, v_scr, h_scr, st_scr):
    @pl.when((pl.program_id(1) == 0) & (pl.program_id(2) == 0))
    def _():
        _s5_expand(kcat_ref, fcat_ref, qcat_ref, wt_scr, min_scr, mout_scr)

    @pl.when(pl.program_id(2) == 0)
    def _():
        st_scr[...] = jnp.zeros_like(st_scr)

    bg, rc, row_w = u_ref.shape[1:]
    n_tiles = st_scr.shape[0]
    half = n_tiles // 2
    u = u_ref[0].reshape(bg * rc, row_w)
    v = _dot(u, min_scr[...])
    for t in range(n_tiles):
        for s in range(bg):
            v_scr[t, pl.ds(s, rc, stride=bg), :] = v[s * rc:(s + 1) * rc, t * LANE:(t + 1) * LANE]
    a = a_ref[0]
    a_t = [jnp.broadcast_to(a[:, t * LANE:(t + 1) * LANE], (bg, LANE)) for t in range(n_tiles)]

    def step(c, h):
        rows = slice(c * bg, (c + 1) * bg)
        new = []
        for t in range(n_tiles):
            h_scr[t, rows, :] = h[t]
        for t in range(half):
            new.append(a_t[t] * h[t] - a_t[half + t] * h[half + t] + v_scr[t, rows, :])
        for t in range(half):
            new.append(a_t[t] * h[half + t] + a_t[half + t] * h[t] + v_scr[half + t, rows, :])
        return tuple(new)

    y_lag = _dot(u, wt_scr[...])
    h = tuple(st_scr[t] for t in range(n_tiles))
    for c in range(rc):
        h = step(c, h)
    for t in range(n_tiles):
        st_scr[t] = h[t]
    h_in = jnp.concatenate(
        [jnp.concatenate([h_scr[t, pl.ds(s, rc, stride=bg), :] for t in range(n_tiles)], axis=1)
         for s in range(bg)], axis=0)
    y = y_lag + _dot(h_in.astype(BF16), mout_scr[...])
    y_ref[0] = y.reshape(bg, rc, row_w)


def _s5_core(uc, p):
    n_slab, B, n_chunks, row_w = uc.shape
    bg = _bgroup(B)
    rc = min(S5_ROWS // bg, n_chunks)
    n_state = 2 * S5_SLAB * S5_STATE
    n_tiles = n_state // LANE
    tables = [p["kcat"], p["fcat"], p["qcat"], p["a_chunk"]]

    def slab_spec(a):
        nd = a.ndim - 1
        return pl.BlockSpec((1,) + a.shape[1:], lambda j, g, r: (j,) + (0,) * nd)

    return pl.pallas_call(
        _s5_kernel,
        grid=(n_slab, B // bg, n_chunks // rc),
        in_specs=[pl.BlockSpec((1, bg, rc, row_w), lambda j, g, r: (j, g, r, 0))]
                 + [slab_spec(a) for a in tables],
        out_specs=pl.BlockSpec((1, bg, rc, row_w), lambda j, g, r: (j, g, r, 0)),
        out_shape=jax.ShapeDtypeStruct((n_slab, B, n_chunks, row_w), F32),
        scratch_shapes=[pltpu.VMEM((row_w, row_w), BF16), pltpu.VMEM((row_w, n_state), BF16),
                        pltpu.VMEM((n_state, row_w), BF16),
                        pltpu.VMEM((n_tiles, bg * rc, LANE), F32),
                        pltpu.VMEM((n_tiles, bg * rc, LANE), F32),
                        pltpu.VMEM((n_tiles, bg, LANE), F32)],
        compiler_params=pltpu.CompilerParams(
            dimension_semantics=("arbitrary", "arbitrary", "arbitrary"),
            vmem_limit_bytes=VMEM_LIMIT),
        name="s5_core",
    )(uc, *tables)


def _gelu_tanh(y):
    c = math.sqrt(2.0 / math.pi)
    return 0.5 * y * (1.0 + jnp.tanh(c * (y + 0.044715 * (y * y * y))))


def _s5_out_ffn_kernel(x_ref, yc_ref, gmix_ref, dskip_ref, wglu_ref, gffn_ref, w1_ref, w2_ref,
                       x2_ref, y_scr, *, tile):
    x = x_ref[0]
    rows = tile // S5_L
    for j in range(N_SLABS):
        for l in range(S5_L):
            y_scr[j, pl.ds(l, rows, stride=S5_L), :] = yc_ref[j, 0, :, l * LANE:(l + 1) * LANE]
    y = jnp.concatenate([y_scr[j] for j in range(N_SLABS)], axis=1)
    u = _rms_rows(x, gmix_ref[...])
    g = _gelu_tanh(y + dskip_ref[...] * u).astype(BF16)
    vg = _dot(g, wglu_ref[...])
    x1 = x + vg[:, :D_MODEL] * jax.nn.sigmoid(vg[:, D_MODEL:])
    x2_ref[0] = _ffn(x1, gffn_ref, w1_ref, w2_ref)


def _s5_out_ffn(x, yc, gmix, dskip, wglu, gffn, w1, w2):
    B, S, D = x.shape
    T = FFN_TILE
    row_w = S5_L * LANE
    consts = [gmix, dskip, wglu, gffn, w1, w2]
    return pl.pallas_call(
        functools.partial(_s5_out_ffn_kernel, tile=T),
        grid=(B, S // T),
        in_specs=[pl.BlockSpec((1, T, D), lambda b, t: (b, t, 0)),
                  pl.BlockSpec((N_SLABS, 1, T // S5_L, row_w), lambda b, t: (0, b, t, 0))]
                 + [_weight_spec(c) for c in consts],
        out_specs=pl.BlockSpec((1, T, D), lambda b, t: (b, t, 0)),
        out_shape=jax.ShapeDtypeStruct((B, S, D), F32),
        scratch_shapes=[pltpu.VMEM((N_SLABS, T, LANE), F32)],
        compiler_params=pltpu.CompilerParams(
            dimension_semantics=("arbitrary", "arbitrary"), vmem_limit_bytes=VMEM_LIMIT),
        name="s5_out_ffn",
    )(x, yc, *[_operand(c) for c in consts])


def _rot_cols(w):
    half = w.shape[1] // 2
    return jnp.concatenate([-w[:, half:], w[:, :half]], axis=1)


def _slot_cols(w, n_heads, width, offset=0):
    k = w.shape[0]
    w = w.reshape(k, n_heads, width)
    w = jnp.pad(w, ((0, 0), (0, 0), (offset, HEAD_SLOT - width - offset)))
    return w.reshape(k, n_heads * HEAD_SLOT)


def _attn_params(w_in, g_cq, w_uq, g_ckv, w_ukv, g_qn_mla, g_kn_mla, g_qn_moba, g_kn_moba):
    sizes = [MLA_Q_RANK, MLA_KV_RANK, MLA_ROPE, MOBA_W, MOBA_W, MOBA_W]
    offs = [0]
    for s in sizes:
        offs.append(offs[-1] + s)
    w_cq, w_ckv, w_kr, w_qb, w_kb, w_vb = [w_in[:, offs[i]:offs[i + 1]] for i in range(6)]
    w_std = jnp.concatenate([w_cq, w_ckv,
                             _slot_cols(w_kr, 1, MLA_ROPE, MLA_NOPE),
                             _slot_cols(_rot_cols(w_kr), 1, MLA_ROPE, MLA_NOPE),
                             _slot_cols(w_kb, MOBA_HEADS, MOBA_HEAD_DIM)], axis=1)
    w_t = jnp.concatenate([w_qb, w_vb], axis=1).T
    wq = w_uq.reshape(MLA_Q_RANK, MLA_HEADS, MLA_QK)
    wq_nope, wq_rope = wq[:, :, :MLA_NOPE], wq[:, :, MLA_NOPE:]
    wq_rot = jnp.concatenate([-wq_rope[:, :, MLA_ROPE // 2:], wq_rope[:, :, :MLA_ROPE // 2]], axis=2)
    w_uq_t = jnp.concatenate([wq_nope, wq_rope, wq_rot], axis=2).reshape(
        MLA_Q_RANK, MLA_HEADS * HEAD_SLOT).T
    wkv = w_ukv.reshape(MLA_KV_RANK, MLA_HEADS, MLA_NOPE + MLA_V)
    w_ukv_std = _slot_cols(wkv[:, :, :MLA_NOPE].reshape(MLA_KV_RANK, MLA_HEADS * MLA_NOPE),
                           MLA_HEADS, MLA_NOPE)
    w_ukv_vt = wkv[:, :, MLA_NOPE:].reshape(MLA_KV_RANK, MLA_HEADS * MLA_V).T
    return {
        "w_std": w_std.astype(BF16), "w_t": w_t.astype(BF16),
        "g_cq": g_cq.reshape(1, -1), "g_ckv": g_ckv.reshape(1, -1),
        "w_uq_t": w_uq_t.astype(BF16), "w_ukv_std": w_ukv_std.astype(BF16),
        "w_ukv_vt": w_ukv_vt.astype(BF16),
        "gq_mla": (g_qn_mla * (MLA_QK ** -0.5 * LOG2E)).reshape(-1, 1),
        "gk_mla": jnp.pad(g_kn_mla, (0, HEAD_SLOT - MLA_QK)).reshape(1, -1),
        "gq_moba": (g_qn_moba * (MOBA_HEAD_DIM ** -0.5 * LOG2E)).reshape(-1, 1),
        "gk_moba": jnp.pad(g_kn_moba, (0, HEAD_SLOT - MOBA_HEAD_DIM)).reshape(1, -1),
    }


def _rope_tables(S):
    half = MLA_ROPE // 2
    inv = ROPE_THETA ** (-jnp.arange(half, dtype=F32) / half)
    ang = jnp.arange(S).astype(F32)[:, None] * inv[None, :]
    cos = jnp.tile(jnp.cos(ang), (1, 2))
    sin = jnp.tile(jnp.sin(ang), (1, 2))
    pad = ((0, 0), (MLA_NOPE, HEAD_SLOT - MLA_QK))
    return {"cos_t": cos.T, "sin_t": sin.T, "cos_r": jnp.pad(cos, pad), "sin_r": jnp.pad(sin, pad)}


def _s5_params(lam_re, lam_im, log_dt, b_re, b_im, c_re, c_im):
    hi = lax.Precision.HIGHEST
    G, P, L = S5_GROUPS, S5_STATE, S5_L
    dt = jnp.exp(log_dt)[:, None]

    steps = jnp.arange(L + 1, dtype=F32)[:, None, None]
    mag = jnp.exp(lam_re * dt * steps)
    pw_re = mag * jnp.cos(lam_im * dt * steps)
    pw_im = mag * jnp.sin(lam_im * dt * steps)
    a_re, a_im = pw_re[1], pw_im[1]
    den = lam_re * lam_re + lam_im * lam_im
    k_re = ((a_re - 1.0) * lam_re + a_im * lam_im) / den
    k_im = (a_im * lam_re - (a_re - 1.0) * lam_im) / den
    bb_re = k_re[..., None] * b_re - k_im[..., None] * b_im
    bb_im = k_re[..., None] * b_im + k_im[..., None] * b_re

    e_re = pw_re[:L, ..., None] * bb_re - pw_im[:L, ..., None] * bb_im
    e_im = pw_re[:L, ..., None] * bb_im + pw_im[:L, ..., None] * bb_re
    lag = (jnp.einsum("gop,sgpi->sgoi", c_re, e_re, precision=hi)
           - jnp.einsum("gop,sgpi->sgoi", c_im, e_im, precision=hi))
    I = S5_GROUP
    kcat = lag.reshape(L, N_SLABS, S5_SLAB, I, I)
    kcat = kcat.transpose(1, 0, 4, 2, 3).reshape(N_SLABS, L, I, S5_SLAB * I)

    f = jnp.stack([e_re[::-1], e_im[::-1]], axis=0)
    f = f.reshape(2, L, N_SLABS, S5_SLAB, P, I)
    fcat = f.transpose(2, 0, 1, 5, 3, 4).reshape(N_SLABS, 2, L, I, S5_SLAB * P)

    nr, ni = pw_re[1:, :, None, :], pw_im[1:, :, None, :]
    q = jnp.stack([c_re * nr - c_im * ni, -(c_re * ni + c_im * nr)], axis=0)
    q = q.reshape(2, L, N_SLABS, S5_SLAB, I, P)
    qcat = q.transpose(2, 0, 1, 5, 3, 4).reshape(N_SLABS, 2, L, P, S5_SLAB * I)

    al_re, al_im = pw_re[L], pw_im[L]
    a_chunk = jnp.concatenate([al_re.reshape(N_SLABS, 1, S5_SLAB * P),
                               al_im.reshape(N_SLABS, 1, S5_SLAB * P)], axis=2)
    return {"kcat": kcat, "fcat": fcat, "qcat": qcat, "a_chunk": a_chunk}


def _cast_kernel(w_ref, o_ref):
    o_ref[...] = w_ref[...].astype(BF16)


def _to_bf16(w):
    n, r, c = w.shape
    rb = min(r, CAST_BLOCK_BYTES // (4 * c))
    spec = pl.BlockSpec((1, rb, c), lambda i, j: (i, j, 0))
    return pl.pallas_call(
        _cast_kernel, grid=(n, r // rb), in_specs=[spec], out_specs=spec,
        out_shape=jax.ShapeDtypeStruct(w.shape, BF16), name="to_bf16")(w)


def kernel(x, mix_norm_g, ffn_norm_g, w_in, g_cq, w_uq, g_ckv, w_ukv, g_qn_mla, g_kn_mla,
           g_qn_moba, g_kn_moba, w_o, lam_re, lam_im, log_dt, b_re, b_im, c_re, c_im,
           d_skip, w_glu, w_ff1, w_ff2):
    B, S, D = x.shape
    depth = mix_norm_g.shape[0]
    assert D == D_MODEL and S % ATT_TILE == 0 and S % FFN_TILE == 0 and depth % 2 == 0
    assert S % min(PROJ_TILE, S) == 0 and PROJ_TILE % ATT_TILE == 0
    tabs = _rope_tables(S)
    row = lambda v: v.reshape(1, -1)
    w_ff1, w_ff2, w_glu, w_o = _to_bf16(w_ff1), _to_bf16(w_ff2), _to_bf16(w_glu), _to_bf16(w_o)
    for layer in range(0, depth, 2):
        i = layer // 2
        ap = _attn_params(w_in[i], g_cq[i], w_uq[i], g_ckv[i], w_ukv[i], g_qn_mla[i], g_kn_mla[i],
                          g_qn_moba[i], g_kn_moba[i])
        qt, k, vt = _attn_proj(x, row(mix_norm_g[layer]), ap, tabs)
        o = _flash(qt, k, vt)
        x, uc = _attn_out_ffn(x, o, _Layer(w_o, i), row(ffn_norm_g[layer]), _Layer(w_ff1, layer),
                              _Layer(w_ff2, layer), row(mix_norm_g[layer + 1]))
        sp = _s5_params(lam_re[i], lam_im[i], log_dt[i], b_re[i], b_im[i], c_re[i], c_im[i])
        yc = _s5_core(uc, sp)
        x = _s5_out_ffn(x, yc, row(mix_norm_g[layer + 1]), row(d_skip[i]), _Layer(w_glu, i),
                        row(ffn_norm_g[layer + 1]), _Layer(w_ff1, layer + 1),
                        _Layer(w_ff2, layer + 1))
    return x
```

```python
import functools
import math
from typing import NamedTuple

import jax
import jax.numpy as jnp
from jax import lax
from jax.experimental import pallas as pl
from jax.experimental.pallas import tpu as pltpu

F32 = jnp.float32
BF16 = jnp.bfloat16

D_MODEL = 1024
D_FF = 4 * D_MODEL
EPS = 1e-6
MLA_HEADS = 8
MLA_NOPE = 64
MLA_ROPE = 32
MLA_V = 64
MLA_QK = MLA_NOPE + MLA_ROPE
MLA_Q_RANK = 256
MLA_KV_RANK = 128
ROPE_THETA = 10000.0
MOBA_HEADS = 8
MOBA_HEAD_DIM = 64
MOBA_W = MOBA_HEADS * MOBA_HEAD_DIM
MOBA_BLOCK = 256
MOBA_TOPK = 3
S5_GROUP = 16
S5_GROUPS = D_MODEL // S5_GROUP
S5_STATE = 64

LANE = 128
HEAD_SLOT = 128
N_HEADS = MLA_HEADS + MOBA_HEADS
MASK_VALUE = -1e30
SUM_ROWS = 16
LOG2E = math.log2(math.e)
ATT_TILE = 512
PROJ_TILE = 1024
SCORE_ROWS = 256
FLASH_HEADS = 4
FFN_TILE = 512
FF_CHUNK = 1024
S5_L = 8
S5_SLAB = LANE // S5_GROUP
N_SLABS = D_MODEL // LANE
S5_ROWS = 1024
VMEM_LIMIT = 56 * 1024 * 1024
CAST_BLOCK_BYTES = 4 * 1024 * 1024

_NT = (((1,), (1,)), ((), ()))


def _dot(a, b, precision=None):
    return jnp.dot(a, b, preferred_element_type=F32, precision=precision)


def _dot_nt(a, b):
    return lax.dot_general(a, b, _NT, preferred_element_type=F32)


def _rms_rows(x, gain_row):
    ms = jnp.mean(x * x, axis=-1, keepdims=True)
    return x * lax.rsqrt(ms + EPS) * gain_row


def _const_spec(shape):
    nd = len(shape)
    return pl.BlockSpec(shape, lambda *_: (0,) * nd, pipeline_mode=pl.Buffered(1))


class _Layer(NamedTuple):
    stack: jax.Array
    index: int


def _weight_spec(w):
    if isinstance(w, _Layer):
        nd = w.stack.ndim - 1
        return pl.BlockSpec((None,) + w.stack.shape[1:], lambda *_: (w.index,) + (0,) * nd,
                            pipeline_mode=pl.Buffered(1))
    return _const_spec(w.shape)


def _operand(w):
    return w.stack if isinstance(w, _Layer) else w


def _attn_proj_kernel(x_ref, gmix_ref, wstd_ref, wt_ref, gcq_ref, gckv_ref, wuqt_ref, wukv_ref,
                      wukvvt_ref, cost_ref, sint_ref, cosr_ref, sinr_ref, gqmla_ref, gkmla_ref,
                      gqmoba_ref, gkmoba_ref, qt_ref, k_ref, vt_ref, km_ref, *, tile):
    ti = pl.program_id(1)
    nsub = tile // MOBA_BLOCK
    sub = MOBA_BLOCK
    per_q = qt_ref.shape[4] // sub

    @pl.when(ti == 0)
    def _():
        km_ref[...] = jnp.zeros_like(km_ref)

    gq_mla = gqmla_ref[...]
    gk_mla = gkmla_ref[...]
    gq_moba = gqmoba_ref[...]
    gk_moba = gkmoba_ref[...]
    nb = km_ref.shape[1]
    pad_q = jnp.zeros((HEAD_SLOT - MLA_QK, sub), F32)
    pad_m = jnp.zeros((HEAD_SLOT - MOBA_HEAD_DIM - nb, sub), F32)
    jidx = lax.broadcasted_iota(jnp.int32, (nb, sub), 0)
    lower = [jnp.where(jp < jidx, 1.0, 0.0) for jp in range(nb)]
    lane = lax.broadcasted_iota(jnp.int32, (sub, HEAD_SLOT), 1)

    def project(sb):
        rows = slice(sb * sub, (sb + 1) * sub)
        hb = _rms_rows(x_ref[0, rows, :], gmix_ref[...]).astype(BF16)
        pstd = _dot(hb, wstd_ref[...])
        pt = _dot_nt(wt_ref[...], hb)
        return pstd, pt

    def finish(sb, pstd, pt):
        rows = slice(sb * sub, (sb + 1) * sub)
        cols = slice((sb % per_q) * sub, (sb % per_q + 1) * sub)
        qtile = sb // per_q
        cur = ti * nsub + sb
        o = 0
        cq = pstd[:, o:o + MLA_Q_RANK]; o += MLA_Q_RANK
        ckv = pstd[:, o:o + MLA_KV_RANK]; o += MLA_KV_RANK
        kr_a = pstd[:, o:o + HEAD_SLOT]; o += HEAD_SLOT
        kr_b = pstd[:, o:o + HEAD_SLOT]; o += HEAD_SLOT
        kb_off = o

        cqn = _rms_rows(cq, gcq_ref[...]).astype(BF16)
        ckvn = _rms_rows(ckv, gckv_ref[...]).astype(BF16)
        qup_t = _dot_nt(wuqt_ref[...], cqn)
        kv_std = _dot(ckvn, wukv_ref[...])
        vt_ref[0, qtile, 0:MLA_HEADS * MLA_V, cols] = _dot_nt(wukvvt_ref[...], ckvn).astype(BF16)
        vt_ref[0, qtile, MLA_HEADS * MLA_V:, cols] = pt[MOBA_W:, :].astype(BF16)
        k_rope = kr_a * cosr_ref[rows, :] + kr_b * sinr_ref[rows, :]
        cos_t = cost_ref[:, rows]
        sin_t = sint_ref[:, rows]
        for h in range(MLA_HEADS):
            blk = qup_t[h * HEAD_SLOT:(h + 1) * HEAD_SLOT]
            nope = blk[0:MLA_NOPE]
            rope = (blk[MLA_NOPE:MLA_QK] * cos_t + blk[MLA_QK:MLA_QK + MLA_ROPE] * sin_t)
            ssq = (jnp.sum(nope * nope, axis=0, keepdims=True)
                   + jnp.sum(rope * rope, axis=0, keepdims=True))
            r = lax.rsqrt(ssq * (1.0 / MLA_QK) + EPS)
            qn = jnp.concatenate([nope * r * gq_mla[0:MLA_NOPE], rope * r * gq_mla[MLA_NOPE:MLA_QK],
                                  pad_q], axis=0)
            qt_ref[0, h, qtile, :, cols] = qn.astype(BF16)
            kh = kv_std[:, h * HEAD_SLOT:(h + 1) * HEAD_SLOT] + k_rope
            ssk = jnp.sum(kh * kh, axis=-1, keepdims=True)
            k_ref[0, h, rows, :] = (kh * lax.rsqrt(ssk * (1.0 / MLA_QK) + EPS) * gk_mla).astype(BF16)

        onehot = jnp.where(lane == MOBA_HEAD_DIM + cur, 1.0, 0.0)
        for h in range(MOBA_HEADS):
            kh = pstd[:, kb_off + h * HEAD_SLOT:kb_off + (h + 1) * HEAD_SLOT]
            ssk = jnp.sum(kh * kh, axis=-1, keepdims=True)
            kn = kh * lax.rsqrt(ssk * (1.0 / MOBA_HEAD_DIM) + EPS) * gk_moba
            km_ref[h, pl.ds(cur, 1), :] = jnp.mean(kn, axis=0, keepdims=True)
            k_ref[0, MLA_HEADS + h, rows, :] = (kn + onehot).astype(BF16)

            qh = pt[h * MOBA_HEAD_DIM:(h + 1) * MOBA_HEAD_DIM]
            ssq = jnp.sum(qh * qh, axis=0, keepdims=True)
            qn = qh * lax.rsqrt(ssq * (1.0 / MOBA_HEAD_DIM) + EPS) * gq_moba
            gate = _dot(km_ref[h][:, 0:MOBA_HEAD_DIM], qn, precision=lax.Precision.HIGHEST)
            cnt = jnp.zeros((nb, sub), F32)
            for jp in range(nb):
                gj = gate[jp:jp + 1, :]
                tie = jnp.where(gj == gate, lower[jp], 0.0)
                cnt = cnt + jnp.where(jp < cur, jnp.where(gj > gate, 1.0, tie), 0.0)
            bias = jnp.where(jidx < cur, jnp.where(cnt < float(MOBA_TOPK), 0.0, MASK_VALUE),
                             jnp.where(jidx == cur, 0.0, MASK_VALUE))
            qt_ref[0, MLA_HEADS + h, qtile, :, cols] = (
                jnp.concatenate([qn, bias, pad_m], axis=0).astype(BF16))

    nxt = project(0)
    for sb in range(nsub):
        cur_vals = nxt
        if sb + 1 < nsub:
            nxt = project(sb + 1)
        finish(sb, *cur_vals)


def _attn_proj(x, gmix, p, tabs):
    B, S, D = x.shape
    T = min(PROJ_TILE, S)
    tf = ATT_TILE
    per = T // tf
    nt = S // T
    nb = S // MOBA_BLOCK
    consts = [gmix, p["w_std"], p["w_t"], p["g_cq"], p["g_ckv"], p["w_uq_t"], p["w_ukv_std"],
              p["w_ukv_vt"]]
    tail = [p["gq_mla"], p["gk_mla"], p["gq_moba"], p["gk_moba"]]
    in_specs = ([pl.BlockSpec((1, T, D), lambda b, t: (b, t, 0))]
                + [_weight_spec(c) for c in consts]
                + [pl.BlockSpec((MLA_ROPE, T), lambda b, t: (0, t)),
                   pl.BlockSpec((MLA_ROPE, T), lambda b, t: (0, t)),
                   pl.BlockSpec((T, HEAD_SLOT), lambda b, t: (t, 0)),
                   pl.BlockSpec((T, HEAD_SLOT), lambda b, t: (t, 0))]
                + [_weight_spec(c) for c in tail])
    out_shape = (jax.ShapeDtypeStruct((B, N_HEADS, S // tf, HEAD_SLOT, tf), BF16),
                 jax.ShapeDtypeStruct((B, N_HEADS, S, HEAD_SLOT), BF16),
                 jax.ShapeDtypeStruct((B, S // tf, N_HEADS * MLA_V, tf), BF16))
    out_specs = (pl.BlockSpec((1, N_HEADS, per, HEAD_SLOT, tf), lambda b, t: (b, 0, t, 0, 0)),
                 pl.BlockSpec((1, N_HEADS, T, HEAD_SLOT), lambda b, t: (b, 0, t, 0)),
                 pl.BlockSpec((1, per, N_HEADS * MLA_V, tf), lambda b, t: (b, t, 0, 0)))
    return pl.pallas_call(
        functools.partial(_attn_proj_kernel, tile=T),
        grid=(B, nt),
        in_specs=in_specs,
        out_specs=out_specs,
        out_shape=out_shape,
        scratch_shapes=[pltpu.VMEM((MOBA_HEADS, nb, HEAD_SLOT), F32)],
        compiler_params=pltpu.CompilerParams(
            dimension_semantics=("arbitrary", "arbitrary"), vmem_limit_bytes=VMEM_LIMIT),
        name="attn_proj",
    )(x, *[_operand(c) for c in consts], tabs["cos_t"], tabs["sin_t"], tabs["cos_r"], tabs["sin_r"],
      *[_operand(c) for c in tail])


def _flash_kernel(qt_ref, k_ref, vt_ref, o_ref, s_scr, p_scr, acc_scr):
    nh, nt, _, tile = qt_ref.shape[1:]
    nchunk = tile // SCORE_ROWS

    def score_chunk(hh, qidx, j, r, cmax):
        k0 = pl.multiple_of(j * tile, tile) + r * SCORE_ROWS
        s = _dot(k_ref[0, hh, pl.ds(k0, SCORE_ROWS), :], qt_ref[0, hh, qidx])
        s_scr[hh, r * SCORE_ROWS:(r + 1) * SCORE_ROWS, :] = s
        c = jnp.max(s, axis=0, keepdims=True)
        return c if cmax is None else jnp.maximum(cmax, c)

    def step(qi, j, ms, cmaxes, diagonal):
        ones = jnp.ones((SUM_ROWS, tile), BF16)
        new_ms, new_cmaxes = [], []
        for hh in range(nh):
            def chunk(r):
                s = s_scr[hh, r * SCORE_ROWS:(r + 1) * SCORE_ROWS, :]
                if diagonal:
                    key = lax.broadcasted_iota(jnp.int32, s.shape, 0) + r * SCORE_ROWS
                    qry = lax.broadcasted_iota(jnp.int32, s.shape, 1)
                    s = jnp.where(key <= qry, s, MASK_VALUE)
                return s
            cmax = cmaxes[hh]
            if diagonal:
                cmax = functools.reduce(jnp.maximum, [jnp.max(chunk(r), axis=0, keepdims=True)
                                                      for r in range(nchunk)])
            m_new = jnp.maximum(ms[hh], cmax)
            nxt = None
            for r in range(nchunk):
                p_scr[hh, r * SCORE_ROWS:(r + 1) * SCORE_ROWS, :] = (
                    jnp.exp2(chunk(r) - m_new).astype(BF16))
                if diagonal:
                    nxt = score_chunk(hh, jnp.minimum(qi + 1, nt - 1), 0, r, nxt)
                else:
                    nxt = score_chunk(hh, qi, j + 1, r, nxt)
            v_ext = jnp.concatenate([vt_ref[0, j, hh * MLA_V:(hh + 1) * MLA_V, :], ones], axis=0)
            acc_scr[hh] = jnp.exp2(ms[hh] - m_new) * acc_scr[hh] + _dot(v_ext, p_scr[hh])
            new_ms.append(m_new)
            new_cmaxes.append(nxt)
        return new_ms, new_cmaxes

    cmax0 = []
    for hh in range(nh):
        c = None
        for r in range(nchunk):
            c = score_chunk(hh, 0, 0, r, c)
        cmax0.append(c)
    m0 = jnp.full((1, tile), MASK_VALUE, F32)

    def query_tile(qi, cmaxes):
        acc_scr[...] = jnp.zeros_like(acc_scr)

        def body(t, carry):
            ms, cms = step(qi, t, carry[0:nh], carry[nh:], False)
            return tuple(ms) + tuple(cms)

        carry = lax.fori_loop(0, qi, body, (m0,) * nh + tuple(cmaxes))
        _, cms = step(qi, qi, carry[0:nh], carry[nh:], True)
        outs = []
        for hh in range(nh):
            acc = acc_scr[hh]
            outs.append(acc[0:MLA_V] * (1.0 / acc[MLA_V:MLA_V + 1]))
        o_ref[0, pl.ds(pl.multiple_of(qi * tile, tile), tile), :] = (
            jnp.concatenate(outs, axis=0).T.astype(BF16))
        return tuple(cms)

    lax.fori_loop(0, nt, query_tile, tuple(cmax0))


def _flash(qt, k, vt):
    B, H, nt, _, T = qt.shape
    S = nt * T
    nh = FLASH_HEADS
    return pl.pallas_call(
        _flash_kernel,
        grid=(B, H // nh),
        in_specs=[pl.BlockSpec((1, nh, nt, HEAD_SLOT, T), lambda b, p: (b, p, 0, 0, 0)),
                  pl.BlockSpec((1, nh, S, HEAD_SLOT), lambda b, p: (b, p, 0, 0)),
                  pl.BlockSpec((1, nt, nh * MLA_V, T), lambda b, p: (b, 0, p, 0))],
        out_specs=pl.BlockSpec((1, S, nh * MLA_V), lambda b, p: (b, 0, p)),
        out_shape=jax.ShapeDtypeStruct((B, S, H * MLA_V), BF16),
        scratch_shapes=[pltpu.VMEM((nh, T, T), F32), pltpu.VMEM((nh, T, T), BF16),
                        pltpu.VMEM((nh, MLA_V + SUM_ROWS, T), F32)],
        compiler_params=pltpu.CompilerParams(
            dimension_semantics=("arbitrary", "arbitrary"), vmem_limit_bytes=VMEM_LIMIT),
        name="flash",
    )(qt, k, vt)


def _ffn(x1, gffn_ref, w1_ref, w2_ref):
    hb = _rms_rows(x1, gffn_ref[...]).astype(BF16)
    acc = x1
    for c in range(D_FF // FF_CHUNK):
        a = _dot(hb, w1_ref[:, c * FF_CHUNK:(c + 1) * FF_CHUNK])
        a = jnp.square(jnp.maximum(a, 0.0)).astype(BF16)
        acc = acc + _dot(a, w2_ref[c * FF_CHUNK:(c + 1) * FF_CHUNK, :])
    return acc


def _emit_chunk_rows(x2, gnext_ref, u_scr, uc_ref, tile):
    u = _rms_rows(x2, gnext_ref[...])
    rows = tile // S5_L
    for j in range(N_SLABS):
        u_scr[j] = u[:, j * LANE:(j + 1) * LANE]
        for l in range(S5_L):
            uc_ref[j, 0, :, l * LANE:(l + 1) * LANE] = (
                u_scr[j, pl.ds(l, rows, stride=S5_L), :].astype(BF16))


def _attn_out_ffn_kernel(x_ref, o_ref, wo_ref, gffn_ref, w1_ref, w2_ref, gnext_ref,
                         x2_ref, uc_ref, u_scr, *, tile):
    x1 = x_ref[0] + _dot(o_ref[0], wo_ref[...])
    x2 = _ffn(x1, gffn_ref, w1_ref, w2_ref)
    x2_ref[0] = x2
    _emit_chunk_rows(x2, gnext_ref, u_scr, uc_ref, tile)


def _bgroup(B):
    return 8 if B % 8 == 0 else B


def _attn_out_ffn(x, o, wo, gffn, w1, w2, gnext):
    B, S, D = x.shape
    T = FFN_TILE
    row_w = S5_L * LANE
    consts_a = [wo, gffn, w1, w2, gnext]
    return pl.pallas_call(
        functools.partial(_attn_out_ffn_kernel, tile=T),
        grid=(B, S // T),
        in_specs=[pl.BlockSpec((1, T, D), lambda b, t: (b, t, 0)),
                  pl.BlockSpec((1, T, D), lambda b, t: (b, t, 0))]
                 + [_weight_spec(c) for c in consts_a],
        out_specs=(pl.BlockSpec((1, T, D), lambda b, t: (b, t, 0)),
                   pl.BlockSpec((N_SLABS, 1, T // S5_L, row_w), lambda b, t: (0, b, t, 0))),
        out_shape=(jax.ShapeDtypeStruct((B, S, D), F32),
                   jax.ShapeDtypeStruct((N_SLABS, B, S // S5_L, row_w), BF16)),
        scratch_shapes=[pltpu.VMEM((N_SLABS, T, LANE), F32)],
        compiler_params=pltpu.CompilerParams(
            dimension_semantics=("arbitrary", "arbitrary"), vmem_limit_bytes=VMEM_LIMIT),
        name="attn_out_ffn",
    )(x, o, *[_operand(c) for c in consts_a])


def _s5_expand(kcat_ref, fcat_ref, qcat_ref, wt_scr, min_scr, mout_scr):
    L, I, P = S5_L, S5_GROUP, S5_STATE
    half = S5_SLAB * P

    def same_group(shape, rows_per_group, cols_per_group):
        r = lax.broadcasted_iota(jnp.int32, shape, 0) // rows_per_group
        c = lax.broadcasted_iota(jnp.int32, shape, 1) // cols_per_group
        return r == c

    def tiled(block, mask):
        return jnp.where(mask, jnp.concatenate([block] * S5_SLAB, axis=0), 0.0).astype(BF16)

    m_kk = same_group((LANE, LANE), I, I)
    m_in = same_group((LANE, half), I, P)
    m_out = same_group((half, LANE), P, I)
    zeros = jnp.zeros((LANE, LANE), BF16)
    lag_blocks = [tiled(kcat_ref[0, s], m_kk) for s in range(L)]
    for li in range(L):
        for lo in range(L):
            wt_scr[li * LANE:(li + 1) * LANE, lo * LANE:(lo + 1) * LANE] = (
                lag_blocks[lo - li] if lo >= li else zeros)
    for c in range(2):
        for l in range(L):
            min_scr[l * LANE:(l + 1) * LANE, c * half:(c + 1) * half] = tiled(fcat_ref[0, c, l], m_in)
            mout_scr[c * half:(c + 1) * half, l * LANE:(l + 1) * LANE] = tiled(qcat_ref[0, c, l], m_out)


def _s5_kernel(u_ref, kcat_ref, fcat_ref, qcat_ref, a_ref, y_ref,
               wt_scr, min_scr, mout_scr, v_scr, h_scr, st_scr):
    @pl.when((pl.program_id(1) == 0) & (pl.program_id(2) == 0))
    def _():
        _s5_expand(kcat_ref, fcat_ref, qcat_ref, wt_scr, min_scr, mout_scr)

    @pl.when(pl.program_id(2) == 0)
    def _():
        st_scr[...] = jnp.zeros_like(st_scr)

    bg, rc, row_w = u_ref.shape[1:]
    n_tiles = st_scr.shape[0]
    half = n_tiles // 2
    u = u_ref[0].reshape(bg * rc, row_w)
    v = _dot(u, min_scr[...])
    for t in range(n_tiles):
        for s in range(bg):
            v_scr[t, pl.ds(s, rc, stride=bg), :] = v[s * rc:(s + 1) * rc, t * LANE:(t + 1) * LANE]
    a = a_ref[0]
    a_t = [jnp.broadcast_to(a[:, t * LANE:(t + 1) * LANE], (bg, LANE)) for t in range(n_tiles)]

    def step(c, h):
        rows = slice(c * bg, (c + 1) * bg)
        new = []
        for t in range(n_tiles):
            h_scr[t, rows, :] = h[t]
        for t in range(half):
            new.append(a_t[t] * h[t] - a_t[half + t] * h[half + t] + v_scr[t, rows, :])
        for t in range(half):
            new.append(a_t[t] * h[half + t] + a_t[half + t] * h[t] + v_scr[half + t, rows, :])
        return tuple(new)

    y_lag = _dot(u, wt_scr[...])
    h = tuple(st_scr[t] for t in range(n_tiles))
    for c in range(rc):
        h = step(c, h)
    for t in range(n_tiles):
        st_scr[t] = h[t]
    h_in = jnp.concatenate(
        [jnp.concatenate([h_scr[t, pl.ds(s, rc, stride=bg), :] for t in range(n_tiles)], axis=1)
         for s in range(bg)], axis=0)
    y = y_lag + _dot(h_in.astype(BF16), mout_scr[...])
    y_ref[0] = y.reshape(bg, rc, row_w)


def _s5_core(uc, p):
    n_slab, B, n_chunks, row_w = uc.shape
    bg = _bgroup(B)
    rc = min(S5_ROWS // bg, n_chunks)
    n_state = 2 * S5_SLAB * S5_STATE
    n_tiles = n_state // LANE
    tables = [p["kcat"], p["fcat"], p["qcat"], p["a_chunk"]]

    def slab_spec(w):
        nd = w.stack.ndim - 2
        return pl.BlockSpec((None, 1) + w.stack.shape[2:],
                            lambda j, g, r: (w.index, j) + (0,) * nd)

    return pl.pallas_call(
        _s5_kernel,
        grid=(n_slab, B // bg, n_chunks // rc),
        in_specs=[pl.BlockSpec((1, bg, rc, row_w), lambda j, g, r: (j, g, r, 0))]
                 + [slab_spec(w) for w in tables],
        out_specs=pl.BlockSpec((1, bg, rc, row_w), lambda j, g, r: (j, g, r, 0)),
        out_shape=jax.ShapeDtypeStruct((n_slab, B, n_chunks, row_w), F32),
        scratch_shapes=[pltpu.VMEM((row_w, row_w), BF16), pltpu.VMEM((row_w, n_state), BF16),
                        pltpu.VMEM((n_state, row_w), BF16),
                        pltpu.VMEM((n_tiles, bg * rc, LANE), F32),
                        pltpu.VMEM((n_tiles, bg * rc, LANE), F32),
                        pltpu.VMEM((n_tiles, bg, LANE), F32)],
        compiler_params=pltpu.CompilerParams(
            dimension_semantics=("arbitrary", "arbitrary", "arbitrary"),
            vmem_limit_bytes=VMEM_LIMIT),
        name="s5_core",
    )(uc, *[w.stack for w in tables])


def _gelu_tanh(y):
    c = math.sqrt(2.0 / math.pi)
    return 0.5 * y * (1.0 + jnp.tanh(c * (y + 0.044715 * (y * y * y))))


def _s5_out_ffn_kernel(x_ref, yc_ref, gmix_ref, dskip_ref, wglu_ref, gffn_ref, w1_ref, w2_ref,
                       x2_ref, y_scr, *, tile):
    x = x_ref[0]
    rows = tile // S5_L
    for j in range(N_SLABS):
        for l in range(S5_L):
            y_scr[j, pl.ds(l, rows, stride=S5_L), :] = yc_ref[j, 0, :, l * LANE:(l + 1) * LANE]
    y = jnp.concatenate([y_scr[j] for j in range(N_SLABS)], axis=1)
    u = _rms_rows(x, gmix_ref[...])
    g = _gelu_tanh(y + dskip_ref[...] * u).astype(BF16)
    vg = _dot(g, wglu_ref[...])
    x1 = x + vg[:, :D_MODEL] * jax.nn.sigmoid(vg[:, D_MODEL:])
    x2_ref[0] = _ffn(x1, gffn_ref, w1_ref, w2_ref)


def _s5_out_ffn(x, yc, gmix, dskip, wglu, gffn, w1, w2):
    B, S, D = x.shape
    T = FFN_TILE
    row_w = S5_L * LANE
    consts = [gmix, dskip, wglu, gffn, w1, w2]
    return pl.pallas_call(
        functools.partial(_s5_out_ffn_kernel, tile=T),
        grid=(B, S // T),
        in_specs=[pl.BlockSpec((1, T, D), lambda b, t: (b, t, 0)),
                  pl.BlockSpec((N_SLABS, 1, T // S5_L, row_w), lambda b, t: (0, b, t, 0))]
                 + [_weight_spec(c) for c in consts],
        out_specs=pl.BlockSpec((1, T, D), lambda b, t: (b, t, 0)),
        out_shape=jax.ShapeDtypeStruct((B, S, D), F32),
        scratch_shapes=[pltpu.VMEM((N_SLABS, T, LANE), F32)],
        compiler_params=pltpu.CompilerParams(
            dimension_semantics=("arbitrary", "arbitrary"), vmem_limit_bytes=VMEM_LIMIT),
        name="s5_out_ffn",
    )(x, yc, *[_operand(c) for c in consts])


def _rot_cols(w):
    half = w.shape[1] // 2
    return jnp.concatenate([-w[:, half:], w[:, :half]], axis=1)


def _slot_cols(w, n_heads, width, offset=0):
    k = w.shape[0]
    w = w.reshape(k, n_heads, width)
    w = jnp.pad(w, ((0, 0), (0, 0), (offset, HEAD_SLOT - width - offset)))
    return w.reshape(k, n_heads * HEAD_SLOT)


def _attn_params(w_in, g_cq, w_uq, g_ckv, w_ukv, g_qn_mla, g_kn_mla, g_qn_moba, g_kn_moba):
    sizes = [MLA_Q_RANK, MLA_KV_RANK, MLA_ROPE, MOBA_W, MOBA_W, MOBA_W]
    offs = [0]
    for s in sizes:
        offs.append(offs[-1] + s)
    w_cq, w_ckv, w_kr, w_qb, w_kb, w_vb = [w_in[:, offs[i]:offs[i + 1]] for i in range(6)]
    w_std = jnp.concatenate([w_cq, w_ckv,
                             _slot_cols(w_kr, 1, MLA_ROPE, MLA_NOPE),
                             _slot_cols(_rot_cols(w_kr), 1, MLA_ROPE, MLA_NOPE),
                             _slot_cols(w_kb, MOBA_HEADS, MOBA_HEAD_DIM)], axis=1)
    w_t = jnp.concatenate([w_qb, w_vb], axis=1).T
    wq = w_uq.reshape(MLA_Q_RANK, MLA_HEADS, MLA_QK)
    wq_nope, wq_rope = wq[:, :, :MLA_NOPE], wq[:, :, MLA_NOPE:]
    wq_rot = jnp.concatenate([-wq_rope[:, :, MLA_ROPE // 2:], wq_rope[:, :, :MLA_ROPE // 2]], axis=2)
    w_uq_t = jnp.concatenate([wq_nope, wq_rope, wq_rot], axis=2).reshape(
        MLA_Q_RANK, MLA_HEADS * HEAD_SLOT).T
    wkv = w_ukv.reshape(MLA_KV_RANK, MLA_HEADS, MLA_NOPE + MLA_V)
    w_ukv_std = _slot_cols(wkv[:, :, :MLA_NOPE].reshape(MLA_KV_RANK, MLA_HEADS * MLA_NOPE),
                           MLA_HEADS, MLA_NOPE)
    w_ukv_vt = wkv[:, :, MLA_NOPE:].reshape(MLA_KV_RANK, MLA_HEADS * MLA_V).T
    return {
        "w_std": w_std.astype(BF16), "w_t": w_t.astype(BF16),
        "g_cq": g_cq.reshape(1, -1), "g_ckv": g_ckv.reshape(1, -1),
        "w_uq_t": w_uq_t.astype(BF16), "w_ukv_std": w_ukv_std.astype(BF16),
        "w_ukv_vt": w_ukv_vt.astype(BF16),
        "gq_mla": (g_qn_mla * (MLA_QK ** -0.5 * LOG2E)).reshape(-1, 1),
        "gk_mla": jnp.pad(g_kn_mla, (0, HEAD_SLOT - MLA_QK)).reshape(1, -1),
        "gq_moba": (g_qn_moba * (MOBA_HEAD_DIM ** -0.5 * LOG2E)).reshape(-1, 1),
        "gk_moba": jnp.pad(g_kn_moba, (0, HEAD_SLOT - MOBA_HEAD_DIM)).reshape(1, -1),
    }


def _rope_tables(S):
    half = MLA_ROPE // 2
    inv = ROPE_THETA ** (-jnp.arange(half, dtype=F32) / half)
    ang = jnp.arange(S).astype(F32)[:, None] * inv[None, :]
    cos = jnp.tile(jnp.cos(ang), (1, 2))
    sin = jnp.tile(jnp.sin(ang), (1, 2))
    pad = ((0, 0), (MLA_NOPE, HEAD_SLOT - MLA_QK))
    return {"cos_t": cos.T, "sin_t": sin.T, "cos_r": jnp.pad(cos, pad), "sin_r": jnp.pad(sin, pad)}


def _s5_params(lam_re, lam_im, log_dt, b_re, b_im, c_re, c_im):
    hi = lax.Precision.HIGHEST
    G, P, L = S5_GROUPS, S5_STATE, S5_L
    dt = jnp.exp(log_dt)[:, None]

    steps = jnp.arange(L + 1, dtype=F32)[:, None, None]
    mag = jnp.exp(lam_re * dt * steps)
    pw_re = mag * jnp.cos(lam_im * dt * steps)
    pw_im = mag * jnp.sin(lam_im * dt * steps)
    a_re, a_im = pw_re[1], pw_im[1]
    den = lam_re * lam_re + lam_im * lam_im
    k_re = ((a_re - 1.0) * lam_re + a_im * lam_im) / den
    k_im = (a_im * lam_re - (a_re - 1.0) * lam_im) / den
    bb_re = k_re[..., None] * b_re - k_im[..., None] * b_im
    bb_im = k_re[..., None] * b_im + k_im[..., None] * b_re

    e_re = pw_re[:L, ..., None] * bb_re - pw_im[:L, ..., None] * bb_im
    e_im = pw_re[:L, ..., None] * bb_im + pw_im[:L, ..., None] * bb_re
    lag = (jnp.einsum("gop,sgpi->sgoi", c_re, e_re, precision=hi)
           - jnp.einsum("gop,sgpi->sgoi", c_im, e_im, precision=hi))
    I = S5_GROUP
    kcat = lag.reshape(L, N_SLABS, S5_SLAB, I, I)
    kcat = kcat.transpose(1, 0, 4, 2, 3).reshape(N_SLABS, L, I, S5_SLAB * I)

    f = jnp.stack([e_re[::-1], e_im[::-1]], axis=0)
    f = f.reshape(2, L, N_SLABS, S5_SLAB, P, I)
    fcat = f.transpose(2, 0, 1, 5, 3, 4).reshape(N_SLABS, 2, L, I, S5_SLAB * P)

    nr, ni = pw_re[1:, :, None, :], pw_im[1:, :, None, :]
    q = jnp.stack([c_re * nr - c_im * ni, -(c_re * ni + c_im * nr)], axis=0)
    q = q.reshape(2, L, N_SLABS, S5_SLAB, I, P)
    qcat = q.transpose(2, 0, 1, 5, 3, 4).reshape(N_SLABS, 2, L, P, S5_SLAB * I)

    al_re, al_im = pw_re[L], pw_im[L]
    a_chunk = jnp.concatenate([al_re.reshape(N_SLABS, 1, S5_SLAB * P),
                               al_im.reshape(N_SLABS, 1, S5_SLAB * P)], axis=2)
    return {"kcat": kcat, "fcat": fcat, "qcat": qcat, "a_chunk": a_chunk}


def _cast_kernel(w_ref, o_ref):
    o_ref[...] = w_ref[...].astype(BF16)


def _to_bf16(w):
    n, r, c = w.shape
    rb = min(r, CAST_BLOCK_BYTES // (4 * c))
    spec = pl.BlockSpec((1, rb, c), lambda i, j: (i, j, 0))
    return pl.pallas_call(
        _cast_kernel, grid=(n, r // rb), in_specs=[spec], out_specs=spec,
        out_shape=jax.ShapeDtypeStruct(w.shape, BF16), name="to_bf16")(w)


def kernel(x, mix_norm_g, ffn_norm_g, w_in, g_cq, w_uq, g_ckv, w_ukv, g_qn_mla, g_kn_mla,
           g_qn_moba, g_kn_moba, w_o, lam_re, lam_im, log_dt, b_re, b_im, c_re, c_im,
           d_skip, w_glu, w_ff1, w_ff2):
    B, S, D = x.shape
    depth = mix_norm_g.shape[0]
    assert D == D_MODEL and S % ATT_TILE == 0 and S % FFN_TILE == 0 and depth % 2 == 0
    assert S % min(PROJ_TILE, S) == 0 and PROJ_TILE % ATT_TILE == 0
    tabs = _rope_tables(S)
    row = lambda v: v.reshape(1, -1)
    w_ff1, w_ff2, w_glu, w_o = _to_bf16(w_ff1), _to_bf16(w_ff2), _to_bf16(w_glu), _to_bf16(w_o)
    aps = jax.vmap(_attn_params)(w_in, g_cq, w_uq, g_ckv, w_ukv, g_qn_mla, g_kn_mla, g_qn_moba,
                                 g_kn_moba)
    sps = jax.vmap(_s5_params)(lam_re, lam_im, log_dt, b_re, b_im, c_re, c_im)
    for layer in range(0, depth, 2):
        i = layer // 2
        ap = {name: _Layer(stack, i) for name, stack in aps.items()}
        qt, k, vt = _attn_proj(x, row(mix_norm_g[layer]), ap, tabs)
        o = _flash(qt, k, vt)
        x, uc = _attn_out_ffn(x, o, _Layer(w_o, i), row(ffn_norm_g[layer]), _Layer(w_ff1, layer),
                              _Layer(w_ff2, layer), row(mix_norm_g[layer + 1]))
        sp = {name: _Layer(stack, i) for name, stack in sps.items()}
        yc = _s5_core(uc, sp)
        x = _s5_out_ffn(x, yc, row(mix_norm_g[layer + 1]), row(d_skip[i]), _Layer(w_glu, i),
                        row(ffn_norm_g[layer + 1]), _Layer(w_ff1, layer + 1),
                        _Layer(w_ff2, layer + 1))
    return x
```

```python
import functools
import math
from typing import NamedTuple

import jax
import jax.numpy as jnp
from jax import lax
from jax.experimental import pallas as pl
from jax.experimental.pallas import tpu as pltpu

F32 = jnp.float32
BF16 = jnp.bfloat16

D_MODEL = 1024
D_FF = 4 * D_MODEL
EPS = 1e-6
MLA_HEADS = 8
MLA_NOPE = 64
MLA_ROPE = 32
MLA_V = 64
MLA_QK = MLA_NOPE + MLA_ROPE
MLA_Q_RANK = 256
MLA_KV_RANK = 128
ROPE_THETA = 10000.0
MOBA_HEADS = 8
MOBA_HEAD_DIM = 64
MOBA_W = MOBA_HEADS * MOBA_HEAD_DIM
MOBA_BLOCK = 256
MOBA_TOPK = 3
S5_GROUP = 16
S5_GROUPS = D_MODEL // S5_GROUP
S5_STATE = 64

LANE = 128
HEAD_SLOT = 128
N_HEADS = MLA_HEADS + MOBA_HEADS
MASK_VALUE = -1e30
SUM_ROWS = 16
LOG2E = math.log2(math.e)
ATT_TILE = 512
PROJ_TILE = 1024
SCORE_ROWS = 256
FLASH_HEADS = 4
FFN_TILE = 512
FF_CHUNK = 1024
S5_L = 8
S5_SLAB = LANE // S5_GROUP
N_SLABS = D_MODEL // LANE
S5_ROWS = 1024
VMEM_LIMIT = 56 * 1024 * 1024
CAST_BLOCK_BYTES = 4 * 1024 * 1024

_NT = (((1,), (1,)), ((), ()))


def _dot(a, b, precision=None):
    return jnp.dot(a, b, preferred_element_type=F32, precision=precision)


def _dot_nt(a, b):
    return lax.dot_general(a, b, _NT, preferred_element_type=F32)


def _rms_rows(x, gain_row):
    ms = jnp.mean(x * x, axis=-1, keepdims=True)
    return x * lax.rsqrt(ms + EPS) * gain_row


def _const_spec(shape):
    nd = len(shape)
    return pl.BlockSpec(shape, lambda *_: (0,) * nd, pipeline_mode=pl.Buffered(1))


class _Layer(NamedTuple):
    stack: jax.Array
    index: int


def _weight_spec(w):
    if isinstance(w, _Layer):
        nd = w.stack.ndim - 1
        return pl.BlockSpec((None,) + w.stack.shape[1:], lambda *_: (w.index,) + (0,) * nd,
                            pipeline_mode=pl.Buffered(1))
    return _const_spec(w.shape)


def _operand(w):
    return w.stack if isinstance(w, _Layer) else w


def _attn_proj_kernel(x_ref, gmix_ref, wstd_ref, wt_ref, gcq_ref, gckv_ref, wuqt_ref, wukv_ref,
                      wukvvt_ref, cost_ref, sint_ref, cosr_ref, sinr_ref, gqmla_ref, gkmla_ref,
                      gqmoba_ref, gkmoba_ref, qt_ref, k_ref, vt_ref, km_ref, *, tile):
    ti = pl.program_id(1)
    nsub = tile // MOBA_BLOCK
    sub = MOBA_BLOCK
    per_q = qt_ref.shape[4] // sub

    @pl.when(ti == 0)
    def _():
        km_ref[...] = jnp.zeros_like(km_ref)

    gq_mla = gqmla_ref[...]
    gk_mla = gkmla_ref[...]
    gq_moba = gqmoba_ref[...]
    gk_moba = gkmoba_ref[...]
    nb = km_ref.shape[1]
    pad_q = jnp.zeros((HEAD_SLOT - MLA_QK, sub), F32)
    pad_m = jnp.zeros((HEAD_SLOT - MOBA_HEAD_DIM - nb, sub), F32)
    jidx = lax.broadcasted_iota(jnp.int32, (nb, sub), 0)
    lower = [jnp.where(jp < jidx, 1.0, 0.0) for jp in range(nb)]
    lane = lax.broadcasted_iota(jnp.int32, (sub, HEAD_SLOT), 1)

    def project(sb):
        rows = slice(sb * sub, (sb + 1) * sub)
        hb = _rms_rows(x_ref[0, rows, :], gmix_ref[...]).astype(BF16)
        pstd = _dot(hb, wstd_ref[...])
        pt = _dot_nt(wt_ref[...], hb)
        return pstd, pt

    def finish(sb, pstd, pt):
        rows = slice(sb * sub, (sb + 1) * sub)
        cols = slice((sb % per_q) * sub, (sb % per_q + 1) * sub)
        qtile = sb // per_q
        cur = ti * nsub + sb
        o = 0
        cq = pstd[:, o:o + MLA_Q_RANK]; o += MLA_Q_RANK
        ckv = pstd[:, o:o + MLA_KV_RANK]; o += MLA_KV_RANK
        kr_a = pstd[:, o:o + HEAD_SLOT]; o += HEAD_SLOT
        kr_b = pstd[:, o:o + HEAD_SLOT]; o += HEAD_SLOT
        kb_off = o

        cqn = _rms_rows(cq, gcq_ref[...]).astype(BF16)
        ckvn = _rms_rows(ckv, gckv_ref[...]).astype(BF16)
        qup_t = _dot_nt(wuqt_ref[...], cqn)
        kv_std = _dot(ckvn, wukv_ref[...])
        vt_ref[0, qtile, 0:MLA_HEADS * MLA_V, cols] = _dot_nt(wukvvt_ref[...], ckvn).astype(BF16)
        vt_ref[0, qtile, MLA_HEADS * MLA_V:, cols] = pt[MOBA_W:, :].astype(BF16)
        k_rope = kr_a * cosr_ref[rows, :] + kr_b * sinr_ref[rows, :]
        cos_t = cost_ref[:, rows]
        sin_t = sint_ref[:, rows]
        for h in range(MLA_HEADS):
            blk = qup_t[h * HEAD_SLOT:(h + 1) * HEAD_SLOT]
            nope = blk[0:MLA_NOPE]
            rope = (blk[MLA_NOPE:MLA_QK] * cos_t + blk[MLA_QK:MLA_QK + MLA_ROPE] * sin_t)
            ssq = (jnp.sum(nope * nope, axis=0, keepdims=True)
                   + jnp.sum(rope * rope, axis=0, keepdims=True))
            r = lax.rsqrt(ssq * (1.0 / MLA_QK) + EPS)
            qn = jnp.concatenate([nope * r * gq_mla[0:MLA_NOPE], rope * r * gq_mla[MLA_NOPE:MLA_QK],
                                  pad_q], axis=0)
            qt_ref[0, h, qtile, :, cols] = qn.astype(BF16)
            kh = kv_std[:, h * HEAD_SLOT:(h + 1) * HEAD_SLOT] + k_rope
            ssk = jnp.sum(kh * kh, axis=-1, keepdims=True)
            k_ref[0, h, rows, :] = (kh * lax.rsqrt(ssk * (1.0 / MLA_QK) + EPS) * gk_mla).astype(BF16)

        onehot = jnp.where(lane == MOBA_HEAD_DIM + cur, 1.0, 0.0)
        for h in range(MOBA_HEADS):
            kh = pstd[:, kb_off + h * HEAD_SLOT:kb_off + (h + 1) * HEAD_SLOT]
            ssk = jnp.sum(kh * kh, axis=-1, keepdims=True)
            kn = kh * lax.rsqrt(ssk * (1.0 / MOBA_HEAD_DIM) + EPS) * gk_moba
            km_ref[h, pl.ds(cur, 1), :] = jnp.mean(kn, axis=0, keepdims=True)
            k_ref[0, MLA_HEADS + h, rows, :] = (kn + onehot).astype(BF16)

            qh = pt[h * MOBA_HEAD_DIM:(h + 1) * MOBA_HEAD_DIM]
            ssq = jnp.sum(qh * qh, axis=0, keepdims=True)
            qn = qh * lax.rsqrt(ssq * (1.0 / MOBA_HEAD_DIM) + EPS) * gq_moba
            gate = _dot(km_ref[h][:, 0:MOBA_HEAD_DIM], qn, precision=lax.Precision.HIGHEST)
            cnt = jnp.zeros((nb, sub), F32)
            for jp in range(nb):
                gj = gate[jp:jp + 1, :]
                tie = jnp.where(gj == gate, lower[jp], 0.0)
                cnt = cnt + jnp.where(jp < cur, jnp.where(gj > gate, 1.0, tie), 0.0)
            bias = jnp.where(jidx < cur, jnp.where(cnt < float(MOBA_TOPK), 0.0, MASK_VALUE),
                             jnp.where(jidx == cur, 0.0, MASK_VALUE))
            qt_ref[0, MLA_HEADS + h, qtile, :, cols] = (
                jnp.concatenate([qn, bias, pad_m], axis=0).astype(BF16))

    nxt = project(0)
    for sb in range(nsub):
        cur_vals = nxt
        if sb + 1 < nsub:
            nxt = project(sb + 1)
        finish(sb, *cur_vals)


def _attn_proj(x, gmix, p, tabs):
    B, S, D = x.shape
    T = min(PROJ_TILE, S)
    tf = ATT_TILE
    per = T // tf
    nt = S // T
    nb = S // MOBA_BLOCK
    consts = [gmix, p["w_std"], p["w_t"], p["g_cq"], p["g_ckv"], p["w_uq_t"], p["w_ukv_std"],
              p["w_ukv_vt"]]
    tail = [p["gq_mla"], p["gk_mla"], p["gq_moba"], p["gk_moba"]]
    in_specs = ([pl.BlockSpec((1, T, D), lambda b, t: (b, t, 0))]
                + [_const_spec(c.shape) for c in consts]
                + [pl.BlockSpec((MLA_ROPE, T), lambda b, t: (0, t)),
                   pl.BlockSpec((MLA_ROPE, T), lambda b, t: (0, t)),
                   pl.BlockSpec((T, HEAD_SLOT), lambda b, t: (t, 0)),
                   pl.BlockSpec((T, HEAD_SLOT), lambda b, t: (t, 0))]
                + [_const_spec(c.shape) for c in tail])
    out_shape = (jax.ShapeDtypeStruct((B, N_HEADS, S // tf, HEAD_SLOT, tf), BF16),
                 jax.ShapeDtypeStruct((B, N_HEADS, S, HEAD_SLOT), BF16),
                 jax.ShapeDtypeStruct((B, S // tf, N_HEADS * MLA_V, tf), BF16))
    out_specs = (pl.BlockSpec((1, N_HEADS, per, HEAD_SLOT, tf), lambda b, t: (b, 0, t, 0, 0)),
                 pl.BlockSpec((1, N_HEADS, T, HEAD_SLOT), lambda b, t: (b, 0, t, 0)),
                 pl.BlockSpec((1, per, N_HEADS * MLA_V, tf), lambda b, t: (b, t, 0, 0)))
    return pl.pallas_call(
        functools.partial(_attn_proj_kernel, tile=T),
        grid=(B, nt),
        in_specs=in_specs,
        out_specs=out_specs,
        out_shape=out_shape,
        scratch_shapes=[pltpu.VMEM((MOBA_HEADS, nb, HEAD_SLOT), F32)],
        compiler_params=pltpu.CompilerParams(
            dimension_semantics=("arbitrary", "arbitrary"), vmem_limit_bytes=VMEM_LIMIT),
        name="attn_proj",
    )(x, *consts, tabs["cos_t"], tabs["sin_t"], tabs["cos_r"], tabs["sin_r"], *tail)


def _flash_kernel(qt_ref, k_ref, vt_ref, o_ref, s_scr, acc_scr):
    nh, nt, _, tile = qt_ref.shape[1:]
    nchunk = tile // SCORE_ROWS

    def score_chunk(hh, qidx, j, r, cmax):
        k0 = pl.multiple_of(j * tile, tile) + r * SCORE_ROWS
        s = _dot(k_ref[0, hh, pl.ds(k0, SCORE_ROWS), :], qt_ref[0, hh, qidx])
        s_scr[hh, r * SCORE_ROWS:(r + 1) * SCORE_ROWS, :] = s
        c = jnp.max(s, axis=0, keepdims=True)
        return c if cmax is None else jnp.maximum(cmax, c)

    def step(qi, j, ms, cmaxes, diagonal):
        ones = jnp.ones((SUM_ROWS, tile), BF16)
        new_ms, new_cmaxes = [], []
        for hh in range(nh):
            def chunk(r):
                s = s_scr[hh, r * SCORE_ROWS:(r + 1) * SCORE_ROWS, :]
                if diagonal:
                    key = lax.broadcasted_iota(jnp.int32, s.shape, 0) + r * SCORE_ROWS
                    qry = lax.broadcasted_iota(jnp.int32, s.shape, 1)
                    s = jnp.where(key <= qry, s, MASK_VALUE)
                return s
            cmax = cmaxes[hh]
            if diagonal:
                cmax = functools.reduce(jnp.maximum, [jnp.max(chunk(r), axis=0, keepdims=True)
                                                      for r in range(nchunk)])
            m_new = jnp.maximum(ms[hh], cmax)
            nxt = None
            probs = []
            for r in range(nchunk):
                probs.append(jnp.exp2(chunk(r) - m_new).astype(BF16))
                if diagonal:
                    nxt = score_chunk(hh, jnp.minimum(qi + 1, nt - 1), 0, r, nxt)
                else:
                    nxt = score_chunk(hh, qi, j + 1, r, nxt)
            v_ext = jnp.concatenate([vt_ref[0, j, hh * MLA_V:(hh + 1) * MLA_V, :], ones], axis=0)
            acc_scr[hh] = (jnp.exp2(ms[hh] - m_new) * acc_scr[hh]
                           + _dot(v_ext, jnp.concatenate(probs, axis=0)))
            new_ms.append(m_new)
            new_cmaxes.append(nxt)
        return new_ms, new_cmaxes

    cmax0 = []
    for hh in range(nh):
        c = None
        for r in range(nchunk):
            c = score_chunk(hh, 0, 0, r, c)
        cmax0.append(c)
    m0 = jnp.full((1, tile), MASK_VALUE, F32)

    def query_tile(qi, cmaxes):
        acc_scr[...] = jnp.zeros_like(acc_scr)

        def body(t, carry):
            ms, cms = step(qi, t, carry[0:nh], carry[nh:], False)
            return tuple(ms) + tuple(cms)

        carry = lax.fori_loop(0, qi, body, (m0,) * nh + tuple(cmaxes))
        _, cms = step(qi, qi, carry[0:nh], carry[nh:], True)
        outs = []
        for hh in range(nh):
            acc = acc_scr[hh]
            outs.append(acc[0:MLA_V] * (1.0 / acc[MLA_V:MLA_V + 1]))
        o_ref[0, pl.ds(pl.multiple_of(qi * tile, tile), tile), :] = (
            jnp.concatenate(outs, axis=0).T.astype(BF16))
        return tuple(cms)

    lax.fori_loop(0, nt, query_tile, tuple(cmax0))


def _flash(qt, k, vt):
    B, H, nt, _, T = qt.shape
    S = nt * T
    nh = FLASH_HEADS
    return pl.pallas_call(
        _flash_kernel,
        grid=(B, H // nh),
        in_specs=[pl.BlockSpec((1, nh, nt, HEAD_SLOT, T), lambda b, p: (b, p, 0, 0, 0)),
                  pl.BlockSpec((1, nh, S, HEAD_SLOT), lambda b, p: (b, p, 0, 0)),
                  pl.BlockSpec((1, nt, nh * MLA_V, T), lambda b, p: (b, 0, p, 0))],
        out_specs=pl.BlockSpec((1, S, nh * MLA_V), lambda b, p: (b, 0, p)),
        out_shape=jax.ShapeDtypeStruct((B, S, H * MLA_V), BF16),
        scratch_shapes=[pltpu.VMEM((nh, T, T), F32),
                        pltpu.VMEM((nh, MLA_V + SUM_ROWS, T), F32)],
        compiler_params=pltpu.CompilerParams(
            dimension_semantics=("arbitrary", "arbitrary"), vmem_limit_bytes=VMEM_LIMIT),
        name="flash",
    )(qt, k, vt)


def _ffn(x1, gffn_ref, w1_ref, w2_ref):
    hb = _rms_rows(x1, gffn_ref[...]).astype(BF16)
    acc = x1
    for c in range(D_FF // FF_CHUNK):
        a = _dot(hb, w1_ref[:, c * FF_CHUNK:(c + 1) * FF_CHUNK])
        a = jnp.square(jnp.maximum(a, 0.0)).astype(BF16)
        acc = acc + _dot(a, w2_ref[c * FF_CHUNK:(c + 1) * FF_CHUNK, :])
    return acc


def _emit_chunk_rows(x2, gnext_ref, u_scr, uc_ref, tile):
    u = _rms_rows(x2, gnext_ref[...])
    rows = tile // S5_L
    for j in range(N_SLABS):
        u_scr[j] = u[:, j * LANE:(j + 1) * LANE]
        for l in range(S5_L):
            uc_ref[j, 0, :, l * LANE:(l + 1) * LANE] = (
                u_scr[j, pl.ds(l, rows, stride=S5_L), :].astype(BF16))


def _attn_out_ffn_kernel(x_ref, o_ref, wo_ref, gffn_ref, w1_ref, w2_ref, gnext_ref,
                         x2_ref, uc_ref, u_scr, *, tile):
    x1 = x_ref[0] + _dot(o_ref[0], wo_ref[...])
    x2 = _ffn(x1, gffn_ref, w1_ref, w2_ref)
    x2_ref[0] = x2
    _emit_chunk_rows(x2, gnext_ref, u_scr, uc_ref, tile)


def _bgroup(B):
    return 8 if B % 8 == 0 else B


def _attn_out_ffn(x, o, wo, gffn, w1, w2, gnext):
    B, S, D = x.shape
    T = FFN_TILE
    row_w = S5_L * LANE
    consts_a = [wo, gffn, w1, w2, gnext]
    return pl.pallas_call(
        functools.partial(_attn_out_ffn_kernel, tile=T),
        grid=(B, S // T),
        in_specs=[pl.BlockSpec((1, T, D), lambda b, t: (b, t, 0)),
                  pl.BlockSpec((1, T, D), lambda b, t: (b, t, 0))]
                 + [_weight_spec(c) for c in consts_a],
        out_specs=(pl.BlockSpec((1, T, D), lambda b, t: (b, t, 0)),
                   pl.BlockSpec((N_SLABS, 1, T // S5_L, row_w), lambda b, t: (0, b, t, 0))),
        out_shape=(jax.ShapeDtypeStruct((B, S, D), F32),
                   jax.ShapeDtypeStruct((N_SLABS, B, S // S5_L, row_w), BF16)),
        scratch_shapes=[pltpu.VMEM((N_SLABS, T, LANE), F32)],
        compiler_params=pltpu.CompilerParams(
            dimension_semantics=("arbitrary", "arbitrary"), vmem_limit_bytes=VMEM_LIMIT),
        name="attn_out_ffn",
    )(x, o, *[_operand(c) for c in consts_a])


def _s5_expand(kcat_ref, fcat_ref, qcat_ref, wt_scr, min_scr, mout_scr):
    L, I, P = S5_L, S5_GROUP, S5_STATE
    half = S5_SLAB * P

    def same_group(shape, rows_per_group, cols_per_group):
        r = lax.broadcasted_iota(jnp.int32, shape, 0) // rows_per_group
        c = lax.broadcasted_iota(jnp.int32, shape, 1) // cols_per_group
        return r == c

    def tiled(block, mask):
        return jnp.where(mask, jnp.concatenate([block] * S5_SLAB, axis=0), 0.0).astype(BF16)

    m_kk = same_group((LANE, LANE), I, I)
    m_in = same_group((LANE, half), I, P)
    m_out = same_group((half, LANE), P, I)
    zeros = jnp.zeros((LANE, LANE), BF16)
    lag_blocks = [tiled(kcat_ref[0, s], m_kk) for s in range(L)]
    for li in range(L):
        for lo in range(L):
            wt_scr[li * LANE:(li + 1) * LANE, lo * LANE:(lo + 1) * LANE] = (
                lag_blocks[lo - li] if lo >= li else zeros)
    for c in range(2):
        for l in range(L):
            min_scr[l * LANE:(l + 1) * LANE, c * half:(c + 1) * half] = tiled(fcat_ref[0, c, l], m_in)
            mout_scr[c * half:(c + 1) * half, l * LANE:(l + 1) * LANE] = tiled(qcat_ref[0, c, l], m_out)


def _s5_kernel(u_ref, kcat_ref, fcat_ref, qcat_ref, a_ref, y_ref,
               wt_scr, min_scr, mout_scr, v_scr, h_scr, st_scr):
    @pl.when((pl.program_id(1) == 0) & (pl.program_id(2) == 0))
    def _():
        _s5_expand(kcat_ref, fcat_ref, qcat_ref, wt_scr, min_scr, mout_scr)

    @pl.when(pl.program_id(2) == 0)
    def _():
        st_scr[...] = jnp.zeros_like(st_scr)

    bg, rc, row_w = u_ref.shape[1:]
    n_tiles = st_scr.shape[0]
    half = n_tiles // 2
    u = u_ref[0].reshape(bg * rc, row_w)
    v = _dot(u, min_scr[...])
    for t in range(n_tiles):
        for s in range(bg):
            v_scr[t, pl.ds(s, rc, stride=bg), :] = v[s * rc:(s + 1) * rc, t * LANE:(t + 1) * LANE]
    a = a_ref[0]
    a_t = [jnp.broadcast_to(a[:, t * LANE:(t + 1) * LANE], (bg, LANE)) for t in range(n_tiles)]

    def step(c, h):
        rows = slice(c * bg, (c + 1) * bg)
        new = []
        for t in range(n_tiles):
            h_scr[t, rows, :] = h[t]
        for t in range(half):
            new.append(a_t[t] * h[t] - a_t[half + t] * h[half + t] + v_scr[t, rows, :])
        for t in range(half):
            new.append(a_t[t] * h[half + t] + a_t[half + t] * h[t] + v_scr[half + t, rows, :])
        return tuple(new)

    y_lag = _dot(u, wt_scr[...])
    h = tuple(st_scr[t] for t in range(n_tiles))
    for c in range(rc):
        h = step(c, h)
    for t in range(n_tiles):
        st_scr[t] = h[t]
    h_in = jnp.concatenate(
        [jnp.concatenate([h_scr[t, pl.ds(s, rc, stride=bg), :] for t in range(n_tiles)], axis=1)
         for s in range(bg)], axis=0)
    y = y_lag + _dot(h_in.astype(BF16), mout_scr[...])
    y_ref[0] = y.reshape(bg, rc, row_w)


def _s5_core(uc, p):
    n_slab, B, n_chunks, row_w = uc.shape
    bg = _bgroup(B)
    rc = min(S5_ROWS // bg, n_chunks)
    n_state = 2 * S5_SLAB * S5_STATE
    n_tiles = n_state // LANE
    tables = [p["kcat"], p["fcat"], p["qcat"], p["a_chunk"]]

    def slab_spec(a):
        nd = a.ndim - 1
        return pl.BlockSpec((1,) + a.shape[1:], lambda j, g, r: (j,) + (0,) * nd)

    return pl.pallas_call(
        _s5_kernel,
        grid=(n_slab, B // bg, n_chunks // rc),
        in_specs=[pl.BlockSpec((1, bg, rc, row_w), lambda j, g, r: (j, g, r, 0))]
                 + [slab_spec(a) for a in tables],
        out_specs=pl.BlockSpec((1, bg, rc, row_w), lambda j, g, r: (j, g, r, 0)),
        out_shape=jax.ShapeDtypeStruct((n_slab, B, n_chunks, row_w), F32),
        scratch_shapes=[pltpu.VMEM((row_w, row_w), BF16), pltpu.VMEM((row_w, n_state), BF16),
                        pltpu.VMEM((n_state, row_w), BF16),
                        pltpu.VMEM((n_tiles, bg * rc, LANE), F32),
                        pltpu.VMEM((n_tiles, bg * rc, LANE), F32),
                        pltpu.VMEM((n_tiles, bg, LANE), F32)],
        compiler_params=pltpu.CompilerParams(
            dimension_semantics=("arbitrary", "arbitrary", "arbitrary"),
            vmem_limit_bytes=VMEM_LIMIT),
        name="s5_core",
    )(uc, *tables)


def _gelu_tanh(y):
    c = math.sqrt(2.0 / math.pi)
    return 0.5 * y * (1.0 + jnp.tanh(c * (y + 0.044715 * (y * y * y))))


def _s5_out_ffn_kernel(x_ref, yc_ref, gmix_ref, dskip_ref, wglu_ref, gffn_ref, w1_ref, w2_ref,
                       x2_ref, y_scr, *, tile):
    x = x_ref[0]
    rows = tile // S5_L
    for j in range(N_SLABS):
        for l in range(S5_L):
            y_scr[j, pl.ds(l, rows, stride=S5_L), :] = yc_ref[j, 0, :, l * LANE:(l + 1) * LANE]
    y = jnp.concatenate([y_scr[j] for j in range(N_SLABS)], axis=1)
    u = _rms_rows(x, gmix_ref[...])
    g = _gelu_tanh(y + dskip_ref[...] * u).astype(BF16)
    vg = _dot(g, wglu_ref[...])
    x1 = x + vg[:, :D_MODEL] * jax.nn.sigmoid(vg[:, D_MODEL:])
    x2_ref[0] = _ffn(x1, gffn_ref, w1_ref, w2_ref)


def _s5_out_ffn(x, yc, gmix, dskip, wglu, gffn, w1, w2):
    B, S, D = x.shape
    T = FFN_TILE
    row_w = S5_L * LANE
    consts = [gmix, dskip, wglu, gffn, w1, w2]
    return pl.pallas_call(
        functools.partial(_s5_out_ffn_kernel, tile=T),
        grid=(B, S // T),
        in_specs=[pl.BlockSpec((1, T, D), lambda b, t: (b, t, 0)),
                  pl.BlockSpec((N_SLABS, 1, T // S5_L, row_w), lambda b, t: (0, b, t, 0))]
                 + [_weight_spec(c) for c in consts],
        out_specs=pl.BlockSpec((1, T, D), lambda b, t: (b, t, 0)),
        out_shape=jax.ShapeDtypeStruct((B, S, D), F32),
        scratch_shapes=[pltpu.VMEM((N_SLABS, T, LANE), F32)],
        compiler_params=pltpu.CompilerParams(
            dimension_semantics=("arbitrary", "arbitrary"), vmem_limit_bytes=VMEM_LIMIT),
        name="s5_out_ffn",
    )(x, yc, *[_operand(c) for c in consts])


def _rot_cols(w):
    half = w.shape[1] // 2
    return jnp.concatenate([-w[:, half:], w[:, :half]], axis=1)


def _slot_cols(w, n_heads, width, offset=0):
    k = w.shape[0]
    w = w.reshape(k, n_heads, width)
    w = jnp.pad(w, ((0, 0), (0, 0), (offset, HEAD_SLOT - width - offset)))
    return w.reshape(k, n_heads * HEAD_SLOT)


def _attn_params(w_in, g_cq, w_uq, g_ckv, w_ukv, g_qn_mla, g_kn_mla, g_qn_moba, g_kn_moba):
    sizes = [MLA_Q_RANK, MLA_KV_RANK, MLA_ROPE, MOBA_W, MOBA_W, MOBA_W]
    offs = [0]
    for s in sizes:
        offs.append(offs[-1] + s)
    w_cq, w_ckv, w_kr, w_qb, w_kb, w_vb = [w_in[:, offs[i]:offs[i + 1]] for i in range(6)]
    w_std = jnp.concatenate([w_cq, w_ckv,
                             _slot_cols(w_kr, 1, MLA_ROPE, MLA_NOPE),
                             _slot_cols(_rot_cols(w_kr), 1, MLA_ROPE, MLA_NOPE),
                             _slot_cols(w_kb, MOBA_HEADS, MOBA_HEAD_DIM)], axis=1)
    w_t = jnp.concatenate([w_qb, w_vb], axis=1).T
    wq = w_uq.reshape(MLA_Q_RANK, MLA_HEADS, MLA_QK)
    wq_nope, wq_rope = wq[:, :, :MLA_NOPE], wq[:, :, MLA_NOPE:]
    wq_rot = jnp.concatenate([-wq_rope[:, :, MLA_ROPE // 2:], wq_rope[:, :, :MLA_ROPE // 2]], axis=2)
    w_uq_t = jnp.concatenate([wq_nope, wq_rope, wq_rot], axis=2).reshape(
        MLA_Q_RANK, MLA_HEADS * HEAD_SLOT).T
    wkv = w_ukv.reshape(MLA_KV_RANK, MLA_HEADS, MLA_NOPE + MLA_V)
    w_ukv_std = _slot_cols(wkv[:, :, :MLA_NOPE].reshape(MLA_KV_RANK, MLA_HEADS * MLA_NOPE),
                           MLA_HEADS, MLA_NOPE)
    w_ukv_vt = wkv[:, :, MLA_NOPE:].reshape(MLA_KV_RANK, MLA_HEADS * MLA_V).T
    return {
        "w_std": w_std.astype(BF16), "w_t": w_t.astype(BF16),
        "g_cq": g_cq.reshape(1, -1), "g_ckv": g_ckv.reshape(1, -1),
        "w_uq_t": w_uq_t.astype(BF16), "w_ukv_std": w_ukv_std.astype(BF16),
        "w_ukv_vt": w_ukv_vt.astype(BF16),
        "gq_mla": (g_qn_mla * (MLA_QK ** -0.5 * LOG2E)).reshape(-1, 1),
        "gk_mla": jnp.pad(g_kn_mla, (0, HEAD_SLOT - MLA_QK)).reshape(1, -1),
        "gq_moba": (g_qn_moba * (MOBA_HEAD_DIM ** -0.5 * LOG2E)).reshape(-1, 1),
        "gk_moba": jnp.pad(g_kn_moba, (0, HEAD_SLOT - MOBA_HEAD_DIM)).reshape(1, -1),
    }


def _rope_tables(S):
    half = MLA_ROPE // 2
    inv = ROPE_THETA ** (-jnp.arange(half, dtype=F32) / half)
    ang = jnp.arange(S).astype(F32)[:, None] * inv[None, :]
    cos = jnp.tile(jnp.cos(ang), (1, 2))
    sin = jnp.tile(jnp.sin(ang), (1, 2))
    pad = ((0, 0), (MLA_NOPE, HEAD_SLOT - MLA_QK))
    return {"cos_t": cos.T, "sin_t": sin.T, "cos_r": jnp.pad(cos, pad), "sin_r": jnp.pad(sin, pad)}


def _s5_params(lam_re, lam_im, log_dt, b_re, b_im, c_re, c_im):
    hi = lax.Precision.HIGHEST
    G, P, L = S5_GROUPS, S5_STATE, S5_L
    dt = jnp.exp(log_dt)[:, None]

    steps = jnp.arange(L + 1, dtype=F32)[:, None, None]
    mag = jnp.exp(lam_re * dt * steps)
    pw_re = mag * jnp.cos(lam_im * dt * steps)
    pw_im = mag * jnp.sin(lam_im * dt * steps)
    a_re, a_im = pw_re[1], pw_im[1]
    den = lam_re * lam_re + lam_im * lam_im
    k_re = ((a_re - 1.0) * lam_re + a_im * lam_im) / den
    k_im = (a_im * lam_re - (a_re - 1.0) * lam_im) / den
    bb_re = k_re[..., None] * b_re - k_im[..., None] * b_im
    bb_im = k_re[..., None] * b_im + k_im[..., None] * b_re

    e_re = pw_re[:L, ..., None] * bb_re - pw_im[:L, ..., None] * bb_im
    e_im = pw_re[:L, ..., None] * bb_im + pw_im[:L, ..., None] * bb_re
    lag = (jnp.einsum("gop,sgpi->sgoi", c_re, e_re, precision=hi)
           - jnp.einsum("gop,sgpi->sgoi", c_im, e_im, precision=hi))
    I = S5_GROUP
    kcat = lag.reshape(L, N_SLABS, S5_SLAB, I, I)
    kcat = kcat.transpose(1, 0, 4, 2, 3).reshape(N_SLABS, L, I, S5_SLAB * I)

    f = jnp.stack([e_re[::-1], e_im[::-1]], axis=0)
    f = f.reshape(2, L, N_SLABS, S5_SLAB, P, I)
    fcat = f.transpose(2, 0, 1, 5, 3, 4).reshape(N_SLABS, 2, L, I, S5_SLAB * P)

    nr, ni = pw_re[1:, :, None, :], pw_im[1:, :, None, :]
    q = jnp.stack([c_re * nr - c_im * ni, -(c_re * ni + c_im * nr)], axis=0)
    q = q.reshape(2, L, N_SLABS, S5_SLAB, I, P)
    qcat = q.transpose(2, 0, 1, 5, 3, 4).reshape(N_SLABS, 2, L, P, S5_SLAB * I)

    al_re, al_im = pw_re[L], pw_im[L]
    a_chunk = jnp.concatenate([al_re.reshape(N_SLABS, 1, S5_SLAB * P),
                               al_im.reshape(N_SLABS, 1, S5_SLAB * P)], axis=2)
    return {"kcat": kcat, "fcat": fcat, "qcat": qcat, "a_chunk": a_chunk}


def _cast_kernel(w_ref, o_ref):
    o_ref[...] = w_ref[...].astype(BF16)


def _to_bf16(w):
    n, r, c = w.shape
    rb = min(r, CAST_BLOCK_BYTES // (4 * c))
    spec = pl.BlockSpec((1, rb, c), lambda i, j: (i, j, 0))
    return pl.pallas_call(
        _cast_kernel, grid=(n, r // rb), in_specs=[spec], out_specs=spec,
        out_shape=jax.ShapeDtypeStruct(w.shape, BF16), name="to_bf16")(w)


def kernel(x, mix_norm_g, ffn_norm_g, w_in, g_cq, w_uq, g_ckv, w_ukv, g_qn_mla, g_kn_mla,
           g_qn_moba, g_kn_moba, w_o, lam_re, lam_im, log_dt, b_re, b_im, c_re, c_im,
           d_skip, w_glu, w_ff1, w_ff2):
    B, S, D = x.shape
    depth = mix_norm_g.shape[0]
    assert D == D_MODEL and S % ATT_TILE == 0 and S % FFN_TILE == 0 and depth % 2 == 0
    assert S % min(PROJ_TILE, S) == 0 and PROJ_TILE % ATT_TILE == 0
    tabs = _rope_tables(S)
    row = lambda v: v.reshape(1, -1)
    w_ff1, w_ff2, w_glu, w_o = _to_bf16(w_ff1), _to_bf16(w_ff2), _to_bf16(w_glu), _to_bf16(w_o)
    for layer in range(0, depth, 2):
        i = layer // 2
        ap = _attn_params(w_in[i], g_cq[i], w_uq[i], g_ckv[i], w_ukv[i], g_qn_mla[i], g_kn_mla[i],
                          g_qn_moba[i], g_kn_moba[i])
        qt, k, vt = _attn_proj(x, row(mix_norm_g[layer]), ap, tabs)
        o = _flash(qt, k, vt)
        x, uc = _attn_out_ffn(x, o, _Layer(w_o, i), row(ffn_norm_g[layer]), _Layer(w_ff1, layer),
                              _Layer(w_ff2, layer), row(mix_norm_g[layer + 1]))
        sp = _s5_params(lam_re[i], lam_im[i], log_dt[i], b_re[i], b_im[i], c_re[i], c_im[i])
        yc = _s5_core(uc, sp)
        x = _s5_out_ffn(x, yc, row(mix_norm_g[layer + 1]), row(d_skip[i]), _Layer(w_glu, i),
                        row(ffn_norm_g[layer + 1]), _Layer(w_ff1, layer + 1),
                        _Layer(w_ff2, layer + 1))
    return x
```
